```python
import math
import jax, jax.numpy as jnp
from jax import lax
import numpy as np

D_MODEL = 1024
BATCH = 4
SEQ = 4096
DEPTH = 1

EPS = 1e-5
SSD_HEADS = 16
SSD_HEAD_DIM = 64
SSD_INNER = SSD_HEADS * SSD_HEAD_DIM
SSD_GROUPS = 2
SSD_HEADS_PER_GROUP = SSD_HEADS // SSD_GROUPS
SSD_STATE = 128
SSD_CONV = 4
SSD_CHUNK = 128
SSD_CONV_DIM = SSD_INNER + 2 * SSD_GROUPS * SSD_STATE
SSD_PROJ = SSD_INNER + SSD_CONV_DIM + SSD_HEADS
DA_HEADS = 8
DA_HEAD_DIM = 64
DA_V_DIM = 2 * DA_HEAD_DIM
DA_WIDTH = DA_HEADS * DA_V_DIM
DA_PROJ = 3 * DA_WIDTH
Q_BLOCK = 128
MIX_WIDTH = SSD_INNER + DA_WIDTH
IN_COLS = SSD_PROJ + DA_PROJ
D_FF = -(-8 * D_MODEL // (3 * 256)) * 256

kernel_name = "hymba_ssd_diffattn_block"


def rmsnorm(x, w):
    xf = x.astype(jnp.float32)
    y = xf * lax.rsqrt(jnp.mean(xf * xf, axis=-1, keepdims=True) + EPS)
    return (y * w.astype(jnp.float32)).astype(x.dtype)


def causal_dwconv(u, w, b):
    k = w.shape[0]
    y = lax.conv_general_dilated(u, w[:, None, :].astype(u.dtype), window_strides=(1,),
                                 padding=((k - 1, 0),), dimension_numbers=("NWC", "WIO", "NWC"),
                                 feature_group_count=u.shape[-1])
    return y + b


def segsum_exp(a):
    t = a.shape[-1]
    cs = jnp.cumsum(a, axis=-1)
    diff = cs[..., :, None] - cs[..., None, :]
    mask = jnp.tril(jnp.ones((t, t), dtype=bool))
    return jnp.exp(jnp.where(mask, diff, -jnp.inf))


def ssd_mixer(zxbcdt, conv_w, conv_b, dt_bias, a_log, d_skip, norm_w):
    bsz, seqlen, _ = zxbcdt.shape
    nc = seqlen // SSD_CHUNK
    g, e, l = SSD_GROUPS, SSD_HEADS_PER_GROUP, SSD_CHUNK
    z, xbc, dt = jnp.split(zxbcdt, [SSD_INNER, SSD_INNER + SSD_CONV_DIM], axis=-1)
    xbc = jax.nn.silu(causal_dwconv(xbc, conv_w, conv_b))
    xs, bm, cm = jnp.split(xbc, [SSD_INNER, SSD_INNER + SSD_GROUPS * SSD_STATE], axis=-1)
    dt = jax.nn.softplus((dt + dt_bias).astype(jnp.float32))
    a = -jnp.exp(a_log.astype(jnp.float32))
    dt5 = dt.reshape(bsz, nc, l, g, e)
    a_dt = (dt5 * a.reshape(g, e)).transpose(0, 3, 4, 1, 2)
    x5 = xs.reshape(bsz, nc, l, g, e, SSD_HEAD_DIM)
    xdt = x5 * dt5[..., None]
    bm = bm.reshape(bsz, nc, l, g, SSD_STATE)
    cm = cm.reshape(bsz, nc, l, g, SSD_STATE)
    a_cs = jnp.cumsum(a_dt, axis=-1)
    lmat = segsum_exp(a_dt)
    cb = jnp.einsum("bclgn,bcsgn->bcgls", cm, bm)
    y_diag = jnp.einsum("bcgls,bgecls,bcsgep->bclgep", cb, lmat, xdt)
    decay_states = jnp.exp(a_cs[..., -1:] - a_cs)
    states = jnp.einsum("bclgn,bgecl,bclgep->bcgepn", bm, decay_states, xdt)
    chunk_decay = jnp.exp(a_cs[..., -1])

    def step(carry, inp):
        st, dec = inp
        new = (carry * dec[..., None, None] + st).astype(carry.dtype)
        return new, carry

    init = jnp.zeros_like(states[:, 0])
    _, prev_states = lax.scan(step, init, (jnp.moveaxis(states, 1, 0), jnp.moveaxis(chunk_decay, -1, 0)))
    prev_states = jnp.moveaxis(prev_states, 0, 1)
    y_off = jnp.einsum("bclgn,bcgepn,bgecl->bclgep", cm, prev_states, jnp.exp(a_cs))
    y = y_diag + y_off + x5 * d_skip.astype(jnp.float32).reshape(g, e)[:, :, None]
    y = y.reshape(bsz, seqlen, SSD_INNER)
    gy = (y * jax.nn.silu(z.astype(jnp.float32))).reshape(bsz, seqlen, g, SSD_INNER // g)
    gy = gy * lax.rsqrt(jnp.mean(gy * gy, axis=-1, keepdims=True) + EPS)
    return (gy.reshape(bsz, seqlen, SSD_INNER) * norm_w.astype(jnp.float32)).astype(zxbcdt.dtype)


def diff_attention(qkv, lam_q1, lam_k1, lam_q2, lam_k2, subln_w, lambda_init):
    bsz, seqlen, _ = qkv.shape
    nqb = seqlen // Q_BLOCK
    q, k, v = jnp.split(qkv, 3, axis=-1)
    q = q.reshape(bsz, seqlen, DA_HEADS, 2, DA_HEAD_DIM).transpose(3, 0, 2, 1, 4) * (DA_HEAD_DIM ** -0.5)
    k = k.reshape(bsz, seqlen, DA_HEADS, 2, DA_HEAD_DIM).transpose(3, 0, 2, 1, 4)
    v = v.reshape(bsz, seqlen, DA_HEADS, DA_V_DIM).transpose(0, 2, 1, 3)
    f32 = jnp.float32
    lam = (jnp.exp(jnp.sum(lam_q1.astype(f32) * lam_k1.astype(f32)))
           - jnp.exp(jnp.sum(lam_q2.astype(f32) * lam_k2.astype(f32))) + lambda_init)
    qb = q.reshape(2, bsz, DA_HEADS, nqb, Q_BLOCK, DA_HEAD_DIM).transpose(3, 0, 1, 2, 4, 5)
    kpos = jnp.arange(seqlen)

    def block(args):
        qi, i = args
        s = jnp.einsum("mbhqd,mbhkd->mbhqk", qi, k).astype(f32)
        qpos = i * Q_BLOCK + jnp.arange(Q_BLOCK)
        mask = kpos[None, :] <= qpos[:, None]
        p = jax.nn.softmax(jnp.where(mask, s, -jnp.inf), axis=-1)
        w = p[0] - lam * p[1]
        return jnp.einsum("bhqk,bhkv->bhqv", w.astype(v.dtype), v)

    o = lax.map(block, (qb, jnp.arange(nqb)))
    o = o.transpose(1, 0, 3, 2, 4).reshape(bsz, seqlen, DA_HEADS, DA_V_DIM)
    o = rmsnorm(o, subln_w) * (1.0 - lambda_init)
    return o.reshape(bsz, seqlen, DA_WIDTH)


def setup_inputs(seed: int = 0) -> dict:
    key = jax.random.key(seed)
    ks = jax.random.split(key, 24)
    f32 = jnp.float32
    nrm = lambda k, shape, scale: jax.random.normal(k, shape, f32) * scale
    gain = lambda k, shape: 1.0 + 0.02 * jax.random.normal(k, shape, f32)
    dt = jnp.exp(jax.random.uniform(ks[5], (DEPTH, SSD_HEADS), f32) * (math.log(0.1) - math.log(0.001)) + math.log(0.001))
    dt = jnp.maximum(dt, 1e-4)
    return {
        "x": jax.random.normal(ks[0], (BATCH, SEQ, D_MODEL), f32),
        "mix_norm_w": gain(ks[1], (DEPTH, D_MODEL)),
        "w_in": nrm(ks[2], (DEPTH, D_MODEL, IN_COLS), D_MODEL ** -0.5),
        "conv_w": nrm(ks[3], (DEPTH, SSD_CONV, SSD_CONV_DIM), SSD_CONV ** -0.5),
        "conv_b": nrm(ks[4], (DEPTH, SSD_CONV_DIM), 0.02),
        "dt_bias": dt + jnp.log(-jnp.expm1(-dt)),
        "a_log": jnp.log(jax.random.uniform(ks[6], (DEPTH, SSD_HEADS), f32, 1.0, 16.0)),
        "d_skip": 1.0 + 0.1 * jax.random.normal(ks[7], (DEPTH, SSD_HEADS), f32),
        "ssd_norm_w": gain(ks[8], (DEPTH, SSD_INNER)),
        "lam_q1": nrm(ks[9], (DEPTH, DA_HEAD_DIM), 0.1),
        "lam_k1": nrm(ks[10], (DEPTH, DA_HEAD_DIM), 0.1),
        "lam_q2": nrm(ks[11], (DEPTH, DA_HEAD_DIM), 0.1),
        "lam_k2": nrm(ks[12], (DEPTH, DA_HEAD_DIM), 0.1),
        "subln_w": gain(ks[13], (DEPTH, DA_V_DIM)),
        "w_out": nrm(ks[14], (DEPTH, MIX_WIDTH, D_MODEL), MIX_WIDTH ** -0.5),
        "ffn_norm_w": gain(ks[15], (DEPTH, D_MODEL)),
        "w_gate": nrm(ks[16], (DEPTH, D_MODEL, D_FF), D_MODEL ** -0.5),
        "w_up": nrm(ks[17], (DEPTH, D_MODEL, D_FF), D_MODEL ** -0.5),
        "w_down": nrm(ks[18], (DEPTH, D_FF, D_MODEL), D_FF ** -0.5),
        "final_norm_w": gain(ks[19], (D_MODEL,)),
    }


def reference(x, mix_norm_w, w_in, conv_w, conv_b, dt_bias, a_log, d_skip, ssd_norm_w,
              lam_q1, lam_k1, lam_q2, lam_k2, subln_w, w_out, ffn_norm_w, w_gate, w_up, w_down,
              final_norm_w):
    h = x
    for layer in range(DEPTH):
        lambda_init = 0.8 - 0.6 * math.exp(-0.3 * layer)
        n = rmsnorm(h, mix_norm_w[layer])
        proj = n @ w_in[layer]
        ssd_in, da_in = jnp.split(proj, [SSD_PROJ], axis=-1)
        y_ssd = ssd_mixer(ssd_in, conv_w[layer], conv_b[layer], dt_bias[layer], a_log[layer],
                          d_skip[layer], ssd_norm_w[layer])
        y_da = diff_attention(da_in, lam_q1[layer], lam_k1[layer], lam_q2[layer], lam_k2[layer],
                              subln_w[layer], lambda_init)
        h = h + jnp.concatenate([y_ssd, y_da], axis=-1) @ w_out[layer]
        n2 = rmsnorm(h, ffn_norm_w[layer])
        h = h + (jax.nn.silu(n2 @ w_gate[layer]) * (n2 @ w_up[layer])) @ w_down[layer]
    return rmsnorm(h, final_norm_w)
```

```python
import functools
import math

import jax
import jax.numpy as jnp
from jax import lax
from jax.experimental import pallas as pl
from jax.experimental.pallas import tpu as pltpu

F32 = jnp.float32
BF16 = jnp.bfloat16

EPS = 1e-5
D_MODEL = 1024
SSD_HEADS = 16
SSD_HEAD_DIM = 64
SSD_INNER = SSD_HEADS * SSD_HEAD_DIM
SSD_GROUPS = 2
SSD_GROUP_WIDTH = SSD_INNER // SSD_GROUPS
SSD_STATE = 128
SSD_CONV = 4
SSD_CHUNK = 128
SSD_BC = SSD_GROUPS * SSD_STATE
SSD_CONV_DIM = SSD_INNER + 2 * SSD_BC
DA_HEADS = 8
DA_HEAD_DIM = 64
DA_V_DIM = 2 * DA_HEAD_DIM
DA_WIDTH = DA_HEADS * DA_V_DIM
D_FF = 2816
LAMBDA_INIT = 0.8 - 0.6 * math.exp(-0.3 * 0)

LANES = 128
CONV_HALO = 8
VMEM_LIMIT = 56 * 1024 * 1024

TM_PROJ = 512
TQ = 256
TK = 256
FF_CHUNK = 1408


def _const_spec(shape):
    nd = len(shape)
    return pl.BlockSpec(shape, lambda *_: (0,) * nd)


def _rms_scale(xf):
    return lax.rsqrt(jnp.mean(xf * xf, axis=-1, keepdims=True) + EPS)


def _dot(a, b):
    return jnp.dot(a, b, preferred_element_type=F32)


def _dot_nt(a, b):
    return lax.dot_general(a, b, (((1,), (1,)), ((), ())), preferred_element_type=F32)


def _split3(a):
    hi = a.astype(BF16)
    r1 = a - hi.astype(F32)
    mid = r1.astype(BF16)
    lo = (r1 - mid.astype(F32)).astype(BF16)
    return hi, mid, lo


def _sigmoid(x):
    return 1.0 / (1.0 + jnp.exp(-x))


def _in_proj_kernel(x_ref, nw_ref, wa_ref, wt_ref, z_ref, xbc_ref, dt_ref, k_ref, qt_ref, vt_ref):
    xf = x_ref[...]
    xn = (xf * _rms_scale(xf) * nw_ref[...]).astype(BF16)
    c0 = 0
    for ref, width in ((z_ref, SSD_INNER), (xbc_ref, SSD_CONV_DIM), (dt_ref, LANES), (k_ref, DA_WIDTH)):
        ref[...] = _dot(xn, wa_ref[:, c0:c0 + width]).astype(ref.dtype)
        c0 += width
    qt_ref[...] = _dot_nt(wt_ref[0:DA_WIDTH, :], xn).astype(BF16)
    vt_ref[...] = _dot_nt(wt_ref[DA_WIDTH:2 * DA_WIDTH, :], xn).astype(BF16)


def _in_proj(x2, nw, wa, wt):
    t = x2.shape[0]
    tm = TM_PROJ
    row = lambda w: pl.BlockSpec((tm, w), lambda i: (i, 0))
    col = pl.BlockSpec((DA_WIDTH, tm), lambda i: (0, i))
    return pl.pallas_call(
        _in_proj_kernel,
        grid=(t // tm,),
        in_specs=[row(D_MODEL), _const_spec(nw.shape), _const_spec(wa.shape), _const_spec(wt.shape)],
        out_specs=[row(SSD_INNER), row(SSD_CONV_DIM), row(LANES), row(DA_WIDTH), col, col],
        out_shape=[
            jax.ShapeDtypeStruct((t, SSD_INNER), BF16),
            jax.ShapeDtypeStruct((t, SSD_CONV_DIM), BF16),
            jax.ShapeDtypeStruct((t, LANES), F32),
            jax.ShapeDtypeStruct((t, DA_WIDTH), BF16),
            jax.ShapeDtypeStruct((DA_WIDTH, t), BF16),
            jax.ShapeDtypeStruct((DA_WIDTH, t), BF16),
        ],
        compiler_params=pltpu.CompilerParams(
            dimension_semantics=("arbitrary",), vmem_limit_bytes=VMEM_LIMIT),
        name="in_proj",
    )(x2, nw, wa, wt)


def _ssd_kernel(z_ref, xbc_ref, dt_ref, cw_ref, cb_ref, dtb_ref, alog_ref, dskip_ref, nw_ref,
                tri3_ref, exp3_ref, y_ref, ext_ref, conv_ref, state_ref, yacc_ref):
    L = SSD_CHUNK
    c = pl.program_id(1)

    @pl.when(c == 0)
    def _():
        state_ref[...] = jnp.zeros_like(state_ref)
        ext_ref[0:CONV_HALO, :] = jnp.zeros((CONV_HALO, SSD_CONV_DIM), F32)

    @pl.when(c > 0)
    def _():
        ext_ref[0:CONV_HALO, :] = ext_ref[L:L + CONV_HALO, :]

    ext_ref[CONV_HALO:CONV_HALO + L, :] = xbc_ref[...].astype(F32)

    for c0 in range(0, SSD_CONV_DIM, 512):
        acc = jnp.broadcast_to(cb_ref[:, c0:c0 + 512], (L, 512))
        for j in range(SSD_CONV):
            r0 = CONV_HALO - (SSD_CONV - 1) + j
            acc = acc + cw_ref[j:j + 1, c0:c0 + 512] * ext_ref[r0:r0 + L, c0:c0 + 512]
        conv_ref[:, c0:c0 + 512] = acc * _sigmoid(acc)

    dtr = dt_ref[...] + dtb_ref[...]
    dtv = jnp.maximum(dtr, 0.0) + jnp.log1p(jnp.exp(-jnp.abs(dtr)))
    adt = dtv * (-jnp.exp(alog_ref[...]))
    cs = _dot(tri3_ref[...], jnp.concatenate(_split3(adt), axis=0))
    cs_t = cs.T
    dt_t = dtv.T
    cs_last = cs[L - 1:L, :]
    ecs = jnp.exp(cs)
    w_state = dtv * jnp.exp(cs_last - cs)

    def expand(a):
        return _dot(jnp.concatenate(_split3(a), axis=1), exp3_ref[...])

    ecs_x = expand(ecs)
    wst_x = expand(w_state)

    row = lax.broadcasted_iota(jnp.int32, (L, L), 0)
    colm = lax.broadcasted_iota(jnp.int32, (L, L), 1)
    tril = row >= colm
    lane = lax.broadcasted_iota(jnp.int32, (L, LANES), 1)
    lo_half = lane < SSD_HEAD_DIM

    for g in range(SSD_GROUPS):
        gx = g * SSD_GROUP_WIDTH
        bm = conv_ref[:, SSD_INNER + g * SSD_STATE:SSD_INNER + (g + 1) * SSD_STATE]
        cm = conv_ref[:, SSD_INNER + SSD_BC + g * SSD_STATE:SSD_INNER + SSD_BC + (g + 1) * SSD_STATE]
        cm16 = cm.astype(BF16)
        cb = _dot_nt(cm16, bm.astype(BF16))
        xs_g = conv_ref[:, gx:gx + SSD_GROUP_WIDTH]

        y_off = _dot(cm16, state_ref[:, gx:gx + SSD_GROUP_WIDTH].astype(BF16)) * ecs_x[:, gx:gx + SSD_GROUP_WIDTH]
        yacc_ref[:, gx:gx + SSD_GROUP_WIDTH] = y_off + xs_g * dskip_ref[:, gx:gx + SSD_GROUP_WIDTH]

        for pair in range(SSD_HEADS // SSD_GROUPS // 2):
            ms = []
            for h in (g * 8 + 2 * pair, g * 8 + 2 * pair + 1):
                seg = cs[:, h:h + 1] - cs_t[h:h + 1, :]
                dec = jnp.exp(jnp.where(tril, seg, -jnp.inf))
                ms.append((cb * dec * dt_t[h:h + 1, :]).astype(BF16))
            x_pair = conv_ref[:, gx + pair * LANES:gx + (pair + 1) * LANES]
            x_blk = jnp.concatenate(
                [jnp.where(lo_half, x_pair, 0.0), jnp.where(lo_half, 0.0, x_pair)], axis=0).astype(BF16)
            sl = slice(gx + pair * LANES, gx + (pair + 1) * LANES)
            yacc_ref[:, sl] = yacc_ref[:, sl] + _dot(jnp.concatenate(ms, axis=1), x_blk)

        xd = (xs_g * wst_x[:, gx:gx + SSD_GROUP_WIDTH]).astype(BF16)
        contrib = _dot(bm.T.astype(BF16), xd)
        state_ref[:, gx:gx + SSD_GROUP_WIDTH] = (
            state_ref[:, gx:gx + SSD_GROUP_WIDTH] * ecs_x[L - 1:L, gx:gx + SSD_GROUP_WIDTH] + contrib)

        zf = z_ref[:, gx:gx + SSD_GROUP_WIDTH].astype(F32)
        gy = yacc_ref[:, gx:gx + SSD_GROUP_WIDTH] * (zf * _sigmoid(zf))
        y_ref[:, gx:gx + SSD_GROUP_WIDTH] = (
            gy * _rms_scale(gy) * nw_ref[:, gx:gx + SSD_GROUP_WIDTH]).astype(y_ref.dtype)


def _ssd(z, xbc, dt, cw, cb, dtb, alog, dskip_x, nw, tri3, exp3, bsz, seqlen):
    nc = seqlen // SSD_CHUNK
    L = SSD_CHUNK
    row = lambda w: pl.BlockSpec((L, w), lambda b, c: (b * nc + c, 0))
    consts = (cw, cb, dtb, alog, dskip_x, nw, tri3, exp3)
    return pl.pallas_call(
        _ssd_kernel,
        grid=(bsz, nc),
        in_specs=[row(SSD_INNER), row(SSD_CONV_DIM), row(LANES)] + [_const_spec(a.shape) for a in consts],
        out_specs=row(SSD_INNER),
        out_shape=jax.ShapeDtypeStruct((bsz * seqlen, SSD_INNER), BF16),
        scratch_shapes=[
            pltpu.VMEM((CONV_HALO + L, SSD_CONV_DIM), F32),
            pltpu.VMEM((L, SSD_CONV_DIM), F32),
            pltpu.VMEM((SSD_STATE, SSD_INNER), F32),
            pltpu.VMEM((L, SSD_INNER), F32),
        ],
        compiler_params=pltpu.CompilerParams(
            dimension_semantics=("arbitrary", "arbitrary"), vmem_limit_bytes=VMEM_LIMIT),
        name="ssd",
    )(z, xbc, dt, *consts)


def _attn_kernel(qt_ref, k_ref, vt_ref, lq1_ref, lk1_ref, lq2_ref, lk2_ref, sw_ref, o_ref,
                 qs_ref, acc_ref, m_ref, l_ref):
    i = pl.program_id(2)

    q = qt_ref[...] * jnp.asarray(DA_HEAD_DIM ** -0.5, BF16)
    d_idx = lax.broadcasted_iota(jnp.int32, (DA_V_DIM, TQ), 0)
    zero = jnp.zeros_like(q)
    qs_ref[:, 0:TQ] = jnp.where(d_idx < DA_HEAD_DIM, q, zero)
    qs_ref[:, TQ:2 * TQ] = jnp.where(d_idx < DA_HEAD_DIM, zero, q)
    m_ref[...] = jnp.full(m_ref.shape, -jnp.inf, F32)
    l_ref[...] = jnp.zeros_like(l_ref)
    acc_ref[...] = jnp.zeros_like(acc_ref)

    def step(j, masked):
        off = pl.multiple_of(j * TK, TK)
        s = _dot(k_ref[pl.ds(off, TK), :], qs_ref[...])
        if masked:
            key = lax.broadcasted_iota(jnp.int32, s.shape, 0)
            qry = lax.broadcasted_iota(jnp.int32, s.shape, 1) & (TQ - 1)
            s = jnp.where(key <= qry, s, -jnp.inf)
        m_prev = m_ref[...]
        m_new = jnp.maximum(m_prev, jnp.max(s, axis=0, keepdims=True))
        alpha = jnp.exp(m_prev - m_new)
        p = jnp.exp(s - m_new)
        l_ref[...] = alpha * l_ref[...] + jnp.sum(p, axis=0, keepdims=True)
        acc_ref[...] = alpha * acc_ref[...] + _dot(vt_ref[:, pl.ds(off, TK)], p.astype(BF16))
        m_ref[...] = m_new

    def body(j, carry):
        step(j, masked=False)
        return carry

    lax.fori_loop(0, i, body, 0)
    step(i, masked=True)

    lam = (jnp.exp(jnp.sum(lq1_ref[...] * lk1_ref[...], axis=1, keepdims=True))
           - jnp.exp(jnp.sum(lq2_ref[...] * lk2_ref[...], axis=1, keepdims=True)) + LAMBDA_INIT)
    o_all = acc_ref[...] * (1.0 / l_ref[...])
    o = o_all[:, 0:TQ] - lam * o_all[:, TQ:2 * TQ]
    on = o * lax.rsqrt(jnp.mean(o * o, axis=0, keepdims=True) + EPS)
    o_ref[...] = ((on.T * sw_ref[...]) * (1.0 - LAMBDA_INIT)).astype(o_ref.dtype)


def _attention(qt, k, vt, lq1, lk1, lq2, lk2, sw, bsz, seqlen):
    assert TQ == TK
    nq = seqlen // TQ
    small = (lq1, lk1, lq2, lk2, sw)
    return pl.pallas_call(
        _attn_kernel,
        grid=(bsz, DA_HEADS, nq),
        in_specs=[
            pl.BlockSpec((DA_V_DIM, TQ), lambda b, h, i: (h, b * nq + i)),
            pl.BlockSpec((seqlen, DA_V_DIM), lambda b, h, i: (b, h)),
            pl.BlockSpec((DA_V_DIM, seqlen), lambda b, h, i: (h, b)),
        ] + [_const_spec(a.shape) for a in small],
        out_specs=pl.BlockSpec((TQ, DA_V_DIM), lambda b, h, i: (b * nq + i, h)),
        out_shape=jax.ShapeDtypeStruct((bsz * seqlen, DA_WIDTH), BF16),
        scratch_shapes=[
            pltpu.VMEM((DA_V_DIM, 2 * TQ), BF16),
            pltpu.VMEM((DA_V_DIM, 2 * TQ), F32),
            pltpu.VMEM((1, 2 * TQ), F32),
            pltpu.VMEM((1, 2 * TQ), F32),
        ],
        compiler_params=pltpu.CompilerParams(
            dimension_semantics=("arbitrary", "arbitrary", "arbitrary"), vmem_limit_bytes=VMEM_LIMIT),
        name="diff_attn",
    )(qt, k, vt, *small)


def _out_proj_kernel(x_ref, ys_ref, ya_ref, wo_ref, nw_ref, h_ref, n_ref):
    h = x_ref[...] + _dot(ys_ref[...], wo_ref[0:SSD_INNER, :]) + _dot(ya_ref[...], wo_ref[SSD_INNER:, :])
    h_ref[...] = h
    n_ref[...] = (h * _rms_scale(h) * nw_ref[...]).astype(BF16)


def _out_proj(x2, ys, ya, wo, nw):
    t = x2.shape[0]
    tm = TM_PROJ
    row = pl.BlockSpec((tm, D_MODEL), lambda i: (i, 0))
    return pl.pallas_call(
        _out_proj_kernel,
        grid=(t // tm,),
        in_specs=[row, row, row, _const_spec(wo.shape), _const_spec(nw.shape)],
        out_specs=[row, row],
        out_shape=[jax.ShapeDtypeStruct((t, D_MODEL), F32), jax.ShapeDtypeStruct((t, D_MODEL), BF16)],
        compiler_params=pltpu.CompilerParams(
            dimension_semantics=("arbitrary",), vmem_limit_bytes=VMEM_LIMIT),
        name="out_proj",
    )(x2, ys, ya, wo, nw)


def _ffn_kernel(h_ref, n_ref, wg_ref, wu_ref, wd_ref, fw_ref, o_ref):
    n2 = n_ref[...]
    acc = h_ref[...]
    for f0 in range(0, D_FF, FF_CHUNK):
        gate = _dot(n2, wg_ref[:, f0:f0 + FF_CHUNK])
        up = _dot(n2, wu_ref[:, f0:f0 + FF_CHUNK])
        act = (gate * _sigmoid(gate) * up).astype(BF16)
        acc = acc + _dot(act, wd_ref[f0:f0 + FF_CHUNK, :])
    o_ref[...] = acc * _rms_scale(acc) * fw_ref[...]


def _ffn(h, n2, wg, wu, wd, fw):
    t = h.shape[0]
    tm = TM_PROJ
    row = pl.BlockSpec((tm, D_MODEL), lambda i: (i, 0))
    return pl.pallas_call(
        _ffn_kernel,
        grid=(t // tm,),
        in_specs=[row, row, _const_spec(wg.shape), _const_spec(wu.shape), _const_spec(wd.shape),
                  _const_spec(fw.shape)],
        out_specs=row,
        out_shape=jax.ShapeDtypeStruct((t, D_MODEL), F32),
        compiler_params=pltpu.CompilerParams(
            dimension_semantics=("arbitrary",), vmem_limit_bytes=VMEM_LIMIT),
        name="ffn",
    )(h, n2, wg, wu, wd, fw)


def _pad_lanes(v, fill=0.0):
    return jnp.pad(v.astype(F32), (0, LANES - v.shape[0]), constant_values=fill)[None, :]


def kernel(x, mix_norm_w, w_in, conv_w, conv_b, dt_bias, a_log, d_skip, ssd_norm_w, lam_q1, lam_k1, lam_q2,
           lam_k2, subln_w, w_out, ffn_norm_w, w_gate, w_up, w_down, final_norm_w):
    bsz, seqlen, _ = x.shape
    x2 = x.reshape(bsz * seqlen, D_MODEL)

    w = w_in[0]
    o_dt = SSD_INNER + SSD_CONV_DIM
    o_q = o_dt + SSD_HEADS
    w_dt = jnp.pad(w[:, o_dt:o_q], ((0, 0), (0, LANES - SSD_HEADS)))
    wa = jnp.concatenate([w[:, :o_dt], w_dt, w[:, o_q + DA_WIDTH:o_q + 2 * DA_WIDTH]], axis=1).astype(BF16)
    wt = jnp.concatenate([w[:, o_q:o_q + DA_WIDTH], w[:, o_q + 2 * DA_WIDTH:]], axis=1).T.astype(BF16)

    z, xbc, dt, k, qt, vt = _in_proj(x2, mix_norm_w[0][None, :], wa, wt)

    idx = jnp.arange(SSD_CHUNK)
    tri = (idx[:, None] >= idx[None, :]).astype(BF16)
    tri3 = jnp.concatenate([tri, tri, tri], axis=1)
    sel = (jnp.arange(LANES)[:, None] == (jnp.arange(SSD_INNER)[None, :] // SSD_HEAD_DIM)).astype(BF16)
    exp3 = jnp.concatenate([sel, sel, sel], axis=0)
    dskip_x = jnp.repeat(d_skip[0].astype(F32), SSD_HEAD_DIM)[None, :]

    y_ssd = _ssd(z, xbc, dt, conv_w[0], conv_b[0][None, :], _pad_lanes(dt_bias[0]), _pad_lanes(a_log[0]),
                 dskip_x, ssd_norm_w[0][None, :], tri3, exp3, bsz, seqlen)
    y_da = _attention(qt, k, vt, lam_q1[0][None, :], lam_k1[0][None, :], lam_q2[0][None, :],
                      lam_k2[0][None, :], subln_w[0][None, :], bsz, seqlen)

    h, n2 = _out_proj(x2, y_ssd, y_da, w_out[0].astype(BF16), ffn_norm_w[0][None, :])
    out = _ffn(h, n2, w_gate[0].astype(BF16), w_up[0].astype(BF16), w_down[0].astype(BF16),
               final_norm_w[None, :])
    return out.reshape(bsz, seqlen, D_MODEL)
```

```python
import functools
import math

import jax
import jax.numpy as jnp
from jax import lax
from jax.experimental import pallas as pl
from jax.experimental.pallas import tpu as pltpu

F32 = jnp.float32
BF16 = jnp.bfloat16

EPS = 1e-5
D_MODEL = 1024
SSD_HEADS = 16
SSD_HEAD_DIM = 64
SSD_INNER = SSD_HEADS * SSD_HEAD_DIM
SSD_GROUPS = 2
SSD_GROUP_WIDTH = SSD_INNER // SSD_GROUPS
SSD_STATE = 128
SSD_CONV = 4
SSD_CHUNK = 128
SSD_BC = SSD_GROUPS * SSD_STATE
SSD_CONV_DIM = SSD_INNER + 2 * SSD_BC
DA_HEADS = 8
DA_HEAD_DIM = 64
DA_V_DIM = 2 * DA_HEAD_DIM
DA_WIDTH = DA_HEADS * DA_V_DIM
D_FF = 2816
LAMBDA_INIT = 0.8 - 0.6 * math.exp(-0.3 * 0)

LANES = 128
CONV_HALO = 8
VMEM_LIMIT = 56 * 1024 * 1024

TM_PROJ = 512
TQ = 256
TK = 256
FF_CHUNK = 1408
HEADS_PER_STEP = 8
ONES_ROWS = 16
ACC_ROWS = DA_V_DIM + ONES_ROWS
LOG2E = math.log2(math.e)


def _const_spec(shape):
    nd = len(shape)
    return pl.BlockSpec(shape, lambda *_: (0,) * nd)


def _rms_scale(xf):
    return lax.rsqrt(jnp.mean(xf * xf, axis=-1, keepdims=True) + EPS)


def _dot(a, b):
    return jnp.dot(a, b, preferred_element_type=F32)


def _dot_nt(a, b):
    return lax.dot_general(a, b, (((1,), (1,)), ((), ())), preferred_element_type=F32)


def _split3(a):
    hi = a.astype(BF16)
    r1 = a - hi.astype(F32)
    mid = r1.astype(BF16)
    lo = (r1 - mid.astype(F32)).astype(BF16)
    return hi, mid, lo


def _sigmoid(x):
    return 1.0 / (1.0 + jnp.exp(-x))


def _in_proj_kernel(x_ref, nw_ref, wa_ref, wt_ref, z_ref, xbc_ref, dt_ref, k_ref, qt_ref, vt_ref):
    xf = x_ref[...]
    xn = (xf * _rms_scale(xf) * nw_ref[...]).astype(BF16)
    c0 = 0
    for ref, width in ((z_ref, SSD_INNER), (xbc_ref, SSD_CONV_DIM), (dt_ref, LANES), (k_ref, DA_WIDTH)):
        ref[...] = _dot(xn, wa_ref[:, c0:c0 + width]).astype(ref.dtype)
        c0 += width
    qt_ref[...] = _dot_nt(wt_ref[0:DA_WIDTH, :], xn).astype(BF16)
    vt_ref[...] = _dot_nt(wt_ref[DA_WIDTH:2 * DA_WIDTH, :], xn).astype(BF16)


def _in_proj(x2, nw, wa, wt):
    t = x2.shape[0]
    tm = TM_PROJ
    row = lambda w: pl.BlockSpec((tm, w), lambda i: (i, 0))
    col = pl.BlockSpec((DA_WIDTH, tm), lambda i: (0, i))
    return pl.pallas_call(
        _in_proj_kernel,
        grid=(t // tm,),
        in_specs=[row(D_MODEL), _const_spec(nw.shape), _const_spec(wa.shape), _const_spec(wt.shape)],
        out_specs=[row(SSD_INNER), row(SSD_CONV_DIM), row(LANES), row(DA_WIDTH), col, col],
        out_shape=[
            jax.ShapeDtypeStruct((t, SSD_INNER), BF16),
            jax.ShapeDtypeStruct((t, SSD_CONV_DIM), BF16),
            jax.ShapeDtypeStruct((t, LANES), F32),
            jax.ShapeDtypeStruct((t, DA_WIDTH), BF16),
            jax.ShapeDtypeStruct((DA_WIDTH, t), BF16),
            jax.ShapeDtypeStruct((DA_WIDTH, t), BF16),
        ],
        compiler_params=pltpu.CompilerParams(
            dimension_semantics=("arbitrary",), vmem_limit_bytes=VMEM_LIMIT),
        name="in_proj",
    )(x2, nw, wa, wt)


def _ssd_kernel(z_ref, xbc_ref, dt_ref, cw_ref, cb_ref, dtb_ref, alog_ref, dskip_ref, nw_ref,
                tri3_ref, exp3_ref, y_ref, ext_ref, conv_ref, state_ref, yacc_ref):
    L = SSD_CHUNK
    c = pl.program_id(1)

    @pl.when(c == 0)
    def _():
        state_ref[...] = jnp.zeros_like(state_ref)
        ext_ref[0:CONV_HALO, :] = jnp.zeros((CONV_HALO, SSD_CONV_DIM), F32)

    @pl.when(c > 0)
    def _():
        ext_ref[0:CONV_HALO, :] = ext_ref[L:L + CONV_HALO, :]

    ext_ref[CONV_HALO:CONV_HALO + L, :] = xbc_ref[...].astype(F32)

    for c0 in range(0, SSD_CONV_DIM, 512):
        acc = jnp.broadcast_to(cb_ref[:, c0:c0 + 512], (L, 512))
        for j in range(SSD_CONV):
            r0 = CONV_HALO - (SSD_CONV - 1) + j
            acc = acc + cw_ref[j:j + 1, c0:c0 + 512] * ext_ref[r0:r0 + L, c0:c0 + 512]
        conv_ref[:, c0:c0 + 512] = acc * _sigmoid(acc)

    dtr = dt_ref[...] + dtb_ref[...]
    dtv = jnp.maximum(dtr, 0.0) + jnp.log1p(jnp.exp(-jnp.abs(dtr)))
    adt = dtv * (-jnp.exp(alog_ref[...]))
    cs = _dot(tri3_ref[...], jnp.concatenate(_split3(adt), axis=0))
    cs_t = cs.T
    dt_t = dtv.T
    cs_last = cs[L - 1:L, :]
    ecs = jnp.exp(cs)
    w_state = dtv * jnp.exp(cs_last - cs)

    def expand(a):
        return _dot(jnp.concatenate(_split3(a), axis=1), exp3_ref[...])

    ecs_x = expand(ecs)
    wst_x = expand(w_state)

    row = lax.broadcasted_iota(jnp.int32, (L, L), 0)
    colm = lax.broadcasted_iota(jnp.int32, (L, L), 1)
    tril = row >= colm
    lane = lax.broadcasted_iota(jnp.int32, (L, LANES), 1)
    lo_half = lane < SSD_HEAD_DIM

    for g in range(SSD_GROUPS):
        gx = g * SSD_GROUP_WIDTH
        bm = conv_ref[:, SSD_INNER + g * SSD_STATE:SSD_INNER + (g + 1) * SSD_STATE]
        cm = conv_ref[:, SSD_INNER + SSD_BC + g * SSD_STATE:SSD_INNER + SSD_BC + (g + 1) * SSD_STATE]
        cm16 = cm.astype(BF16)
        cb = _dot_nt(cm16, bm.astype(BF16))
        xs_g = conv_ref[:, gx:gx + SSD_GROUP_WIDTH]

        y_off = _dot(cm16, state_ref[:, gx:gx + SSD_GROUP_WIDTH].astype(BF16)) * ecs_x[:, gx:gx + SSD_GROUP_WIDTH]
        yacc_ref[:, gx:gx + SSD_GROUP_WIDTH] = y_off + xs_g * dskip_ref[:, gx:gx + SSD_GROUP_WIDTH]

        for pair in range(SSD_HEADS // SSD_GROUPS // 2):
            ms = []
            for h in (g * 8 + 2 * pair, g * 8 + 2 * pair + 1):
                seg = cs[:, h:h + 1] - cs_t[h:h + 1, :]
                dec = jnp.exp(jnp.where(tril, seg, -jnp.inf))
                ms.append((cb * dec * dt_t[h:h + 1, :]).astype(BF16))
            x_pair = conv_ref[:, gx + pair * LANES:gx + (pair + 1) * LANES]
            x_blk = jnp.concatenate(
                [jnp.where(lo_half, x_pair, 0.0), jnp.where(lo_half, 0.0, x_pair)], axis=0).astype(BF16)
            sl = slice(gx + pair * LANES, gx + (pair + 1) * LANES)
            yacc_ref[:, sl] = yacc_ref[:, sl] + _dot(jnp.concatenate(ms, axis=1), x_blk)

        xd = (xs_g * wst_x[:, gx:gx + SSD_GROUP_WIDTH]).astype(BF16)
        contrib = _dot(bm.T.astype(BF16), xd)
        state_ref[:, gx:gx + SSD_GROUP_WIDTH] = (
            state_ref[:, gx:gx + SSD_GROUP_WIDTH] * ecs_x[L - 1:L, gx:gx + SSD_GROUP_WIDTH] + contrib)

        zf = z_ref[:, gx:gx + SSD_GROUP_WIDTH].astype(F32)
        gy = yacc_ref[:, gx:gx + SSD_GROUP_WIDTH] * (zf * _sigmoid(zf))
        y_ref[:, gx:gx + SSD_GROUP_WIDTH] = (
            gy * _rms_scale(gy) * nw_ref[:, gx:gx + SSD_GROUP_WIDTH]).astype(y_ref.dtype)


def _ssd(z, xbc, dt, cw, cb, dtb, alog, dskip_x, nw, tri3, exp3, bsz, seqlen):
    nc = seqlen // SSD_CHUNK
    L = SSD_CHUNK
    row = lambda w: pl.BlockSpec((L, w), lambda b, c: (b * nc + c, 0))
    consts = (cw, cb, dtb, alog, dskip_x, nw, tri3, exp3)
    return pl.pallas_call(
        _ssd_kernel,
        grid=(bsz, nc),
        in_specs=[row(SSD_INNER), row(SSD_CONV_DIM), row(LANES)] + [_const_spec(a.shape) for a in consts],
        out_specs=row(SSD_INNER),
        out_shape=jax.ShapeDtypeStruct((bsz * seqlen, SSD_INNER), BF16),
        scratch_shapes=[
            pltpu.VMEM((CONV_HALO + L, SSD_CONV_DIM), F32),
            pltpu.VMEM((L, SSD_CONV_DIM), F32),
            pltpu.VMEM((SSD_STATE, SSD_INNER), F32),
            pltpu.VMEM((L, SSD_INNER), F32),
        ],
        compiler_params=pltpu.CompilerParams(
            dimension_semantics=("arbitrary", "arbitrary"), vmem_limit_bytes=VMEM_LIMIT),
        name="ssd",
    )(z, xbc, dt, *consts)


def _attn_kernel(qt_ref, k_ref, vt_ref, lq1_ref, lk1_ref, lq2_ref, lk2_ref, sw_ref, o_ref,
                 qs_ref, acc_ref, m_ref, s_ref):
    i = pl.program_id(2)
    d_idx = lax.broadcasted_iota(jnp.int32, (DA_V_DIM, TQ), 0)
    ones_rows = jnp.ones((ONES_ROWS, TK), BF16)

    for g in range(HEADS_PER_STEP):
        rows = slice(g * DA_V_DIM, (g + 1) * DA_V_DIM)
        q = (qt_ref[rows, :].astype(F32) * (DA_HEAD_DIM ** -0.5 * LOG2E)).astype(BF16)
        zero = jnp.zeros_like(q)
        qs_ref[rows, 0:TQ] = jnp.where(d_idx < DA_HEAD_DIM, q, zero)
        qs_ref[rows, TQ:2 * TQ] = jnp.where(d_idx < DA_HEAD_DIM, zero, q)
    m_ref[...] = jnp.full(m_ref.shape, -jnp.inf, F32)
    acc_ref[...] = jnp.zeros_like(acc_ref)

    def scores(g, j):
        rows = slice(g * DA_V_DIM, (g + 1) * DA_V_DIM)
        off = pl.multiple_of(j * TK, TK)
        s_ref[g] = _dot(k_ref[pl.ds(off, TK), rows], qs_ref[rows, :])

    def softmax_pv(g, j, masked):
        rows = slice(g * DA_V_DIM, (g + 1) * DA_V_DIM)
        arow = slice(g * ACC_ROWS, (g + 1) * ACC_ROWS)
        off = pl.multiple_of(j * TK, TK)
        s = s_ref[g]
        if masked:
            key = lax.broadcasted_iota(jnp.int32, s.shape, 0)
            qry = lax.broadcasted_iota(jnp.int32, s.shape, 1) & (TQ - 1)
            s = jnp.where(key <= qry, s, -jnp.inf)
        m_prev = m_ref[g:g + 1, :]
        m_new = jnp.maximum(m_prev, jnp.max(s, axis=0, keepdims=True))
        alpha = jnp.exp2(m_prev - m_new)
        p = jnp.exp2(s - m_new).astype(BF16)
        v_ext = jnp.concatenate([vt_ref[rows, pl.ds(off, TK)], ones_rows], axis=0)
        acc_ref[arow, :] = alpha * acc_ref[arow, :] + _dot(v_ext, p)
        m_ref[g:g + 1, :] = m_new

    scores(0, 0)

    def body(j, carry):
        for g in range(HEADS_PER_STEP):
            if g + 1 < HEADS_PER_STEP:
                scores(g + 1, j)
            else:
                scores(0, j + 1)
            softmax_pv(g, j, masked=False)
        return carry

    lax.fori_loop(0, i, body, 0)
    for g in range(HEADS_PER_STEP):
        if g + 1 < HEADS_PER_STEP:
            scores(g + 1, i)
        softmax_pv(g, i, masked=True)

    lam = (jnp.exp(jnp.sum(lq1_ref[...] * lk1_ref[...], axis=1, keepdims=True))
           - jnp.exp(jnp.sum(lq2_ref[...] * lk2_ref[...], axis=1, keepdims=True)) + LAMBDA_INIT)
    for g in range(HEADS_PER_STEP):
        a0 = g * ACC_ROWS
        o_all = acc_ref[a0:a0 + DA_V_DIM, :] * (1.0 / acc_ref[a0 + DA_V_DIM:a0 + DA_V_DIM + 1, :])
        o = o_all[:, 0:TQ] - lam * o_all[:, TQ:2 * TQ]
        on = o * lax.rsqrt(jnp.mean(o * o, axis=0, keepdims=True) + EPS)
        o_ref[:, g * DA_V_DIM:(g + 1) * DA_V_DIM] = (
            (on.T * sw_ref[...]) * (1.0 - LAMBDA_INIT)).astype(o_ref.dtype)


def _attention(qt, k, vt, lq1, lk1, lq2, lk2, sw, bsz, seqlen):
    assert TQ == TK
    nq = seqlen // TQ
    gw = HEADS_PER_STEP * DA_V_DIM
    small = (lq1, lk1, lq2, lk2, sw)
    return pl.pallas_call(
        _attn_kernel,
        grid=(bsz, DA_HEADS // HEADS_PER_STEP, nq),
        in_specs=[
            pl.BlockSpec((gw, TQ), lambda b, h, i: (h, b * nq + i)),
            pl.BlockSpec((seqlen, gw), lambda b, h, i: (b, h)),
            pl.BlockSpec((gw, seqlen), lambda b, h, i: (h, b)),
        ] + [_const_spec(a.shape) for a in small],
        out_specs=pl.BlockSpec((TQ, gw), lambda b, h, i: (b * nq + i, h)),
        out_shape=jax.ShapeDtypeStruct((bsz * seqlen, DA_WIDTH), BF16),
        scratch_shapes=[
            pltpu.VMEM((gw, 2 * TQ), BF16),
            pltpu.VMEM((HEADS_PER_STEP * ACC_ROWS, 2 * TQ), F32),
            pltpu.VMEM((HEADS_PER_STEP, 2 * TQ), F32),
            pltpu.VMEM((HEADS_PER_STEP, TK, 2 * TQ), F32),
        ],
        compiler_params=pltpu.CompilerParams(
            dimension_semantics=("arbitrary", "arbitrary", "arbitrary"), vmem_limit_bytes=VMEM_LIMIT),
        name="diff_attn",
    )(qt, k, vt, *small)


def _out_proj_kernel(x_ref, ys_ref, ya_ref, wo_ref, nw_ref, h_ref, n_ref):
    h = x_ref[...] + _dot(ys_ref[...], wo_ref[0:SSD_INNER, :]) + _dot(ya_ref[...], wo_ref[SSD_INNER:, :])
    h_ref[...] = h
    n_ref[...] = (h * _rms_scale(h) * nw_ref[...]).astype(BF16)


def _out_proj(x2, ys, ya, wo, nw):
    t = x2.shape[0]
    tm = TM_PROJ
    row = pl.BlockSpec((tm, D_MODEL), lambda i: (i, 0))
    return pl.pallas_call(
        _out_proj_kernel,
        grid=(t // tm,),
        in_specs=[row, row, row, _const_spec(wo.shape), _const_spec(nw.shape)],
        out_specs=[row, row],
        out_shape=[jax.ShapeDtypeStruct((t, D_MODEL), F32), jax.ShapeDtypeStruct((t, D_MODEL), BF16)],
        compiler_params=pltpu.CompilerParams(
            dimension_semantics=("arbitrary",), vmem_limit_bytes=VMEM_LIMIT),
        name="out_proj",
    )(x2, ys, ya, wo, nw)


def _ffn_kernel(h_ref, n_ref, wg_ref, wu_ref, wd_ref, fw_ref, o_ref):
    n2 = n_ref[...]
    acc = h_ref[...]
    for f0 in range(0, D_FF, FF_CHUNK):
        gate = _dot(n2, wg_ref[:, f0:f0 + FF_CHUNK])
        up = _dot(n2, wu_ref[:, f0:f0 + FF_CHUNK])
        act = (gate * _sigmoid(gate) * up).astype(BF16)
        acc = acc + _dot(act, wd_ref[f0:f0 + FF_CHUNK, :])
    o_ref[...] = acc * _rms_scale(acc) * fw_ref[...]


def _ffn(h, n2, wg, wu, wd, fw):
    t = h.shape[0]
    tm = TM_PROJ
    row = pl.BlockSpec((tm, D_MODEL), lambda i: (i, 0))
    return pl.pallas_call(
        _ffn_kernel,
        grid=(t // tm,),
        in_specs=[row, row, _const_spec(wg.shape), _const_spec(wu.shape), _const_spec(wd.shape),
                  _const_spec(fw.shape)],
        out_specs=row,
        out_shape=jax.ShapeDtypeStruct((t, D_MODEL), F32),
        compiler_params=pltpu.CompilerParams(
            dimension_semantics=("arbitrary",), vmem_limit_bytes=VMEM_LIMIT),
        name="ffn",
    )(h, n2, wg, wu, wd, fw)


def _pad_lanes(v, fill=0.0):
    return jnp.pad(v.astype(F32), (0, LANES - v.shape[0]), constant_values=fill)[None, :]


def kernel(x, mix_norm_w, w_in, conv_w, conv_b, dt_bias, a_log, d_skip, ssd_norm_w, lam_q1, lam_k1, lam_q2,
           lam_k2, subln_w, w_out, ffn_norm_w, w_gate, w_up, w_down, final_norm_w):
    bsz, seqlen, _ = x.shape
    x2 = x.reshape(bsz * seqlen, D_MODEL)

    w = w_in[0]
    o_dt = SSD_INNER + SSD_CONV_DIM
    o_q = o_dt + SSD_HEADS
    w_dt = jnp.pad(w[:, o_dt:o_q], ((0, 0), (0, LANES - SSD_HEADS)))
    wa = jnp.concatenate([w[:, :o_dt], w_dt, w[:, o_q + DA_WIDTH:o_q + 2 * DA_WIDTH]], axis=1).astype(BF16)
    wt = jnp.concatenate([w[:, o_q:o_q + DA_WIDTH], w[:, o_q + 2 * DA_WIDTH:]], axis=1).T.astype(BF16)

    z, xbc, dt, k, qt, vt = _in_proj(x2, mix_norm_w[0][None, :], wa, wt)

    idx = jnp.arange(SSD_CHUNK)
    tri = (idx[:, None] >= idx[None, :]).astype(BF16)
    tri3 = jnp.concatenate([tri, tri, tri], axis=1)
    sel = (jnp.arange(LANES)[:, None] == (jnp.arange(SSD_INNER)[None, :] // SSD_HEAD_DIM)).astype(BF16)
    exp3 = jnp.concatenate([sel, sel, sel], axis=0)
    dskip_x = jnp.repeat(d_skip[0].astype(F32), SSD_HEAD_DIM)[None, :]

    y_ssd = _ssd(z, xbc, dt, conv_w[0], conv_b[0][None, :], _pad_lanes(dt_bias[0]), _pad_lanes(a_log[0]),
                 dskip_x, ssd_norm_w[0][None, :], tri3, exp3, bsz, seqlen)
    y_da = _attention(qt, k, vt, lam_q1[0][None, :], lam_k1[0][None, :], lam_q2[0][None, :],
                      lam_k2[0][None, :], subln_w[0][None, :], bsz, seqlen)

    h, n2 = _out_proj(x2, y_ssd, y_da, w_out[0].astype(BF16), ffn_norm_w[0][None, :])
    out = _ffn(h, n2, w_gate[0].astype(BF16), w_up[0].astype(BF16), w_down[0].astype(BF16),
               final_norm_w[None, :])
    return out.reshape(bsz, seqlen, D_MODEL)
```

```python
import functools
import math

import jax
import jax.numpy as jnp
from jax import lax
from jax.experimental import pallas as pl
from jax.experimental.pallas import tpu as pltpu

F32 = jnp.float32
BF16 = jnp.bfloat16

EPS = 1e-5
D_MODEL = 1024
SSD_HEADS = 16
SSD_HEAD_DIM = 64
SSD_INNER = SSD_HEADS * SSD_HEAD_DIM
SSD_GROUPS = 2
SSD_GROUP_WIDTH = SSD_INNER // SSD_GROUPS
SSD_STATE = 128
SSD_CONV = 4
SSD_CHUNK = 128
SSD_BC = SSD_GROUPS * SSD_STATE
SSD_CONV_DIM = SSD_INNER + 2 * SSD_BC
DA_HEADS = 8
DA_HEAD_DIM = 64
DA_V_DIM = 2 * DA_HEAD_DIM
DA_WIDTH = DA_HEADS * DA_V_DIM
D_FF = 2816
LAMBDA_INIT = 0.8 - 0.6 * math.exp(-0.3 * 0)

LANES = 128
CONV_HALO = 8
VMEM_LIMIT = 56 * 1024 * 1024

TM_PROJ = 512
TQ = 256
TK = 256
FF_CHUNK = 1408
HEADS_PER_STEP = 8
SCORE_LOOKAHEAD = 2
ONES_ROWS = 16
ACC_ROWS = DA_V_DIM + ONES_ROWS
LOG2E = math.log2(math.e)


def _const_spec(shape):
    nd = len(shape)
    return pl.BlockSpec(shape, lambda *_: (0,) * nd)


def _rms_scale(xf):
    return lax.rsqrt(jnp.mean(xf * xf, axis=-1, keepdims=True) + EPS)


def _dot(a, b):
    return jnp.dot(a, b, preferred_element_type=F32)


def _dot_nt(a, b):
    return lax.dot_general(a, b, (((1,), (1,)), ((), ())), preferred_element_type=F32)


def _split3(a):
    hi = a.astype(BF16)
    r1 = a - hi.astype(F32)
    mid = r1.astype(BF16)
    lo = (r1 - mid.astype(F32)).astype(BF16)
    return hi, mid, lo


def _sigmoid(x):
    return 1.0 / (1.0 + jnp.exp(-x))


def _in_proj_kernel(x_ref, nw_ref, wa_ref, wt_ref, z_ref, xbc_ref, dt_ref, k_ref, qt_ref, vt_ref):
    xf = x_ref[...]
    xn = (xf * _rms_scale(xf) * nw_ref[...]).astype(BF16)
    c0 = 0
    for ref, width in ((z_ref, SSD_INNER), (xbc_ref, SSD_CONV_DIM), (dt_ref, LANES), (k_ref, DA_WIDTH)):
        ref[...] = _dot(xn, wa_ref[:, c0:c0 + width]).astype(ref.dtype)
        c0 += width
    qt_ref[...] = _dot_nt(wt_ref[0:DA_WIDTH, :], xn).astype(BF16)
    vt_ref[...] = _dot_nt(wt_ref[DA_WIDTH:2 * DA_WIDTH, :], xn).astype(BF16)


def _in_proj(x2, nw, wa, wt):
    t = x2.shape[0]
    tm = TM_PROJ
    row = lambda w: pl.BlockSpec((tm, w), lambda i: (i, 0))
    col = pl.BlockSpec((DA_WIDTH, tm), lambda i: (0, i))
    return pl.pallas_call(
        _in_proj_kernel,
        grid=(t // tm,),
        in_specs=[row(D_MODEL), _const_spec(nw.shape), _const_spec(wa.shape), _const_spec(wt.shape)],
        out_specs=[row(SSD_INNER), row(SSD_CONV_DIM), row(LANES), row(DA_WIDTH), col, col],
        out_shape=[
            jax.ShapeDtypeStruct((t, SSD_INNER), BF16),
            jax.ShapeDtypeStruct((t, SSD_CONV_DIM), BF16),
            jax.ShapeDtypeStruct((t, LANES), F32),
            jax.ShapeDtypeStruct((t, DA_WIDTH), BF16),
            jax.ShapeDtypeStruct((DA_WIDTH, t), BF16),
            jax.ShapeDtypeStruct((DA_WIDTH, t), BF16),
        ],
        compiler_params=pltpu.CompilerParams(
            dimension_semantics=("arbitrary",), vmem_limit_bytes=VMEM_LIMIT),
        name="in_proj",
    )(x2, nw, wa, wt)


def _ssd_kernel(z_ref, xbc_ref, dt_ref, cw_ref, cb_ref, dtb_ref, alog_ref, dskip_ref, nw_ref,
                tri3_ref, exp3_ref, y_ref, ext_ref, conv_ref, state_ref, yacc_ref):
    L = SSD_CHUNK
    c = pl.program_id(1)

    @pl.when(c == 0)
    def _():
        state_ref[...] = jnp.zeros_like(state_ref)
        ext_ref[0:CONV_HALO, :] = jnp.zeros((CONV_HALO, SSD_CONV_DIM), F32)

    @pl.when(c > 0)
    def _():
        ext_ref[0:CONV_HALO, :] = ext_ref[L:L + CONV_HALO, :]

    ext_ref[CONV_HALO:CONV_HALO + L, :] = xbc_ref[...].astype(F32)

    for c0 in range(0, SSD_CONV_DIM, 512):
        acc = jnp.broadcast_to(cb_ref[:, c0:c0 + 512], (L, 512))
        for j in range(SSD_CONV):
            r0 = CONV_HALO - (SSD_CONV - 1) + j
            acc = acc + cw_ref[j:j + 1, c0:c0 + 512] * ext_ref[r0:r0 + L, c0:c0 + 512]
        conv_ref[:, c0:c0 + 512] = acc * _sigmoid(acc)

    dtr = dt_ref[...] + dtb_ref[...]
    dtv = jnp.maximum(dtr, 0.0) + jnp.log1p(jnp.exp(-jnp.abs(dtr)))
    adt = dtv * (-jnp.exp(alog_ref[...]))
    cs = _dot(tri3_ref[...], jnp.concatenate(_split3(adt), axis=0))
    cs_t = cs.T
    dt_t = dtv.T
    cs_last = cs[L - 1:L, :]
    ecs = jnp.exp(cs)
    w_state = dtv * jnp.exp(cs_last - cs)

    def expand(a):
        return _dot(jnp.concatenate(_split3(a), axis=1), exp3_ref[...])

    ecs_x = expand(ecs)
    wst_x = expand(w_state)

    row = lax.broadcasted_iota(jnp.int32, (L, L), 0)
    colm = lax.broadcasted_iota(jnp.int32, (L, L), 1)
    tril = row >= colm
    lane = lax.broadcasted_iota(jnp.int32, (L, LANES), 1)
    lo_half = lane < SSD_HEAD_DIM

    for g in range(SSD_GROUPS):
        gx = g * SSD_GROUP_WIDTH
        bm = conv_ref[:, SSD_INNER + g * SSD_STATE:SSD_INNER + (g + 1) * SSD_STATE]
        cm = conv_ref[:, SSD_INNER + SSD_BC + g * SSD_STATE:SSD_INNER + SSD_BC + (g + 1) * SSD_STATE]
        cm16 = cm.astype(BF16)
        cb = _dot_nt(cm16, bm.astype(BF16))
        xs_g = conv_ref[:, gx:gx + SSD_GROUP_WIDTH]

        y_off = _dot(cm16, state_ref[:, gx:gx + SSD_GROUP_WIDTH].astype(BF16)) * ecs_x[:, gx:gx + SSD_GROUP_WIDTH]
        yacc_ref[:, gx:gx + SSD_GROUP_WIDTH] = y_off + xs_g * dskip_ref[:, gx:gx + SSD_GROUP_WIDTH]

        for pair in range(SSD_HEADS // SSD_GROUPS // 2):
            ms = []
            for h in (g * 8 + 2 * pair, g * 8 + 2 * pair + 1):
                seg = cs[:, h:h + 1] - cs_t[h:h + 1, :]
                dec = jnp.exp(jnp.where(tril, seg, -jnp.inf))
                ms.append((cb * dec * dt_t[h:h + 1, :]).astype(BF16))
            x_pair = conv_ref[:, gx + pair * LANES:gx + (pair + 1) * LANES]
            x_blk = jnp.concatenate(
                [jnp.where(lo_half, x_pair, 0.0), jnp.where(lo_half, 0.0, x_pair)], axis=0).astype(BF16)
            sl = slice(gx + pair * LANES, gx + (pair + 1) * LANES)
            yacc_ref[:, sl] = yacc_ref[:, sl] + _dot(jnp.concatenate(ms, axis=1), x_blk)

        xd = (xs_g * wst_x[:, gx:gx + SSD_GROUP_WIDTH]).astype(BF16)
        contrib = _dot(bm.T.astype(BF16), xd)
        state_ref[:, gx:gx + SSD_GROUP_WIDTH] = (
            state_ref[:, gx:gx + SSD_GROUP_WIDTH] * ecs_x[L - 1:L, gx:gx + SSD_GROUP_WIDTH] + contrib)

        zf = z_ref[:, gx:gx + SSD_GROUP_WIDTH].astype(F32)
        gy = yacc_ref[:, gx:gx + SSD_GROUP_WIDTH] * (zf * _sigmoid(zf))
        y_ref[:, gx:gx + SSD_GROUP_WIDTH] = (
            gy * _rms_scale(gy) * nw_ref[:, gx:gx + SSD_GROUP_WIDTH]).astype(y_ref.dtype)


def _ssd(z, xbc, dt, cw, cb, dtb, alog, dskip_x, nw, tri3, exp3, bsz, seqlen):
    nc = seqlen // SSD_CHUNK
    L = SSD_CHUNK
    row = lambda w: pl.BlockSpec((L, w), lambda b, c: (b * nc + c, 0))
    consts = (cw, cb, dtb, alog, dskip_x, nw, tri3, exp3)
    return pl.pallas_call(
        _ssd_kernel,
        grid=(bsz, nc),
        in_specs=[row(SSD_INNER), row(SSD_CONV_DIM), row(LANES)] + [_const_spec(a.shape) for a in consts],
        out_specs=row(SSD_INNER),
        out_shape=jax.ShapeDtypeStruct((bsz * seqlen, SSD_INNER), BF16),
        scratch_shapes=[
            pltpu.VMEM((CONV_HALO + L, SSD_CONV_DIM), F32),
            pltpu.VMEM((L, SSD_CONV_DIM), F32),
            pltpu.VMEM((SSD_STATE, SSD_INNER), F32),
            pltpu.VMEM((L, SSD_INNER), F32),
        ],
        compiler_params=pltpu.CompilerParams(
            dimension_semantics=("arbitrary", "arbitrary"), vmem_limit_bytes=VMEM_LIMIT),
        name="ssd",
    )(z, xbc, dt, *consts)


def _attn_kernel(qt_ref, k_ref, vt_ref, lq1_ref, lk1_ref, lq2_ref, lk2_ref, sw_ref, o_ref,
                 qs_ref, acc_ref, m_ref, s_ref):
    i = pl.program_id(2)
    d_idx = lax.broadcasted_iota(jnp.int32, (DA_V_DIM, TQ), 0)
    ones_rows = jnp.ones((ONES_ROWS, TK), BF16)

    for g in range(HEADS_PER_STEP):
        rows = slice(g * DA_V_DIM, (g + 1) * DA_V_DIM)
        q = (qt_ref[rows, :].astype(F32) * (DA_HEAD_DIM ** -0.5 * LOG2E)).astype(BF16)
        zero = jnp.zeros_like(q)
        qs_ref[rows, 0:TQ] = jnp.where(d_idx < DA_HEAD_DIM, q, zero)
        qs_ref[rows, TQ:2 * TQ] = jnp.where(d_idx < DA_HEAD_DIM, zero, q)

    def scores(g, j):
        rows = slice(g * DA_V_DIM, (g + 1) * DA_V_DIM)
        off = pl.multiple_of(j * TK, TK)
        s_ref[g] = _dot(k_ref[pl.ds(off, TK), rows], qs_ref[rows, :])

    def softmax_pv(g, j, diagonal):
        rows = slice(g * DA_V_DIM, (g + 1) * DA_V_DIM)
        arow = slice(g * ACC_ROWS, (g + 1) * ACC_ROWS)
        off = pl.multiple_of(j * TK, TK)
        s = s_ref[g]
        if diagonal:
            key = lax.broadcasted_iota(jnp.int32, s.shape, 0)
            qry = lax.broadcasted_iota(jnp.int32, s.shape, 1) & (TQ - 1)
            s = jnp.where(key <= qry, s, -jnp.inf)
            m_new = jnp.max(s, axis=0, keepdims=True)
        else:
            m_prev = m_ref[g:g + 1, :]
            m_new = jnp.maximum(m_prev, jnp.max(s, axis=0, keepdims=True))
        p = jnp.exp2(s - m_new).astype(BF16)
        v_ext = jnp.concatenate([vt_ref[rows, pl.ds(off, TK)], ones_rows], axis=0)
        pv = _dot(v_ext, p)
        if diagonal:
            acc_ref[arow, :] = pv
        else:
            acc_ref[arow, :] = jnp.exp2(m_prev - m_new) * acc_ref[arow, :] + pv
        m_ref[g:g + 1, :] = m_new

    for g in range(SCORE_LOOKAHEAD):
        scores(g, i)
    for g in range(HEADS_PER_STEP):
        ahead = g + SCORE_LOOKAHEAD
        if ahead < HEADS_PER_STEP:
            scores(ahead, i)
        else:
            scores(ahead - HEADS_PER_STEP, 0)
        softmax_pv(g, i, diagonal=True)

    def body(j, carry):
        for g in range(HEADS_PER_STEP):
            ahead = g + SCORE_LOOKAHEAD
            if ahead < HEADS_PER_STEP:
                scores(ahead, j)
            else:
                scores(ahead - HEADS_PER_STEP, j + 1)
            softmax_pv(g, j, diagonal=False)
        return carry

    lax.fori_loop(0, i, body, 0)

    lam = (jnp.exp(jnp.sum(lq1_ref[...] * lk1_ref[...], axis=1, keepdims=True))
           - jnp.exp(jnp.sum(lq2_ref[...] * lk2_ref[...], axis=1, keepdims=True)) + LAMBDA_INIT)
    for g in range(HEADS_PER_STEP):
        a0 = g * ACC_ROWS
        o_all = acc_ref[a0:a0 + DA_V_DIM, :] * (1.0 / acc_ref[a0 + DA_V_DIM:a0 + DA_V_DIM + 1, :])
        o = o_all[:, 0:TQ] - lam * o_all[:, TQ:2 * TQ]
        on = o * lax.rsqrt(jnp.mean(o * o, axis=0, keepdims=True) + EPS)
        o_ref[:, g * DA_V_DIM:(g + 1) * DA_V_DIM] = (
            (on.T * sw_ref[...]) * (1.0 - LAMBDA_INIT)).astype(o_ref.dtype)


def _attention(qt, k, vt, lq1, lk1, lq2, lk2, sw, bsz, seqlen):
    assert TQ == TK
    nq = seqlen // TQ
    gw = HEADS_PER_STEP * DA_V_DIM
    small = (lq1, lk1, lq2, lk2, sw)
    return pl.pallas_call(
        _attn_kernel,
        grid=(bsz, DA_HEADS // HEADS_PER_STEP, nq),
        in_specs=[
            pl.BlockSpec((gw, TQ), lambda b, h, i: (h, b * nq + i)),
            pl.BlockSpec((seqlen, gw), lambda b, h, i: (b, h)),
            pl.BlockSpec((gw, seqlen), lambda b, h, i: (h, b)),
        ] + [_const_spec(a.shape) for a in small],
        out_specs=pl.BlockSpec((TQ, gw), lambda b, h, i: (b * nq + i, h)),
        out_shape=jax.ShapeDtypeStruct((bsz * seqlen, DA_WIDTH), BF16),
        scratch_shapes=[
            pltpu.VMEM((gw, 2 * TQ), BF16),
            pltpu.VMEM((HEADS_PER_STEP * ACC_ROWS, 2 * TQ), F32),
            pltpu.VMEM((HEADS_PER_STEP, 2 * TQ), F32),
            pltpu.VMEM((HEADS_PER_STEP, TK, 2 * TQ), F32),
        ],
        compiler_params=pltpu.CompilerParams(
            dimension_semantics=("arbitrary", "arbitrary", "arbitrary"), vmem_limit_bytes=VMEM_LIMIT),
        name="diff_attn",
    )(qt, k, vt, *small)


def _out_proj_kernel(x_ref, ys_ref, ya_ref, wo_ref, nw_ref, h_ref, n_ref):
    h = x_ref[...] + _dot(ys_ref[...], wo_ref[0:SSD_INNER, :]) + _dot(ya_ref[...], wo_ref[SSD_INNER:, :])
    h_ref[...] = h
    n_ref[...] = (h * _rms_scale(h) * nw_ref[...]).astype(BF16)


def _out_proj(x2, ys, ya, wo, nw):
    t = x2.shape[0]
    tm = TM_PROJ
    row = pl.BlockSpec((tm, D_MODEL), lambda i: (i, 0))
    return pl.pallas_call(
        _out_proj_kernel,
        grid=(t // tm,),
        in_specs=[row, row, row, _const_spec(wo.shape), _const_spec(nw.shape)],
        out_specs=[row, row],
        out_shape=[jax.ShapeDtypeStruct((t, D_MODEL), F32), jax.ShapeDtypeStruct((t, D_MODEL), BF16)],
        compiler_params=pltpu.CompilerParams(
            dimension_semantics=("arbitrary",), vmem_limit_bytes=VMEM_LIMIT),
        name="out_proj",
    )(x2, ys, ya, wo, nw)


def _ffn_kernel(h_ref, n_ref, wg_ref, wu_ref, wd_ref, fw_ref, o_ref):
    n2 = n_ref[...]
    acc = h_ref[...]
    for f0 in range(0, D_FF, FF_CHUNK):
        gate = _dot(n2, wg_ref[:, f0:f0 + FF_CHUNK])
        up = _dot(n2, wu_ref[:, f0:f0 + FF_CHUNK])
        act = (gate * _sigmoid(gate) * up).astype(BF16)
        acc = acc + _dot(act, wd_ref[f0:f0 + FF_CHUNK, :])
    o_ref[...] = acc * _rms_scale(acc) * fw_ref[...]


def _ffn(h, n2, wg, wu, wd, fw):
    t = h.shape[0]
    tm = TM_PROJ
    row = pl.BlockSpec((tm, D_MODEL), lambda i: (i, 0))
    return pl.pallas_call(
        _ffn_kernel,
        grid=(t // tm,),
        in_specs=[row, row, _const_spec(wg.shape), _const_spec(wu.shape), _const_spec(wd.shape),
                  _const_spec(fw.shape)],
        out_specs=row,
        out_shape=jax.ShapeDtypeStruct((t, D_MODEL), F32),
        compiler_params=pltpu.CompilerParams(
            dimension_semantics=("arbitrary",), vmem_limit_bytes=VMEM_LIMIT),
        name="ffn",
    )(h, n2, wg, wu, wd, fw)


def _pad_lanes(v, fill=0.0):
    return jnp.pad(v.astype(F32), (0, LANES - v.shape[0]), constant_values=fill)[None, :]


def kernel(x, mix_norm_w, w_in, conv_w, conv_b, dt_bias, a_log, d_skip, ssd_norm_w, lam_q1, lam_k1, lam_q2,
           lam_k2, subln_w, w_out, ffn_norm_w, w_gate, w_up, w_down, final_norm_w):
    bsz, seqlen, _ = x.shape
    x2 = x.reshape(bsz * seqlen, D_MODEL)

    w = w_in[0]
    o_dt = SSD_INNER + SSD_CONV_DIM
    o_q = o_dt + SSD_HEADS
    w_dt = jnp.pad(w[:, o_dt:o_q], ((0, 0), (0, LANES - SSD_HEADS)))
    wa = jnp.concatenate([w[:, :o_dt], w_dt, w[:, o_q + DA_WIDTH:o_q + 2 * DA_WIDTH]], axis=1).astype(BF16)
    wt = jnp.concatenate([w[:, o_q:o_q + DA_WIDTH], w[:, o_q + 2 * DA_WIDTH:]], axis=1).T.astype(BF16)

    z, xbc, dt, k, qt, vt = _in_proj(x2, mix_norm_w[0][None, :], wa, wt)

    idx = jnp.arange(SSD_CHUNK)
    tri = (idx[:, None] >= idx[None, :]).astype(BF16)
    tri3 = jnp.concatenate([tri, tri, tri], axis=1)
    sel = (jnp.arange(LANES)[:, None] == (jnp.arange(SSD_INNER)[None, :] // SSD_HEAD_DIM)).astype(BF16)
    exp3 = jnp.concatenate([sel, sel, sel], axis=0)
    dskip_x = jnp.repeat(d_skip[0].astype(F32), SSD_HEAD_DIM)[None, :]

    y_ssd = _ssd(z, xbc, dt, conv_w[0], conv_b[0][None, :], _pad_lanes(dt_bias[0]), _pad_lanes(a_log[0]),
                 dskip_x, ssd_norm_w[0][None, :], tri3, exp3, bsz, seqlen)
    y_da = _attention(qt, k, vt, lam_q1[0][None, :], lam_k1[0][None, :], lam_q2[0][None, :],
                      lam_k2[0][None, :], subln_w[0][None, :], bsz, seqlen)

    h, n2 = _out_proj(x2, y_ssd, y_da, w_out[0].astype(BF16), ffn_norm_w[0][None, :])
    out = _ffn(h, n2, w_gate[0].astype(BF16), w_up[0].astype(BF16), w_down[0].astype(BF16),
               final_norm_w[None, :])
    return out.reshape(bsz, seqlen, D_MODEL)
```

```python
import functools
import math

import jax
import jax.numpy as jnp
from jax import lax
from jax.experimental import pallas as pl
from jax.experimental.pallas import tpu as pltpu

F32 = jnp.float32
BF16 = jnp.bfloat16

EPS = 1e-5
D_MODEL = 1024
SSD_HEADS = 16
SSD_HEAD_DIM = 64
SSD_INNER = SSD_HEADS * SSD_HEAD_DIM
SSD_GROUPS = 2
SSD_GROUP_WIDTH = SSD_INNER // SSD_GROUPS
SSD_STATE = 128
SSD_CONV = 4
SSD_CHUNK = 128
SSD_BC = SSD_GROUPS * SSD_STATE
SSD_CONV_DIM = SSD_INNER + 2 * SSD_BC
DA_HEADS = 8
DA_HEAD_DIM = 64
DA_V_DIM = 2 * DA_HEAD_DIM
DA_WIDTH = DA_HEADS * DA_V_DIM
D_FF = 2816
LAMBDA_INIT = 0.8 - 0.6 * math.exp(-0.3 * 0)

LANES = 128
CONV_HALO = 16
VMEM_LIMIT = 56 * 1024 * 1024

TM_PROJ = 512
TQ = 256
TK = 256
MXU_WIDTH = 256
FF_SPLITS = (0, 6 * MXU_WIDTH, D_FF)
HEADS_PER_STEP = 8
SCORE_LOOKAHEAD = 2
ONES_ROWS = 16
ACC_ROWS = DA_V_DIM + ONES_ROWS
LOG2E = math.log2(math.e)


def _const_spec(shape):
    nd = len(shape)
    return pl.BlockSpec(shape, lambda *_: (0,) * nd, pipeline_mode=pl.Buffered(1))


def _rms_scale(xf):
    return lax.rsqrt(jnp.mean(xf * xf, axis=-1, keepdims=True) + EPS)


def _dot(a, b):
    return jnp.dot(a, b, preferred_element_type=F32)


def _dot_nt(a, b):
    return lax.dot_general(a, b, (((1,), (1,)), ((), ())), preferred_element_type=F32)


def _split3(a):
    hi = a.astype(BF16)
    r1 = a - hi.astype(F32)
    mid = r1.astype(BF16)
    lo = (r1 - mid.astype(F32)).astype(BF16)
    return hi, mid, lo


def _silu(x):
    h = 0.5 * x
    return h + h * jnp.tanh(h)


def _in_proj_kernel(x_ref, nw_ref, wa_ref, wt_ref, z_ref, xbc_ref, dt_ref, k_ref, qt_ref, vt_ref):
    xf = x_ref[...]
    xn = (xf * _rms_scale(xf) * nw_ref[...]).astype(BF16)
    c0 = 0
    for ref, width in ((z_ref, SSD_INNER), (xbc_ref, SSD_CONV_DIM), (dt_ref, LANES), (k_ref, DA_WIDTH)):
        ref[...] = _dot(xn, wa_ref[:, c0:c0 + width]).astype(ref.dtype)
        c0 += width
    qt_ref[...] = _dot_nt(wt_ref[0:DA_WIDTH, :], xn).astype(BF16)
    vt_ref[...] = _dot_nt(wt_ref[DA_WIDTH:2 * DA_WIDTH, :], xn).astype(BF16)


def _in_proj(x2, nw, wa, wt):
    t = x2.shape[0]
    tm = TM_PROJ
    row = lambda w: pl.BlockSpec((tm, w), lambda i: (i, 0))
    col = pl.BlockSpec((DA_WIDTH, tm), lambda i: (0, i))
    return pl.pallas_call(
        _in_proj_kernel,
        grid=(t // tm,),
        in_specs=[row(D_MODEL), _const_spec(nw.shape), _const_spec(wa.shape), _const_spec(wt.shape)],
        out_specs=[row(SSD_INNER), row(SSD_CONV_DIM), row(LANES), row(DA_WIDTH), col, col],
        out_shape=[
            jax.ShapeDtypeStruct((t, SSD_INNER), BF16),
            jax.ShapeDtypeStruct((t, SSD_CONV_DIM), BF16),
            jax.ShapeDtypeStruct((t, LANES), F32),
            jax.ShapeDtypeStruct((t, DA_WIDTH), BF16),
            jax.ShapeDtypeStruct((DA_WIDTH, t), BF16),
            jax.ShapeDtypeStruct((DA_WIDTH, t), BF16),
        ],
        compiler_params=pltpu.CompilerParams(
            dimension_semantics=("arbitrary",), vmem_limit_bytes=VMEM_LIMIT),
        name="in_proj",
    )(x2, nw, wa, wt)


def _ssd_kernel(z_ref, xbc_ref, dt_ref, cw_ref, cb_ref, dtb_ref, alog_ref, dskip_ref, nw_ref,
                tri3_ref, exp2_ref, shift_ref, y_ref, ext_ref, conv_ref, state_ref, yacc_ref):
    L = SSD_CHUNK
    c = pl.program_id(1)

    @pl.when(c == 0)
    def _():
        state_ref[...] = jnp.zeros_like(state_ref)
        ext_ref[0:CONV_HALO, :] = jnp.zeros((CONV_HALO, SSD_CONV_DIM), BF16)

    @pl.when(c > 0)
    def _():
        ext_ref[0:CONV_HALO, :] = ext_ref[L:L + CONV_HALO, :]

    ext_ref[CONV_HALO:CONV_HALO + L, :] = xbc_ref[...]

    for c0 in range(0, SSD_CONV_DIM, 512):
        e = ext_ref[:, c0:c0 + 512]
        acc = cb_ref[:, c0:c0 + 512] + cw_ref[SSD_CONV - 1:SSD_CONV, c0:c0 + 512] * e[CONV_HALO:, :].astype(F32)
        shifted = _dot(shift_ref[...], e)
        for back in range(1, SSD_CONV):
            j = SSD_CONV - 1 - back
            acc = acc + cw_ref[j:j + 1, c0:c0 + 512] * shifted[(back - 1) * L:back * L, :]
        conv_ref[:, c0:c0 + 512] = _silu(acc)

    dtr = dt_ref[...] + dtb_ref[...]
    dtv = jnp.maximum(dtr, 0.0) + jnp.log(1.0 + jnp.exp(-jnp.abs(dtr)))
    adt = dtv * (-jnp.exp(alog_ref[...]))
    cs = _dot(tri3_ref[...], jnp.concatenate(_split3(adt), axis=0))
    csd_t = (cs - jnp.log(dtv)).T
    cs_last = cs[L - 1:L, :]
    ecs = jnp.exp(cs)
    w_state = dtv * jnp.exp(cs_last - cs)

    def expand(a):
        hi = a.astype(BF16)
        lo = (a - hi.astype(F32)).astype(BF16)
        return _dot(jnp.concatenate([hi, lo], axis=1), exp2_ref[...])

    ecs_x = expand(ecs)
    wst_x = expand(w_state)

    row = lax.broadcasted_iota(jnp.int32, (L, L), 0)
    colm = lax.broadcasted_iota(jnp.int32, (L, L), 1)
    tril = row >= colm
    lane = lax.broadcasted_iota(jnp.int32, (L, LANES), 1)
    lo_half = lane < SSD_HEAD_DIM

    for g in range(SSD_GROUPS):
        gx = g * SSD_GROUP_WIDTH
        bm = conv_ref[:, SSD_INNER + g * SSD_STATE:SSD_INNER + (g + 1) * SSD_STATE]
        cm = conv_ref[:, SSD_INNER + SSD_BC + g * SSD_STATE:SSD_INNER + SSD_BC + (g + 1) * SSD_STATE]
        cm16 = cm.astype(BF16)
        cb = _dot_nt(cm16, bm.astype(BF16))
        xs_g = conv_ref[:, gx:gx + SSD_GROUP_WIDTH]

        y_off = _dot(cm16, state_ref[:, gx:gx + SSD_GROUP_WIDTH].astype(BF16)) * ecs_x[:, gx:gx + SSD_GROUP_WIDTH]
        yacc_ref[:, gx:gx + SSD_GROUP_WIDTH] = y_off + xs_g * dskip_ref[:, gx:gx + SSD_GROUP_WIDTH]

        for pair in range(SSD_HEADS // SSD_GROUPS // 2):
            ms = []
            for h in (g * 8 + 2 * pair, g * 8 + 2 * pair + 1):
                seg = cs[:, h:h + 1] - csd_t[h:h + 1, :]
                ms.append((cb * jnp.exp(jnp.where(tril, seg, -jnp.inf))).astype(BF16))
            x_pair = conv_ref[:, gx + pair * LANES:gx + (pair + 1) * LANES]
            x_blk = jnp.concatenate(
                [jnp.where(lo_half, x_pair, 0.0), jnp.where(lo_half, 0.0, x_pair)], axis=0).astype(BF16)
            sl = slice(gx + pair * LANES, gx + (pair + 1) * LANES)
            yacc_ref[:, sl] = yacc_ref[:, sl] + _dot(jnp.concatenate(ms, axis=1), x_blk)

        xd = (xs_g * wst_x[:, gx:gx + SSD_GROUP_WIDTH]).astype(BF16)
        contrib = _dot(bm.T.astype(BF16), xd)
        state_ref[:, gx:gx + SSD_GROUP_WIDTH] = (
            state_ref[:, gx:gx + SSD_GROUP_WIDTH] * ecs_x[L - 1:L, gx:gx + SSD_GROUP_WIDTH] + contrib)

        zf = z_ref[:, gx:gx + SSD_GROUP_WIDTH].astype(F32)
        gy = yacc_ref[:, gx:gx + SSD_GROUP_WIDTH] * _silu(zf)
        y_ref[:, gx:gx + SSD_GROUP_WIDTH] = (
            gy * _rms_scale(gy) * nw_ref[:, gx:gx + SSD_GROUP_WIDTH]).astype(y_ref.dtype)


def _ssd(z, xbc, dt, cw, cb, dtb, alog, dskip_x, nw, tri3, exp2, shift, bsz, seqlen):
    nc = seqlen // SSD_CHUNK
    L = SSD_CHUNK
    row = lambda w: pl.BlockSpec((L, w), lambda b, c: (b * nc + c, 0))
    consts = (cw, cb, dtb, alog, dskip_x, nw, tri3, exp2, shift)
    return pl.pallas_call(
        _ssd_kernel,
        grid=(bsz, nc),
        in_specs=[row(SSD_INNER), row(SSD_CONV_DIM), row(LANES)] + [_const_spec(a.shape) for a in consts],
        out_specs=row(SSD_INNER),
        out_shape=jax.ShapeDtypeStruct((bsz * seqlen, SSD_INNER), BF16),
        scratch_shapes=[
            pltpu.VMEM((CONV_HALO + L, SSD_CONV_DIM), BF16),
            pltpu.VMEM((L, SSD_CONV_DIM), F32),
            pltpu.VMEM((SSD_STATE, SSD_INNER), F32),
            pltpu.VMEM((L, SSD_INNER), F32),
        ],
        compiler_params=pltpu.CompilerParams(
            dimension_semantics=("arbitrary", "arbitrary"), vmem_limit_bytes=VMEM_LIMIT),
        name="ssd",
    )(z, xbc, dt, *consts)


def _attn_kernel(qt_ref, k_ref, vt_ref, lq1_ref, lk1_ref, lq2_ref, lk2_ref, sw_ref, o_ref,
                 qs_ref, acc_ref, m_ref, s_ref):
    i = pl.program_id(2)
    d_idx = lax.broadcasted_iota(jnp.int32, (DA_V_DIM, TQ), 0)
    ones_rows = jnp.ones((ONES_ROWS, TK), BF16)

    for g in range(HEADS_PER_STEP):
        rows = slice(g * DA_V_DIM, (g + 1) * DA_V_DIM)
        q = (qt_ref[rows, :].astype(F32) * (DA_HEAD_DIM ** -0.5 * LOG2E)).astype(BF16)
        zero = jnp.zeros_like(q)
        qs_ref[rows, 0:TQ] = jnp.where(d_idx < DA_HEAD_DIM, q, zero)
        qs_ref[rows, TQ:2 * TQ] = jnp.where(d_idx < DA_HEAD_DIM, zero, q)

    def scores(g, j):
        rows = slice(g * DA_V_DIM, (g + 1) * DA_V_DIM)
        off = pl.multiple_of(j * TK, TK)
        s_ref[g] = _dot(k_ref[pl.ds(off, TK), rows], qs_ref[rows, :])

    def softmax_pv(g, j, diagonal):
        rows = slice(g * DA_V_DIM, (g + 1) * DA_V_DIM)
        arow = slice(g * ACC_ROWS, (g + 1) * ACC_ROWS)
        off = pl.multiple_of(j * TK, TK)
        s = s_ref[g]
        if diagonal:
            key = lax.broadcasted_iota(jnp.int32, s.shape, 0)
            qry = lax.broadcasted_iota(jnp.int32, s.shape, 1) & (TQ - 1)
            s = jnp.where(key <= qry, s, -jnp.inf)
            m_new = jnp.max(s, axis=0, keepdims=True)
        else:
            m_prev = m_ref[g:g + 1, :]
            m_new = jnp.maximum(m_prev, jnp.max(s, axis=0, keepdims=True))
        p = jnp.exp2(s - m_new).astype(BF16)
        v_ext = jnp.concatenate([vt_ref[rows, pl.ds(off, TK)], ones_rows], axis=0)
        pv = _dot(v_ext, p)
        if diagonal:
            acc_ref[arow, :] = pv
        else:
            acc_ref[arow, :] = jnp.exp2(m_prev - m_new) * acc_ref[arow, :] + pv
        m_ref[g:g + 1, :] = m_new

    for g in range(SCORE_LOOKAHEAD):
        scores(g, i)
    for g in range(HEADS_PER_STEP):
        ahead = g + SCORE_LOOKAHEAD
        if ahead < HEADS_PER_STEP:
            scores(ahead, i)
        else:
            scores(ahead - HEADS_PER_STEP, 0)
        softmax_pv(g, i, diagonal=True)

    def body(j, carry):
        for g in range(HEADS_PER_STEP):
            ahead = g + SCORE_LOOKAHEAD
            if ahead < HEADS_PER_STEP:
                scores(ahead, j)
            else:
                scores(ahead - HEADS_PER_STEP, j + 1)
            softmax_pv(g, j, diagonal=False)
        return carry

    lax.fori_loop(0, i, body, 0)

    lam = (jnp.exp(jnp.sum(lq1_ref[...] * lk1_ref[...], axis=1, keepdims=True))
           - jnp.exp(jnp.sum(lq2_ref[...] * lk2_ref[...], axis=1, keepdims=True)) + LAMBDA_INIT)
    for g in range(HEADS_PER_STEP):
        a0 = g * ACC_ROWS
        o_all = acc_ref[a0:a0 + DA_V_DIM, :] * (1.0 / acc_ref[a0 + DA_V_DIM:a0 + DA_V_DIM + 1, :])
        o = o_all[:, 0:TQ] - lam * o_all[:, TQ:2 * TQ]
        on = o * lax.rsqrt(jnp.mean(o * o, axis=0, keepdims=True) + EPS)
        o_ref[:, g * DA_V_DIM:(g + 1) * DA_V_DIM] = (
            (on.T * sw_ref[...]) * (1.0 - LAMBDA_INIT)).astype(o_ref.dtype)


def _attention(qt, k, vt, lq1, lk1, lq2, lk2, sw, bsz, seqlen):
    assert TQ == TK
    nq = seqlen // TQ
    gw = HEADS_PER_STEP * DA_V_DIM
    small = (lq1, lk1, lq2, lk2, sw)
    return pl.pallas_call(
        _attn_kernel,
        grid=(bsz, DA_HEADS // HEADS_PER_STEP, nq),
        in_specs=[
            pl.BlockSpec((gw, TQ), lambda b, h, i: (h, b * nq + i)),
            pl.BlockSpec((seqlen, gw), lambda b, h, i: (b, h)),
            pl.BlockSpec((gw, seqlen), lambda b, h, i: (h, b)),
        ] + [_const_spec(a.shape) for a in small],
        out_specs=pl.BlockSpec((TQ, gw), lambda b, h, i: (b * nq + i, h)),
        out_shape=jax.ShapeDtypeStruct((bsz * seqlen, DA_WIDTH), BF16),
        scratch_shapes=[
            pltpu.VMEM((gw, 2 * TQ), BF16),
            pltpu.VMEM((HEADS_PER_STEP * ACC_ROWS, 2 * TQ), F32),
            pltpu.VMEM((HEADS_PER_STEP, 2 * TQ), F32),
            pltpu.VMEM((HEADS_PER_STEP, TK, 2 * TQ), F32),
        ],
        compiler_params=pltpu.CompilerParams(
            dimension_semantics=("arbitrary", "arbitrary", "arbitrary"), vmem_limit_bytes=VMEM_LIMIT),
        name="diff_attn",
    )(qt, k, vt, *small)


def _mlp_kernel(x_ref, ys_ref, ya_ref, wo_ref, nw_ref, wg_ref, wu_ref, wd_ref, fw_ref, o_ref, h_ref):
    h_ref[...] = x_ref[...] + _dot(ys_ref[...], wo_ref[0:SSD_INNER, :]) + _dot(ya_ref[...], wo_ref[SSD_INNER:, :])
    h = h_ref[...]
    n2 = (h * _rms_scale(h) * nw_ref[...]).astype(BF16)
    ffn = None
    for f0, f1 in zip(FF_SPLITS[:-1], FF_SPLITS[1:]):
        gate = _dot(n2, wg_ref[:, f0:f1])
        up = _dot(n2, wu_ref[:, f0:f1])
        act = (_silu(gate) * up).astype(BF16)
        down = _dot(act, wd_ref[f0:f1, :])
        ffn = down if ffn is None else ffn + down
    out = h_ref[...] + ffn
    o_ref[...] = out * _rms_scale(out) * fw_ref[...]


def _mlp(x2, ys, ya, wo, nw, wg, wu, wd, fw):
    t = x2.shape[0]
    tm = TM_PROJ
    row = pl.BlockSpec((tm, D_MODEL), lambda i: (i, 0))
    consts = (wo, nw, wg, wu, wd, fw)
    return pl.pallas_call(
        _mlp_kernel,
        grid=(t // tm,),
        in_specs=[row, row, row] + [_const_spec(a.shape) for a in consts],
        out_specs=row,
        out_shape=jax.ShapeDtypeStruct((t, D_MODEL), F32),
        scratch_shapes=[pltpu.VMEM((tm, D_MODEL), F32)],
        compiler_params=pltpu.CompilerParams(
            dimension_semantics=("arbitrary",), vmem_limit_bytes=VMEM_LIMIT),
        name="mlp",
    )(x2, ys, ya, *consts)


def _pad_lanes(v, fill=0.0):
    return jnp.pad(v.astype(F32), (0, LANES - v.shape[0]), constant_values=fill)[None, :]


def kernel(x, mix_norm_w, w_in, conv_w, conv_b, dt_bias, a_log, d_skip, ssd_norm_w, lam_q1, lam_k1, lam_q2,
           lam_k2, subln_w, w_out, ffn_norm_w, w_gate, w_up, w_down, final_norm_w):
    bsz, seqlen, _ = x.shape
    x2 = x.reshape(bsz * seqlen, D_MODEL)

    w = w_in[0]
    o_dt = SSD_INNER + SSD_CONV_DIM
    o_q = o_dt + SSD_HEADS
    w_dt = jnp.pad(w[:, o_dt:o_q], ((0, 0), (0, LANES - SSD_HEADS)))
    wa = jnp.concatenate([w[:, :o_dt], w_dt, w[:, o_q + DA_WIDTH:o_q + 2 * DA_WIDTH]], axis=1).astype(BF16)
    wt = jnp.concatenate([w[:, o_q:o_q + DA_WIDTH], w[:, o_q + 2 * DA_WIDTH:]], axis=1).T.astype(BF16)

    z, xbc, dt, k, qt, vt = _in_proj(x2, mix_norm_w[0][None, :], wa, wt)

    idx = jnp.arange(SSD_CHUNK)
    tri = (idx[:, None] >= idx[None, :]).astype(BF16)
    tri3 = jnp.concatenate([tri, tri, tri], axis=1)
    sel = (jnp.arange(LANES)[:, None] == (jnp.arange(SSD_INNER)[None, :] // SSD_HEAD_DIM)).astype(BF16)
    exp2 = jnp.concatenate([sel, sel], axis=0)
    src = CONV_HALO + idx[None, :, None] - jnp.arange(1, SSD_CONV)[:, None, None]
    shift = (jnp.arange(CONV_HALO + SSD_CHUNK)[None, None, :] == src).astype(BF16)
    shift = shift.reshape((SSD_CONV - 1) * SSD_CHUNK, CONV_HALO + SSD_CHUNK)
    dskip_x = jnp.repeat(d_skip[0].astype(F32), SSD_HEAD_DIM)[None, :]

    y_ssd = _ssd(z, xbc, dt, conv_w[0], conv_b[0][None, :], _pad_lanes(dt_bias[0]), _pad_lanes(a_log[0]),
                 dskip_x, ssd_norm_w[0][None, :], tri3, exp2, shift, bsz, seqlen)
    y_da = _attention(qt, k, vt, lam_q1[0][None, :], lam_k1[0][None, :], lam_q2[0][None, :],
                      lam_k2[0][None, :], subln_w[0][None, :], bsz, seqlen)

    out = _mlp(x2, y_ssd, y_da, w_out[0].astype(BF16), ffn_norm_w[0][None, :], w_gate[0].astype(BF16),
               w_up[0].astype(BF16), w_down[0].astype(BF16), final_norm_w[None, :])
    return out.reshape(bsz, seqlen, D_MODEL)
```

```python
import functools
import math

import jax
import jax.numpy as jnp
from jax import lax
from jax.experimental import pallas as pl
from jax.experimental.pallas import tpu as pltpu

F32 = jnp.float32
BF16 = jnp.bfloat16

EPS = 1e-5
D_MODEL = 1024
SSD_HEADS = 16
SSD_HEAD_DIM = 64
SSD_INNER = SSD_HEADS * SSD_HEAD_DIM
SSD_GROUPS = 2
SSD_GROUP_WIDTH = SSD_INNER // SSD_GROUPS
SSD_STATE = 128
SSD_CONV = 4
SSD_CHUNK = 128
SSD_BC = SSD_GROUPS * SSD_STATE
SSD_CONV_DIM = SSD_INNER + 2 * SSD_BC
DA_HEADS = 8
DA_HEAD_DIM = 64
DA_V_DIM = 2 * DA_HEAD_DIM
DA_WIDTH = DA_HEADS * DA_V_DIM
D_FF = 2816
LAMBDA_INIT = 0.8 - 0.6 * math.exp(-0.3 * 0)

LANES = 128
CONV_HALO = 16
VMEM_LIMIT = 56 * 1024 * 1024

TM_PROJ = 512
TQ = 256
TK = 256
MXU_WIDTH = 256
FF_SPLITS = (0, 6 * MXU_WIDTH, D_FF)
HEADS_PER_STEP = 8
SCORE_LOOKAHEAD = 2
ONES_ROWS = 16
ACC_ROWS = DA_V_DIM + ONES_ROWS
LOG2E = math.log2(math.e)


def _const_spec(shape):
    nd = len(shape)
    return pl.BlockSpec(shape, lambda *_: (0,) * nd, pipeline_mode=pl.Buffered(1))


def _rms_scale(xf):
    return lax.rsqrt(jnp.mean(xf * xf, axis=-1, keepdims=True) + EPS)


def _dot(a, b):
    return jnp.dot(a, b, preferred_element_type=F32)


def _dot_nt(a, b):
    return lax.dot_general(a, b, (((1,), (1,)), ((), ())), preferred_element_type=F32)


def _split3(a):
    hi = a.astype(BF16)
    r1 = a - hi.astype(F32)
    mid = r1.astype(BF16)
    lo = (r1 - mid.astype(F32)).astype(BF16)
    return hi, mid, lo


def _silu(x):
    h = 0.5 * x
    return h + h * jnp.tanh(h)


def _in_proj_kernel(x_ref, nw_ref, wa_ref, wt_ref, z_ref, xbc_ref, dt_ref, k_ref, qt_ref, vt_ref):
    xf = x_ref[...]
    xn = (xf * _rms_scale(xf) * nw_ref[...]).astype(BF16)
    c0 = 0
    for ref, width in ((z_ref, SSD_INNER), (xbc_ref, SSD_CONV_DIM), (dt_ref, LANES), (k_ref, DA_WIDTH)):
        ref[...] = _dot(xn, wa_ref[:, c0:c0 + width]).astype(ref.dtype)
        c0 += width
    qt_ref[...] = (_dot_nt(wt_ref[0:DA_WIDTH, :], xn) * (DA_HEAD_DIM ** -0.5 * LOG2E)).astype(BF16)
    vt_ref[...] = _dot_nt(wt_ref[DA_WIDTH:2 * DA_WIDTH, :], xn).astype(BF16)


def _in_proj(x2, nw, wa, wt):
    t = x2.shape[0]
    tm = TM_PROJ
    row = lambda w: pl.BlockSpec((tm, w), lambda i: (i, 0))
    col = pl.BlockSpec((DA_WIDTH, tm), lambda i: (0, i))
    return pl.pallas_call(
        _in_proj_kernel,
        grid=(t // tm,),
        in_specs=[row(D_MODEL), _const_spec(nw.shape), _const_spec(wa.shape), _const_spec(wt.shape)],
        out_specs=[row(SSD_INNER), row(SSD_CONV_DIM), row(LANES), row(DA_WIDTH), col, col],
        out_shape=[
            jax.ShapeDtypeStruct((t, SSD_INNER), BF16),
            jax.ShapeDtypeStruct((t, SSD_CONV_DIM), BF16),
            jax.ShapeDtypeStruct((t, LANES), F32),
            jax.ShapeDtypeStruct((t, DA_WIDTH), BF16),
            jax.ShapeDtypeStruct((DA_WIDTH, t), BF16),
            jax.ShapeDtypeStruct((DA_WIDTH, t), BF16),
        ],
        compiler_params=pltpu.CompilerParams(
            dimension_semantics=("arbitrary",), vmem_limit_bytes=VMEM_LIMIT),
        name="in_proj",
    )(x2, nw, wa, wt)


def _ssd_kernel(z_ref, xbc_ref, dt_ref, cw_ref, cb_ref, dtb_ref, alog_ref, dskip_ref, nw_ref,
                tri3_ref, exp2_ref, shift_ref, y_ref, ext_ref, conv_ref, state_ref, yacc_ref):
    L = SSD_CHUNK
    c = pl.program_id(1)

    @pl.when(c == 0)
    def _():
        state_ref[...] = jnp.zeros_like(state_ref)
        ext_ref[0:CONV_HALO, :] = jnp.zeros((CONV_HALO, SSD_CONV_DIM), BF16)

    @pl.when(c > 0)
    def _():
        ext_ref[0:CONV_HALO, :] = ext_ref[L:L + CONV_HALO, :]

    ext_ref[CONV_HALO:CONV_HALO + L, :] = xbc_ref[...]

    for c0 in range(0, SSD_CONV_DIM, 512):
        e = ext_ref[:, c0:c0 + 512]
        acc = cb_ref[:, c0:c0 + 512] + cw_ref[SSD_CONV - 1:SSD_CONV, c0:c0 + 512] * e[CONV_HALO:, :].astype(F32)
        shifted = _dot(shift_ref[...], e)
        for back in range(1, SSD_CONV):
            j = SSD_CONV - 1 - back
            acc = acc + cw_ref[j:j + 1, c0:c0 + 512] * shifted[(back - 1) * L:back * L, :]
        conv_ref[:, c0:c0 + 512] = _silu(acc)

    dtr = dt_ref[...] + dtb_ref[...]
    dtv = jnp.maximum(dtr, 0.0) + jnp.log(1.0 + jnp.exp(-jnp.abs(dtr)))
    adt = dtv * (-jnp.exp(alog_ref[...]))
    cs = _dot(tri3_ref[...], jnp.concatenate(_split3(adt), axis=0))
    csd_t = (cs - jnp.log(dtv)).T
    cs_last = cs[L - 1:L, :]
    ecs = jnp.exp(cs)
    w_state = dtv * jnp.exp(cs_last - cs)

    def expand(a):
        hi = a.astype(BF16)
        lo = (a - hi.astype(F32)).astype(BF16)
        return _dot(jnp.concatenate([hi, lo], axis=1), exp2_ref[...])

    ecs_x = expand(ecs)
    wst_x = expand(w_state)

    row = lax.broadcasted_iota(jnp.int32, (L, L), 0)
    colm = lax.broadcasted_iota(jnp.int32, (L, L), 1)
    tril = row >= colm
    lane = lax.broadcasted_iota(jnp.int32, (L, LANES), 1)
    lo_half = lane < SSD_HEAD_DIM

    for g in range(SSD_GROUPS):
        gx = g * SSD_GROUP_WIDTH
        bm = conv_ref[:, SSD_INNER + g * SSD_STATE:SSD_INNER + (g + 1) * SSD_STATE]
        cm = conv_ref[:, SSD_INNER + SSD_BC + g * SSD_STATE:SSD_INNER + SSD_BC + (g + 1) * SSD_STATE]
        cm16 = cm.astype(BF16)
        cb = _dot_nt(cm16, bm.astype(BF16))
        xs_g = conv_ref[:, gx:gx + SSD_GROUP_WIDTH]

        y_off = _dot(cm16, state_ref[:, gx:gx + SSD_GROUP_WIDTH].astype(BF16)) * ecs_x[:, gx:gx + SSD_GROUP_WIDTH]
        yacc_ref[:, gx:gx + SSD_GROUP_WIDTH] = y_off + xs_g * dskip_ref[:, gx:gx + SSD_GROUP_WIDTH]

        for pair in range(SSD_HEADS // SSD_GROUPS // 2):
            ms = []
            for h in (g * 8 + 2 * pair, g * 8 + 2 * pair + 1):
                seg = cs[:, h:h + 1] - csd_t[h:h + 1, :]
                ms.append((cb * jnp.exp(jnp.where(tril, seg, -jnp.inf))).astype(BF16))
            x_pair = conv_ref[:, gx + pair * LANES:gx + (pair + 1) * LANES]
            x_blk = jnp.concatenate(
                [jnp.where(lo_half, x_pair, 0.0), jnp.where(lo_half, 0.0, x_pair)], axis=0).astype(BF16)
            sl = slice(gx + pair * LANES, gx + (pair + 1) * LANES)
            yacc_ref[:, sl] = yacc_ref[:, sl] + _dot(jnp.concatenate(ms, axis=1), x_blk)

        xd = (xs_g * wst_x[:, gx:gx + SSD_GROUP_WIDTH]).astype(BF16)
        contrib = _dot(bm.T.astype(BF16), xd)
        state_ref[:, gx:gx + SSD_GROUP_WIDTH] = (
            state_ref[:, gx:gx + SSD_GROUP_WIDTH] * ecs_x[L - 1:L, gx:gx + SSD_GROUP_WIDTH] + contrib)

        zf = z_ref[:, gx:gx + SSD_GROUP_WIDTH].astype(F32)
        gy = yacc_ref[:, gx:gx + SSD_GROUP_WIDTH] * _silu(zf)
        y_ref[:, gx:gx + SSD_GROUP_WIDTH] = (
            gy * _rms_scale(gy) * nw_ref[:, gx:gx + SSD_GROUP_WIDTH]).astype(y_ref.dtype)


def _ssd(z, xbc, dt, cw, cb, dtb, alog, dskip_x, nw, tri3, exp2, shift, bsz, seqlen):
    nc = seqlen // SSD_CHUNK
    L = SSD_CHUNK
    row = lambda w: pl.BlockSpec((L, w), lambda b, c: (b * nc + c, 0))
    consts = (cw, cb, dtb, alog, dskip_x, nw, tri3, exp2, shift)
    return pl.pallas_call(
        _ssd_kernel,
        grid=(bsz, nc),
        in_specs=[row(SSD_INNER), row(SSD_CONV_DIM), row(LANES)] + [_const_spec(a.shape) for a in consts],
        out_specs=row(SSD_INNER),
        out_shape=jax.ShapeDtypeStruct((bsz * seqlen, SSD_INNER), BF16),
        scratch_shapes=[
            pltpu.VMEM((CONV_HALO + L, SSD_CONV_DIM), BF16),
            pltpu.VMEM((L, SSD_CONV_DIM), F32),
            pltpu.VMEM((SSD_STATE, SSD_INNER), F32),
            pltpu.VMEM((L, SSD_INNER), F32),
        ],
        compiler_params=pltpu.CompilerParams(
            dimension_semantics=("arbitrary", "arbitrary"), vmem_limit_bytes=VMEM_LIMIT),
        name="ssd",
    )(z, xbc, dt, *consts)


def _attn_kernel(qt_ref, k_ref, vt_ref, lq1_ref, lk1_ref, lq2_ref, lk2_ref, sw_ref, o_ref,
                 qs_ref, acc_ref, m_ref, mx_ref, s_ref):
    i = pl.program_id(2)
    d_idx = lax.broadcasted_iota(jnp.int32, (DA_V_DIM, TQ), 0)
    ones_rows = jnp.ones((ONES_ROWS, TK), BF16)

    for g in range(HEADS_PER_STEP):
        rows = slice(g * DA_V_DIM, (g + 1) * DA_V_DIM)
        q = qt_ref[rows, :]
        zero = jnp.zeros_like(q)
        qs_ref[rows, 0:TQ] = jnp.where(d_idx < DA_HEAD_DIM, q, zero)
        qs_ref[rows, TQ:2 * TQ] = jnp.where(d_idx < DA_HEAD_DIM, zero, q)

    def scores(g, j, diagonal):
        rows = slice(g * DA_V_DIM, (g + 1) * DA_V_DIM)
        off = pl.multiple_of(j * TK, TK)
        s = _dot(k_ref[pl.ds(off, TK), rows], qs_ref[rows, :])
        if diagonal:
            key = lax.broadcasted_iota(jnp.int32, s.shape, 0)
            qry = lax.broadcasted_iota(jnp.int32, s.shape, 1) & (TQ - 1)
            s = jnp.where(key <= qry, s, -jnp.inf)
        s_ref[g] = s
        mx_ref[g:g + 1, :] = jnp.max(s, axis=0, keepdims=True)

    def softmax_pv(g, j, first):
        rows = slice(g * DA_V_DIM, (g + 1) * DA_V_DIM)
        arow = slice(g * ACC_ROWS, (g + 1) * ACC_ROWS)
        off = pl.multiple_of(j * TK, TK)
        if first:
            m_new = mx_ref[g:g + 1, :]
        else:
            m_prev = m_ref[g:g + 1, :]
            m_new = jnp.maximum(m_prev, mx_ref[g:g + 1, :])
        p = jnp.exp2(s_ref[g] - m_new).astype(BF16)
        v_ext = jnp.concatenate([vt_ref[rows, pl.ds(off, TK)], ones_rows], axis=0)
        pv = _dot(v_ext, p)
        if first:
            acc_ref[arow, :] = pv
        else:
            acc_ref[arow, :] = jnp.exp2(m_prev - m_new) * acc_ref[arow, :] + pv
        m_ref[g:g + 1, :] = m_new

    def run_tiles(tiles, diagonal, next_tile):
        chains = [(t, g) for t in tiles for g in range(HEADS_PER_STEP)]
        for n, (t, g) in enumerate(chains):
            ahead = n + SCORE_LOOKAHEAD
            if ahead < len(chains):
                scores(chains[ahead][1], chains[ahead][0], diagonal)
            else:
                scores(ahead - len(chains), next_tile, False)
            softmax_pv(g, t, first=diagonal)

    for g in range(SCORE_LOOKAHEAD):
        scores(g, i, True)
    run_tiles([i], diagonal=True, next_tile=0)

    def body(jj, carry):
        run_tiles([2 * jj, 2 * jj + 1], diagonal=False, next_tile=2 * jj + 2)
        return carry

    lax.fori_loop(0, i // 2, body, 0)

    @pl.when(i % 2 == 1)
    def _():
        run_tiles([i - 1], diagonal=False, next_tile=i)


    lam = (jnp.exp(jnp.sum(lq1_ref[...] * lk1_ref[...], axis=1, keepdims=True))
           - jnp.exp(jnp.sum(lq2_ref[...] * lk2_ref[...], axis=1, keepdims=True)) + LAMBDA_INIT)
    for g in range(HEADS_PER_STEP):
        a0 = g * ACC_ROWS
        o_all = acc_ref[a0:a0 + DA_V_DIM, :] * (1.0 / acc_ref[a0 + DA_V_DIM:a0 + DA_V_DIM + 1, :])
        o = o_all[:, 0:TQ] - lam * o_all[:, TQ:2 * TQ]
        on = o * lax.rsqrt(jnp.mean(o * o, axis=0, keepdims=True) + EPS)
        o_ref[:, g * DA_V_DIM:(g + 1) * DA_V_DIM] = (
            (on.T * sw_ref[...]) * (1.0 - LAMBDA_INIT)).astype(o_ref.dtype)


def _attention(qt, k, vt, lq1, lk1, lq2, lk2, sw, bsz, seqlen):
    assert TQ == TK
    nq = seqlen // TQ
    gw = HEADS_PER_STEP * DA_V_DIM
    small = (lq1, lk1, lq2, lk2, sw)
    return pl.pallas_call(
        _attn_kernel,
        grid=(bsz, DA_HEADS // HEADS_PER_STEP, nq),
        in_specs=[
            pl.BlockSpec((gw, TQ), lambda b, h, i: (h, b * nq + i)),
            pl.BlockSpec((seqlen, gw), lambda b, h, i: (b, h)),
            pl.BlockSpec((gw, seqlen), lambda b, h, i: (h, b)),
        ] + [_const_spec(a.shape) for a in small],
        out_specs=pl.BlockSpec((TQ, gw), lambda b, h, i: (b * nq + i, h)),
        out_shape=jax.ShapeDtypeStruct((bsz * seqlen, DA_WIDTH), BF16),
        scratch_shapes=[
            pltpu.VMEM((gw, 2 * TQ), BF16),
            pltpu.VMEM((HEADS_PER_STEP * ACC_ROWS, 2 * TQ), F32),
            pltpu.VMEM((HEADS_PER_STEP, 2 * TQ), F32),
            pltpu.VMEM((HEADS_PER_STEP, 2 * TQ), F32),
            pltpu.VMEM((HEADS_PER_STEP, TK, 2 * TQ), F32),
        ],
        compiler_params=pltpu.CompilerParams(
            dimension_semantics=("arbitrary", "arbitrary", "arbitrary"), vmem_limit_bytes=VMEM_LIMIT),
        name="diff_attn",
    )(qt, k, vt, *small)


def _mlp_kernel(x_ref, ys_ref, ya_ref, wo_ref, nw_ref, wg_ref, wu_ref, wd_ref, fw_ref, o_ref, h_ref):
    h_ref[...] = x_ref[...] + _dot(ys_ref[...], wo_ref[0:SSD_INNER, :]) + _dot(ya_ref[...], wo_ref[SSD_INNER:, :])
    h = h_ref[...]
    n2 = (h * _rms_scale(h) * nw_ref[...]).astype(BF16)
    ffn = None
    for f0, f1 in zip(FF_SPLITS[:-1], FF_SPLITS[1:]):
        gate = _dot(n2, wg_ref[:, f0:f1])
        up = _dot(n2, wu_ref[:, f0:f1])
        act = (_silu(gate) * up).astype(BF16)
        down = _dot(act, wd_ref[f0:f1, :])
        ffn = down if ffn is None else ffn + down
    out = h_ref[...] + ffn
    o_ref[...] = out * _rms_scale(out) * fw_ref[...]


def _mlp(x2, ys, ya, wo, nw, wg, wu, wd, fw):
    t = x2.shape[0]
    tm = TM_PROJ
    row = pl.BlockSpec((tm, D_MODEL), lambda i: (i, 0))
    consts = (wo, nw, wg, wu, wd, fw)
    return pl.pallas_call(
        _mlp_kernel,
        grid=(t // tm,),
        in_specs=[row, row, row] + [_const_spec(a.shape) for a in consts],
        out_specs=row,
        out_shape=jax.ShapeDtypeStruct((t, D_MODEL), F32),
        scratch_shapes=[pltpu.VMEM((tm, D_MODEL), F32)],
        compiler_params=pltpu.CompilerParams(
            dimension_semantics=("arbitrary",), vmem_limit_bytes=VMEM_LIMIT),
        name="mlp",
    )(x2, ys, ya, *consts)


def _pad_lanes(v, fill=0.0):
    return jnp.pad(v.astype(F32), (0, LANES - v.shape[0]), constant_values=fill)[None, :]


def kernel(x, mix_norm_w, w_in, conv_w, conv_b, dt_bias, a_log, d_skip, ssd_norm_w, lam_q1, lam_k1, lam_q2,
           lam_k2, subln_w, w_out, ffn_norm_w, w_gate, w_up, w_down, final_norm_w):
    bsz, seqlen, _ = x.shape
    x2 = x.reshape(bsz * seqlen, D_MODEL)

    w = w_in[0]
    o_dt = SSD_INNER + SSD_CONV_DIM
    o_q = o_dt + SSD_HEADS
    w_dt = jnp.pad(w[:, o_dt:o_q], ((0, 0), (0, LANES - SSD_HEADS)))
    wa = jnp.concatenate([w[:, :o_dt], w_dt, w[:, o_q + DA_WIDTH:o_q + 2 * DA_WIDTH]], axis=1).astype(BF16)
    wt = jnp.concatenate([w[:, o_q:o_q + DA_WIDTH], w[:, o_q + 2 * DA_WIDTH:]], axis=1).T.astype(BF16)

    z, xbc, dt, k, qt, vt = _in_proj(x2, mix_norm_w[0][None, :], wa, wt)

    idx = jnp.arange(SSD_CHUNK)
    tri = (idx[:, None] >= idx[None, :]).astype(BF16)
    tri3 = jnp.concatenate([tri, tri, tri], axis=1)
    sel = (jnp.arange(LANES)[:, None] == (jnp.arange(SSD_INNER)[None, :] // SSD_HEAD_DIM)).astype(BF16)
    exp2 = jnp.concatenate([sel, sel], axis=0)
    src = CONV_HALO + idx[None, :, None] - jnp.arange(1, SSD_CONV)[:, None, None]
    shift = (jnp.arange(CONV_HALO + SSD_CHUNK)[None, None, :] == src).astype(BF16)
    shift = shift.reshape((SSD_CONV - 1) * SSD_CHUNK, CONV_HALO + SSD_CHUNK)
    dskip_x = jnp.repeat(d_skip[0].astype(F32), SSD_HEAD_DIM)[None, :]

    y_ssd = _ssd(z, xbc, dt, conv_w[0], conv_b[0][None, :], _pad_lanes(dt_bias[0]), _pad_lanes(a_log[0]),
                 dskip_x, ssd_norm_w[0][None, :], tri3, exp2, shift, bsz, seqlen)
    y_da = _attention(qt, k, vt, lam_q1[0][None, :], lam_k1[0][None, :], lam_q2[0][None, :],
                      lam_k2[0][None, :], subln_w[0][None, :], bsz, seqlen)

    out = _mlp(x2, y_ssd, y_da, w_out[0].astype(BF16), ffn_norm_w[0][None, :], w_gate[0].astype(BF16),
               w_up[0].astype(BF16), w_down[0].astype(BF16), final_norm_w[None, :])
    return out.reshape(bsz, seqlen, D_MODEL)
```

```python
import functools
import math

import jax
import jax.numpy as jnp
from jax import lax
from jax.experimental import pallas as pl
from jax.experimental.pallas import tpu as pltpu

F32 = jnp.float32
BF16 = jnp.bfloat16

EPS = 1e-5
D_MODEL = 1024
SSD_HEADS = 16
SSD_HEAD_DIM = 64
SSD_INNER = SSD_HEADS * SSD_HEAD_DIM
SSD_GROUPS = 2
SSD_GROUP_WIDTH = SSD_INNER // SSD_GROUPS
SSD_STATE = 128
SSD_CONV = 4
SSD_CHUNK = 128
SSD_BC = SSD_GROUPS * SSD_STATE
SSD_CONV_DIM = SSD_INNER + 2 * SSD_BC
DA_HEADS = 8
DA_HEAD_DIM = 64
DA_V_DIM = 2 * DA_HEAD_DIM
DA_WIDTH = DA_HEADS * DA_V_DIM
D_FF = 2816
LAMBDA_INIT = 0.8 - 0.6 * math.exp(-0.3 * 0)

LANES = 128
CONV_HALO = 16
VMEM_LIMIT = 56 * 1024 * 1024

TM_PROJ = 512
TQ = 256
TK = 256
MXU_WIDTH = 256
FF_SPLITS = (0, 6 * MXU_WIDTH, D_FF)
HEADS_PER_STEP = 8
SSD_ROWS_PER_STEP = 4
SCORE_LOOKAHEAD = 2
ONES_ROWS = 16
ACC_ROWS = DA_V_DIM + ONES_ROWS
LOG2E = math.log2(math.e)


def _const_spec(shape):
    nd = len(shape)
    return pl.BlockSpec(shape, lambda *_: (0,) * nd, pipeline_mode=pl.Buffered(1))


def _rms_scale(xf):
    return lax.rsqrt(jnp.mean(xf * xf, axis=-1, keepdims=True) + EPS)


def _dot(a, b):
    return jnp.dot(a, b, preferred_element_type=F32)


def _dot_nt(a, b):
    return lax.dot_general(a, b, (((1,), (1,)), ((), ())), preferred_element_type=F32)


def _split3(a):
    hi = a.astype(BF16)
    r1 = a - hi.astype(F32)
    mid = r1.astype(BF16)
    lo = (r1 - mid.astype(F32)).astype(BF16)
    return hi, mid, lo


def _silu_of_twice(h):
    return h + h * jnp.tanh(h)


def _silu(x):
    return _silu_of_twice(0.5 * x)


def _in_proj_kernel(x_ref, nw_ref, wa_ref, wt_ref, z_ref, xbc_ref, dt_ref, k_ref, qt_ref, vt_ref):
    xf = x_ref[...]
    xn = (xf * _rms_scale(xf) * nw_ref[...]).astype(BF16)
    c0 = 0
    for ref, width, scale in ((z_ref, SSD_INNER, 0.5), (xbc_ref, SSD_CONV_DIM, None), (dt_ref, LANES, None),
                              (k_ref, DA_WIDTH, None)):
        r = _dot(xn, wa_ref[:, c0:c0 + width])
        ref[...] = (r if scale is None else r * scale).astype(ref.dtype)
        c0 += width
    qt_ref[...] = (_dot_nt(wt_ref[0:DA_WIDTH, :], xn) * (DA_HEAD_DIM ** -0.5 * LOG2E)).astype(BF16)
    vt_ref[...] = _dot_nt(wt_ref[DA_WIDTH:2 * DA_WIDTH, :], xn).astype(BF16)


def _in_proj(x2, nw, wa, wt):
    t = x2.shape[0]
    tm = TM_PROJ
    row = lambda w: pl.BlockSpec((tm, w), lambda i: (i, 0))
    col = pl.BlockSpec((DA_WIDTH, tm), lambda i: (0, i))
    return pl.pallas_call(
        _in_proj_kernel,
        grid=(t // tm,),
        in_specs=[row(D_MODEL), _const_spec(nw.shape), _const_spec(wa.shape), _const_spec(wt.shape)],
        out_specs=[row(SSD_INNER), row(SSD_CONV_DIM), row(LANES), row(DA_WIDTH), col, col],
        out_shape=[
            jax.ShapeDtypeStruct((t, SSD_INNER), BF16),
            jax.ShapeDtypeStruct((t, SSD_CONV_DIM), BF16),
            jax.ShapeDtypeStruct((t, LANES), F32),
            jax.ShapeDtypeStruct((t, DA_WIDTH), BF16),
            jax.ShapeDtypeStruct((DA_WIDTH, t), BF16),
            jax.ShapeDtypeStruct((DA_WIDTH, t), BF16),
        ],
        compiler_params=pltpu.CompilerParams(
            dimension_semantics=("arbitrary",), vmem_limit_bytes=VMEM_LIMIT),
        name="in_proj",
    )(x2, nw, wa, wt)


def _ssd_kernel(z_ref, xbc_ref, dt_ref, cw_ref, cb_ref, dtb_ref, alog_ref, dskip_ref, nw_ref,
                tri3_ref, exp2_ref, shift_ref, y_ref, ext_ref, conv_ref, state_ref, yacc_ref):
    L = SSD_CHUNK
    c = pl.program_id(1)

    @pl.when(c == 0)
    def _():
        state_ref[...] = jnp.zeros_like(state_ref)
        ext_ref[:, 0:CONV_HALO, :] = jnp.zeros((ext_ref.shape[0], CONV_HALO, SSD_CONV_DIM), BF16)

    @pl.when(c > 0)
    def _():
        ext_ref[:, 0:CONV_HALO, :] = ext_ref[:, L:L + CONV_HALO, :]

    consts = (cw_ref, cb_ref, dtb_ref, alog_ref, dskip_ref, nw_ref, tri3_ref, exp2_ref, shift_ref)
    for b in range(z_ref.shape[0]):
        _ssd_chunk(z_ref.at[b], xbc_ref.at[b], dt_ref.at[b], *consts, y_ref.at[b], ext_ref.at[b],
                   conv_ref.at[b], state_ref.at[b], yacc_ref.at[b])


def _ssd_chunk(z_ref, xbc_ref, dt_ref, cw_ref, cb_ref, dtb_ref, alog_ref, dskip_ref, nw_ref,
               tri3_ref, exp2_ref, shift_ref, y_ref, ext_ref, conv_ref, state_ref, yacc_ref):
    L = SSD_CHUNK
    ext_ref[CONV_HALO:CONV_HALO + L, :] = xbc_ref[...]

    for c0 in range(0, SSD_CONV_DIM, 512):
        e = ext_ref[:, c0:c0 + 512]
        cwh = 0.5 * cw_ref[:, c0:c0 + 512]
        acc = 0.5 * cb_ref[:, c0:c0 + 512] + cwh[SSD_CONV - 1:SSD_CONV, :] * e[CONV_HALO:, :].astype(F32)
        shifted = _dot(shift_ref[...], e)
        for back in range(1, SSD_CONV):
            j = SSD_CONV - 1 - back
            acc = acc + cwh[j:j + 1, :] * shifted[(back - 1) * L:back * L, :]
        conv_ref[:, c0:c0 + 512] = _silu_of_twice(acc)

    dtr = dt_ref[...] + dtb_ref[...]
    dtv = jnp.maximum(dtr, 0.0) + jnp.log(1.0 + jnp.exp(-jnp.abs(dtr)))
    adt = dtv * (-jnp.exp(alog_ref[...]))
    cs = _dot(tri3_ref[...], jnp.concatenate(_split3(adt), axis=0)) * LOG2E
    csd_t = (cs - jnp.log2(dtv)).T
    cs_last = cs[L - 1:L, :]
    ecs = jnp.exp2(cs)
    w_state = dtv * jnp.exp2(cs_last - cs)

    def expand(a):
        hi = a.astype(BF16)
        lo = (a - hi.astype(F32)).astype(BF16)
        return _dot(jnp.concatenate([hi, lo], axis=1), exp2_ref[...])

    ecs_x = expand(ecs)
    wst_x = expand(w_state)

    row = lax.broadcasted_iota(jnp.int32, (L, L), 0)
    colm = lax.broadcasted_iota(jnp.int32, (L, L), 1)
    tril = row >= colm
    lane = lax.broadcasted_iota(jnp.int32, (L, LANES), 1)
    lo_half = lane < SSD_HEAD_DIM

    for g in range(SSD_GROUPS):
        gx = g * SSD_GROUP_WIDTH
        bm = conv_ref[:, SSD_INNER + g * SSD_STATE:SSD_INNER + (g + 1) * SSD_STATE]
        cm = conv_ref[:, SSD_INNER + SSD_BC + g * SSD_STATE:SSD_INNER + SSD_BC + (g + 1) * SSD_STATE]
        cm16 = cm.astype(BF16)
        cb = _dot_nt(cm16, bm.astype(BF16))
        xs_g = conv_ref[:, gx:gx + SSD_GROUP_WIDTH]

        y_off = _dot(cm16, state_ref[:, gx:gx + SSD_GROUP_WIDTH].astype(BF16)) * ecs_x[:, gx:gx + SSD_GROUP_WIDTH]
        yacc_ref[:, gx:gx + SSD_GROUP_WIDTH] = y_off + xs_g * dskip_ref[:, gx:gx + SSD_GROUP_WIDTH]

        for pair in range(SSD_HEADS // SSD_GROUPS // 2):
            ms = []
            for h in (g * 8 + 2 * pair, g * 8 + 2 * pair + 1):
                seg = cs[:, h:h + 1] - csd_t[h:h + 1, :]
                ms.append((cb * jnp.exp2(jnp.where(tril, seg, -jnp.inf))).astype(BF16))
            x_pair = conv_ref[:, gx + pair * LANES:gx + (pair + 1) * LANES]
            x_blk = jnp.concatenate(
                [jnp.where(lo_half, x_pair, 0.0), jnp.where(lo_half, 0.0, x_pair)], axis=0).astype(BF16)
            sl = slice(gx + pair * LANES, gx + (pair + 1) * LANES)
            yacc_ref[:, sl] = yacc_ref[:, sl] + _dot(jnp.concatenate(ms, axis=1), x_blk)

        xd = (xs_g * wst_x[:, gx:gx + SSD_GROUP_WIDTH]).astype(BF16)
        contrib = _dot(bm.T.astype(BF16), xd)
        state_ref[:, gx:gx + SSD_GROUP_WIDTH] = (
            state_ref[:, gx:gx + SSD_GROUP_WIDTH] * ecs_x[L - 1:L, gx:gx + SSD_GROUP_WIDTH] + contrib)

        z_half = z_ref[:, gx:gx + SSD_GROUP_WIDTH].astype(F32)
        gy = yacc_ref[:, gx:gx + SSD_GROUP_WIDTH] * _silu_of_twice(z_half)
        y_ref[:, gx:gx + SSD_GROUP_WIDTH] = (
            gy * _rms_scale(gy) * nw_ref[:, gx:gx + SSD_GROUP_WIDTH]).astype(y_ref.dtype)


def _ssd(z, xbc, dt, cw, cb, dtb, alog, dskip_x, nw, tri3, exp2, shift, bsz, seqlen):
    nc = seqlen // SSD_CHUNK
    L = SSD_CHUNK
    rb = SSD_ROWS_PER_STEP
    row = lambda w: pl.BlockSpec((rb, L, w), lambda r, c: (r, c, 0))
    per_batch = lambda a: a.reshape(bsz, seqlen, a.shape[-1])
    consts = (cw, cb, dtb, alog, dskip_x, nw, tri3, exp2, shift)
    y = pl.pallas_call(
        _ssd_kernel,
        grid=(bsz // rb, nc),
        in_specs=[row(SSD_INNER), row(SSD_CONV_DIM), row(LANES)] + [_const_spec(a.shape) for a in consts],
        out_specs=row(SSD_INNER),
        out_shape=jax.ShapeDtypeStruct((bsz, seqlen, SSD_INNER), BF16),
        scratch_shapes=[
            pltpu.VMEM((rb, CONV_HALO + L, SSD_CONV_DIM), BF16),
            pltpu.VMEM((rb, L, SSD_CONV_DIM), F32),
            pltpu.VMEM((rb, SSD_STATE, SSD_INNER), F32),
            pltpu.VMEM((rb, L, SSD_INNER), F32),
        ],
        compiler_params=pltpu.CompilerParams(
            dimension_semantics=("arbitrary", "arbitrary"), vmem_limit_bytes=VMEM_LIMIT),
        name="ssd",
    )(per_batch(z), per_batch(xbc), per_batch(dt), *consts)
    return y.reshape(bsz * seqlen, SSD_INNER)


def _attn_kernel(qt_ref, k_ref, vt_ref, lq1_ref, lk1_ref, lq2_ref, lk2_ref, sw_ref, o_ref,
                 qs_ref, acc_ref, m_ref, mx_ref, s_ref):
    i = pl.program_id(2)
    d_idx = lax.broadcasted_iota(jnp.int32, (DA_V_DIM, TQ), 0)
    ones_rows = jnp.ones((ONES_ROWS, TK), BF16)

    for g in range(HEADS_PER_STEP):
        rows = slice(g * DA_V_DIM, (g + 1) * DA_V_DIM)
        q = qt_ref[rows, :]
        zero = jnp.zeros_like(q)
        qs_ref[rows, 0:TQ] = jnp.where(d_idx < DA_HEAD_DIM, q, zero)
        qs_ref[rows, TQ:2 * TQ] = jnp.where(d_idx < DA_HEAD_DIM, zero, q)

    def scores(g, j, diagonal):
        rows = slice(g * DA_V_DIM, (g + 1) * DA_V_DIM)
        off = pl.multiple_of(j * TK, TK)
        s = _dot(k_ref[pl.ds(off, TK), rows], qs_ref[rows, :])
        if diagonal:
            key = lax.broadcasted_iota(jnp.int32, s.shape, 0)
            qry = lax.broadcasted_iota(jnp.int32, s.shape, 1) & (TQ - 1)
            s = jnp.where(key <= qry, s, -jnp.inf)
        s_ref[g] = s
        mx_ref[g:g + 1, :] = jnp.max(s, axis=0, keepdims=True)

    def softmax_pv(g, j, first):
        rows = slice(g * DA_V_DIM, (g + 1) * DA_V_DIM)
        arow = slice(g * ACC_ROWS, (g + 1) * ACC_ROWS)
        off = pl.multiple_of(j * TK, TK)
        if first:
            m_new = mx_ref[g:g + 1, :]
        else:
            m_prev = m_ref[g:g + 1, :]
            m_new = jnp.maximum(m_prev, mx_ref[g:g + 1, :])
        p = jnp.exp2(s_ref[g] - m_new).astype(BF16)
        v_ext = jnp.concatenate([vt_ref[rows, pl.ds(off, TK)], ones_rows], axis=0)
        pv = _dot(v_ext, p)
        if first:
            acc_ref[arow, :] = pv
        else:
            acc_ref[arow, :] = jnp.exp2(m_prev - m_new) * acc_ref[arow, :] + pv
        m_ref[g:g + 1, :] = m_new

    def run_tiles(tiles, diagonal, next_tile):
        chains = [(t, g) for t in tiles for g in range(HEADS_PER_STEP)]
        for n, (t, g) in enumerate(chains):
            ahead = n + SCORE_LOOKAHEAD
            if ahead < len(chains):
                scores(chains[ahead][1], chains[ahead][0], diagonal)
            else:
                scores(ahead - len(chains), next_tile, False)
            softmax_pv(g, t, first=diagonal)

    for g in range(SCORE_LOOKAHEAD):
        scores(g, i, True)
    run_tiles([i], diagonal=True, next_tile=0)

    def body(jj, carry):
        run_tiles([2 * jj, 2 * jj + 1], diagonal=False, next_tile=2 * jj + 2)
        return carry

    lax.fori_loop(0, i // 2, body, 0)

    @pl.when(i % 2 == 1)
    def _():
        run_tiles([i - 1], diagonal=False, next_tile=i)


    lam = (jnp.exp(jnp.sum(lq1_ref[...] * lk1_ref[...], axis=1, keepdims=True))
           - jnp.exp(jnp.sum(lq2_ref[...] * lk2_ref[...], axis=1, keepdims=True)) + LAMBDA_INIT)
    for g in range(HEADS_PER_STEP):
        a0 = g * ACC_ROWS
        o_all = acc_ref[a0:a0 + DA_V_DIM, :] * (1.0 / acc_ref[a0 + DA_V_DIM:a0 + DA_V_DIM + 1, :])
        o = o_all[:, 0:TQ] - lam * o_all[:, TQ:2 * TQ]
        on = o * lax.rsqrt(jnp.mean(o * o, axis=0, keepdims=True) + EPS)
        o_ref[:, g * DA_V_DIM:(g + 1) * DA_V_DIM] = (
            (on.T * sw_ref[...]) * (1.0 - LAMBDA_INIT)).astype(o_ref.dtype)


def _attention(qt, k, vt, lq1, lk1, lq2, lk2, sw, bsz, seqlen):
    assert TQ == TK
    nq = seqlen // TQ
    gw = HEADS_PER_STEP * DA_V_DIM
    small = (lq1, lk1, lq2, lk2, sw)
    return pl.pallas_call(
        _attn_kernel,
        grid=(bsz, DA_HEADS // HEADS_PER_STEP, nq),
        in_specs=[
            pl.BlockSpec((gw, TQ), lambda b, h, i: (h, b * nq + i)),
            pl.BlockSpec((seqlen, gw), lambda b, h, i: (b, h)),
            pl.BlockSpec((gw, seqlen), lambda b, h, i: (h, b)),
        ] + [_const_spec(a.shape) for a in small],
        out_specs=pl.BlockSpec((TQ, gw), lambda b, h, i: (b * nq + i, h)),
        out_shape=jax.ShapeDtypeStruct((bsz * seqlen, DA_WIDTH), BF16),
        scratch_shapes=[
            pltpu.VMEM((gw, 2 * TQ), BF16),
            pltpu.VMEM((HEADS_PER_STEP * ACC_ROWS, 2 * TQ), F32),
            pltpu.VMEM((HEADS_PER_STEP, 2 * TQ), F32),
            pltpu.VMEM((HEADS_PER_STEP, 2 * TQ), F32),
            pltpu.VMEM((HEADS_PER_STEP, TK, 2 * TQ), F32),
        ],
        compiler_params=pltpu.CompilerParams(
            dimension_semantics=("arbitrary", "arbitrary", "arbitrary"), vmem_limit_bytes=VMEM_LIMIT),
        name="diff_attn",
    )(qt, k, vt, *small)


def _mlp_kernel(x_ref, ys_ref, ya_ref, wo_ref, nw_ref, wg_ref, wu_ref, wd_ref, fw_ref, o_ref, h_ref):
    h_ref[...] = x_ref[...] + _dot(ys_ref[...], wo_ref[0:SSD_INNER, :]) + _dot(ya_ref[...], wo_ref[SSD_INNER:, :])
    h = h_ref[...]
    n2 = (h * _rms_scale(h) * nw_ref[...]).astype(BF16)
    ffn = None
    for f0, f1 in zip(FF_SPLITS[:-1], FF_SPLITS[1:]):
        gate = _dot(n2, wg_ref[:, f0:f1])
        up = _dot(n2, wu_ref[:, f0:f1])
        act = (_silu(gate) * up).astype(BF16)
        down = _dot(act, wd_ref[f0:f1, :])
        ffn = down if ffn is None else ffn + down
    out = h_ref[...] + ffn
    o_ref[...] = out * _rms_scale(out) * fw_ref[...]


def _mlp(x2, ys, ya, wo, nw, wg, wu, wd, fw):
    t = x2.shape[0]
    tm = TM_PROJ
    row = pl.BlockSpec((tm, D_MODEL), lambda i: (i, 0))
    consts = (wo, nw, wg, wu, wd, fw)
    return pl.pallas_call(
        _mlp_kernel,
        grid=(t // tm,),
        in_specs=[row, row, row] + [_const_spec(a.shape) for a in consts],
        out_specs=row,
        out_shape=jax.ShapeDtypeStruct((t, D_MODEL), F32),
        scratch_shapes=[pltpu.VMEM((tm, D_MODEL), F32)],
        compiler_params=pltpu.CompilerParams(
            dimension_semantics=("arbitrary",), vmem_limit_bytes=VMEM_LIMIT),
        name="mlp",
    )(x2, ys, ya, *consts)


def _pad_lanes(v, fill=0.0):
    return jnp.pad(v.astype(F32), (0, LANES - v.shape[0]), constant_values=fill)[None, :]


def kernel(x, mix_norm_w, w_in, conv_w, conv_b, dt_bias, a_log, d_skip, ssd_norm_w, lam_q1, lam_k1, lam_q2,
           lam_k2, subln_w, w_out, ffn_norm_w, w_gate, w_up, w_down, final_norm_w):
    bsz, seqlen, _ = x.shape
    x2 = x.reshape(bsz * seqlen, D_MODEL)

    w = w_in[0]
    o_dt = SSD_INNER + SSD_CONV_DIM
    o_q = o_dt + SSD_HEADS
    w_dt = jnp.pad(w[:, o_dt:o_q], ((0, 0), (0, LANES - SSD_HEADS)))
    wa = jnp.concatenate([w[:, :o_dt], w_dt, w[:, o_q + DA_WIDTH:o_q + 2 * DA_WIDTH]], axis=1).astype(BF16)
    wt = jnp.concatenate([w[:, o_q:o_q + DA_WIDTH], w[:, o_q + 2 * DA_WIDTH:]], axis=1).T.astype(BF16)

    z, xbc, dt, k, qt, vt = _in_proj(x2, mix_norm_w[0][None, :], wa, wt)

    idx = jnp.arange(SSD_CHUNK)
    tri = (idx[:, None] >= idx[None, :]).astype(BF16)
    tri3 = jnp.concatenate([tri, tri, tri], axis=1)
    sel = (jnp.arange(LANES)[:, None] == (jnp.arange(SSD_INNER)[None, :] // SSD_HEAD_DIM)).astype(BF16)
    exp2 = jnp.concatenate([sel, sel], axis=0)
    src = CONV_HALO + idx[None, :, None] - jnp.arange(1, SSD_CONV)[:, None, None]
    shift = (jnp.arange(CONV_HALO + SSD_CHUNK)[None, None, :] == src).astype(BF16)
    shift = shift.reshape((SSD_CONV - 1) * SSD_CHUNK, CONV_HALO + SSD_CHUNK)
    dskip_x = jnp.repeat(d_skip[0].astype(F32), SSD_HEAD_DIM)[None, :]

    y_ssd = _ssd(z, xbc, dt, conv_w[0], conv_b[0][None, :], _pad_lanes(dt_bias[0]), _pad_lanes(a_log[0]),
                 dskip_x, ssd_norm_w[0][None, :], tri3, exp2, shift, bsz, seqlen)
    y_da = _attention(qt, k, vt, lam_q1[0][None, :], lam_k1[0][None, :], lam_q2[0][None, :],
                      lam_k2[0][None, :], subln_w[0][None, :], bsz, seqlen)

    out = _mlp(x2, y_ssd, y_da, w_out[0].astype(BF16), ffn_norm_w[0][None, :], w_gate[0].astype(BF16),
               w_up[0].astype(BF16), w_down[0].astype(BF16), final_norm_w[None, :])
    return out.reshape(bsz, seqlen, D_MODEL)
```

```python
import functools
import math

import jax
import jax.numpy as jnp
from jax import lax
from jax.experimental import pallas as pl
from jax.experimental.pallas import tpu as pltpu

F32 = jnp.float32
BF16 = jnp.bfloat16

EPS = 1e-5
D_MODEL = 1024
SSD_HEADS = 16
SSD_HEAD_DIM = 64
SSD_INNER = SSD_HEADS * SSD_HEAD_DIM
SSD_GROUPS = 2
SSD_GROUP_WIDTH = SSD_INNER // SSD_GROUPS
SSD_STATE = 128
SSD_CONV = 4
SSD_CHUNK = 128
SSD_BC = SSD_GROUPS * SSD_STATE
SSD_CONV_DIM = SSD_INNER + 2 * SSD_BC
DA_HEADS = 8
DA_HEAD_DIM = 64
DA_V_DIM = 2 * DA_HEAD_DIM
DA_WIDTH = DA_HEADS * DA_V_DIM
D_FF = 2816
LAMBDA_INIT = 0.8 - 0.6 * math.exp(-0.3 * 0)

LANES = 128
CONV_HALO = 16
VMEM_LIMIT = 56 * 1024 * 1024

TM_PROJ = 512
TM_IN_PROJ = 1024
TQ = 256
TK = 256
MXU_WIDTH = 256
FF_SPLITS = (0, 6 * MXU_WIDTH, D_FF)
HEADS_PER_STEP = 8
SSD_ROWS_PER_STEP = 4
SCORE_LOOKAHEAD = 2
ONES_ROWS = 16
ACC_ROWS = DA_V_DIM + ONES_ROWS
LOG2E = math.log2(math.e)


def _const_spec(shape):
    nd = len(shape)
    return pl.BlockSpec(shape, lambda *_: (0,) * nd, pipeline_mode=pl.Buffered(1))


def _rms_scale(xf):
    return lax.rsqrt(jnp.mean(xf * xf, axis=-1, keepdims=True) + EPS)


def _dot(a, b):
    return jnp.dot(a, b, preferred_element_type=F32)


def _dot_nt(a, b):
    return lax.dot_general(a, b, (((1,), (1,)), ((), ())), preferred_element_type=F32)


def _split3(a):
    hi = a.astype(BF16)
    r1 = a - hi.astype(F32)
    mid = r1.astype(BF16)
    lo = (r1 - mid.astype(F32)).astype(BF16)
    return hi, mid, lo


def _silu_of_twice(h):
    return h + h * jnp.tanh(h)


def _silu(x):
    return _silu_of_twice(0.5 * x)


def _in_proj_kernel(x_ref, nw_ref, wa_ref, z_ref, xbc_ref, dt_ref, k_ref, qt_ref, vt_ref):
    xf = x_ref[...]
    xn = (xf * _rms_scale(xf) * nw_ref[...]).astype(BF16)
    c0 = 0
    for ref, width, scale, transpose in (
            (z_ref, SSD_INNER, 0.5, False), (xbc_ref, SSD_CONV_DIM, None, False), (dt_ref, LANES, None, False),
            (qt_ref, DA_WIDTH, DA_HEAD_DIM ** -0.5 * LOG2E, True), (k_ref, DA_WIDTH, None, False),
            (vt_ref, DA_WIDTH, None, True)):
        r = _dot(xn, wa_ref[:, c0:c0 + width])
        r = r if scale is None else r * scale
        ref[...] = (r.T if transpose else r).astype(ref.dtype)
        c0 += width


def _in_proj(x2, nw, wa):
    t = x2.shape[0]
    tm = TM_IN_PROJ
    row = lambda w: pl.BlockSpec((tm, w), lambda i: (i, 0))
    col = pl.BlockSpec((DA_WIDTH, tm), lambda i: (0, i))
    return pl.pallas_call(
        _in_proj_kernel,
        grid=(t // tm,),
        in_specs=[row(D_MODEL), _const_spec(nw.shape), _const_spec(wa.shape)],
        out_specs=[row(SSD_INNER), row(SSD_CONV_DIM), row(LANES), row(DA_WIDTH), col, col],
        out_shape=[
            jax.ShapeDtypeStruct((t, SSD_INNER), BF16),
            jax.ShapeDtypeStruct((t, SSD_CONV_DIM), BF16),
            jax.ShapeDtypeStruct((t, LANES), F32),
            jax.ShapeDtypeStruct((t, DA_WIDTH), BF16),
            jax.ShapeDtypeStruct((DA_WIDTH, t), BF16),
            jax.ShapeDtypeStruct((DA_WIDTH, t), BF16),
        ],
        compiler_params=pltpu.CompilerParams(
            dimension_semantics=("arbitrary",), vmem_limit_bytes=VMEM_LIMIT),
        name="in_proj",
    )(x2, nw, wa)


def _ssd_kernel(z_ref, xbc_ref, dt_ref, cw_ref, cb_ref, dtb_ref, alog_ref, dskip_ref, nw_ref,
                tri3_ref, exp2_ref, shift_ref, y_ref, ext_ref, conv_ref, state_ref, yacc_ref):
    L = SSD_CHUNK
    c = pl.program_id(1)

    @pl.when(c == 0)
    def _():
        state_ref[...] = jnp.zeros_like(state_ref)
        ext_ref[:, 0:CONV_HALO, :] = jnp.zeros((ext_ref.shape[0], CONV_HALO, SSD_CONV_DIM), BF16)

    @pl.when(c > 0)
    def _():
        ext_ref[:, 0:CONV_HALO, :] = ext_ref[:, L:L + CONV_HALO, :]

    consts = (cw_ref, cb_ref, dtb_ref, alog_ref, dskip_ref, nw_ref, tri3_ref, exp2_ref, shift_ref)
    _round_robin([
        _ssd_chunk(z_ref.at[b], xbc_ref.at[b], dt_ref.at[b], *consts, y_ref.at[b], ext_ref.at[b],
                   conv_ref.at[b], state_ref.at[b], yacc_ref.at[b])
        for b in range(z_ref.shape[0])])


def _round_robin(stages):
    stages = list(stages)
    while stages:
        for gen in list(stages):
            if next(gen, StopIteration) is StopIteration:
                stages.remove(gen)


def _ssd_chunk(z_ref, xbc_ref, dt_ref, cw_ref, cb_ref, dtb_ref, alog_ref, dskip_ref, nw_ref,
               tri3_ref, exp2_ref, shift_ref, y_ref, ext_ref, conv_ref, state_ref, yacc_ref):
    L = SSD_CHUNK
    ext_ref[CONV_HALO:CONV_HALO + L, :] = xbc_ref[...]

    for c0 in range(0, SSD_CONV_DIM, 512):
        e = ext_ref[:, c0:c0 + 512]
        cwh = 0.5 * cw_ref[:, c0:c0 + 512]
        acc = 0.5 * cb_ref[:, c0:c0 + 512] + cwh[SSD_CONV - 1:SSD_CONV, :] * e[CONV_HALO:, :].astype(F32)
        shifted = _dot(shift_ref[...], e)
        for back in range(1, SSD_CONV):
            j = SSD_CONV - 1 - back
            acc = acc + cwh[j:j + 1, :] * shifted[(back - 1) * L:back * L, :]
        conv_ref[:, c0:c0 + 512] = _silu_of_twice(acc)

    dtr = dt_ref[...] + dtb_ref[...]
    dtv = jnp.maximum(dtr, 0.0) + jnp.log(1.0 + jnp.exp(-jnp.abs(dtr)))
    adt = dtv * (-jnp.exp(alog_ref[...]))
    cs = _dot(tri3_ref[...], jnp.concatenate(_split3(adt), axis=0)) * LOG2E
    yield
    csd_t = (cs - jnp.log2(dtv)).T
    cs_last = cs[L - 1:L, :]
    ecs = jnp.exp2(cs)
    w_state = dtv * jnp.exp2(cs_last - cs)

    def expand(a):
        hi = a.astype(BF16)
        lo = (a - hi.astype(F32)).astype(BF16)
        return _dot(jnp.concatenate([hi, lo], axis=1), exp2_ref[...])

    ecs_x = expand(ecs)
    wst_x = expand(w_state)
    yield

    row = lax.broadcasted_iota(jnp.int32, (L, L), 0)
    colm = lax.broadcasted_iota(jnp.int32, (L, L), 1)
    tril = row >= colm
    lane = lax.broadcasted_iota(jnp.int32, (L, LANES), 1)
    lo_half = lane < SSD_HEAD_DIM

    for g in range(SSD_GROUPS):
        gx = g * SSD_GROUP_WIDTH
        bm = conv_ref[:, SSD_INNER + g * SSD_STATE:SSD_INNER + (g + 1) * SSD_STATE]
        cm = conv_ref[:, SSD_INNER + SSD_BC + g * SSD_STATE:SSD_INNER + SSD_BC + (g + 1) * SSD_STATE]
        cm16 = cm.astype(BF16)
        cb = _dot_nt(cm16, bm.astype(BF16))
        xs_g = conv_ref[:, gx:gx + SSD_GROUP_WIDTH]

        y_off = _dot(cm16, state_ref[:, gx:gx + SSD_GROUP_WIDTH].astype(BF16))
        yield
        yacc_ref[:, gx:gx + SSD_GROUP_WIDTH] = (
            y_off * ecs_x[:, gx:gx + SSD_GROUP_WIDTH] + xs_g * dskip_ref[:, gx:gx + SSD_GROUP_WIDTH])

        for pair in range(SSD_HEADS // SSD_GROUPS // 2):
            ms = []
            for h in (g * 8 + 2 * pair, g * 8 + 2 * pair + 1):
                seg = cs[:, h:h + 1] - csd_t[h:h + 1, :]
                ms.append((cb * jnp.exp2(jnp.where(tril, seg, -jnp.inf))).astype(BF16))
            x_pair = conv_ref[:, gx + pair * LANES:gx + (pair + 1) * LANES]
            x_blk = jnp.concatenate(
                [jnp.where(lo_half, x_pair, 0.0), jnp.where(lo_half, 0.0, x_pair)], axis=0).astype(BF16)
            sl = slice(gx + pair * LANES, gx + (pair + 1) * LANES)
            y_diag = _dot(jnp.concatenate(ms, axis=1), x_blk)
            yield
            yacc_ref[:, sl] = yacc_ref[:, sl] + y_diag

        xd = (xs_g * wst_x[:, gx:gx + SSD_GROUP_WIDTH]).astype(BF16)
        contrib = _dot(bm.T.astype(BF16), xd)
        yield
        state_ref[:, gx:gx + SSD_GROUP_WIDTH] = (
            state_ref[:, gx:gx + SSD_GROUP_WIDTH] * ecs_x[L - 1:L, gx:gx + SSD_GROUP_WIDTH] + contrib)

        z_half = z_ref[:, gx:gx + SSD_GROUP_WIDTH].astype(F32)
        gy = yacc_ref[:, gx:gx + SSD_GROUP_WIDTH] * _silu_of_twice(z_half)
        y_ref[:, gx:gx + SSD_GROUP_WIDTH] = (
            gy * _rms_scale(gy) * nw_ref[:, gx:gx + SSD_GROUP_WIDTH]).astype(y_ref.dtype)


def _ssd(z, xbc, dt, cw, cb, dtb, alog, dskip_x, nw, tri3, exp2, shift, bsz, seqlen):
    nc = seqlen // SSD_CHUNK
    L = SSD_CHUNK
    rb = SSD_ROWS_PER_STEP
    row = lambda w: pl.BlockSpec((rb, L, w), lambda r, c: (r, c, 0))
    per_batch = lambda a: a.reshape(bsz, seqlen, a.shape[-1])
    consts = (cw, cb, dtb, alog, dskip_x, nw, tri3, exp2, shift)
    y = pl.pallas_call(
        _ssd_kernel,
        grid=(bsz // rb, nc),
        in_specs=[row(SSD_INNER), row(SSD_CONV_DIM), row(LANES)] + [_const_spec(a.shape) for a in consts],
        out_specs=row(SSD_INNER),
        out_shape=jax.ShapeDtypeStruct((bsz, seqlen, SSD_INNER), BF16),
        scratch_shapes=[
            pltpu.VMEM((rb, CONV_HALO + L, SSD_CONV_DIM), BF16),
            pltpu.VMEM((rb, L, SSD_CONV_DIM), F32),
            pltpu.VMEM((rb, SSD_STATE, SSD_INNER), F32),
            pltpu.VMEM((rb, L, SSD_INNER), F32),
        ],
        compiler_params=pltpu.CompilerParams(
            dimension_semantics=("arbitrary", "arbitrary"), vmem_limit_bytes=VMEM_LIMIT),
        name="ssd",
    )(per_batch(z), per_batch(xbc), per_batch(dt), *consts)
    return y.reshape(bsz * seqlen, SSD_INNER)


def _attn_kernel(qt_ref, k_ref, vt_ref, lq1_ref, lk1_ref, lq2_ref, lk2_ref, sw_ref, o_ref,
                 qs_ref, acc_ref, m_ref, mx_ref, s_ref):
    i = pl.program_id(2)
    d_idx = lax.broadcasted_iota(jnp.int32, (DA_V_DIM, TQ), 0)
    ones_rows = jnp.ones((ONES_ROWS, TK), BF16)

    for g in range(HEADS_PER_STEP):
        rows = slice(g * DA_V_DIM, (g + 1) * DA_V_DIM)
        q = qt_ref[rows, :]
        zero = jnp.zeros_like(q)
        qs_ref[rows, 0:TQ] = jnp.where(d_idx < DA_HEAD_DIM, q, zero)
        qs_ref[rows, TQ:2 * TQ] = jnp.where(d_idx < DA_HEAD_DIM, zero, q)

    def scores(g, j, diagonal):
        rows = slice(g * DA_V_DIM, (g + 1) * DA_V_DIM)
        off = pl.multiple_of(j * TK, TK)
        s = _dot(k_ref[pl.ds(off, TK), rows], qs_ref[rows, :])
        if diagonal:
            key = lax.broadcasted_iota(jnp.int32, s.shape, 0)
            qry = lax.broadcasted_iota(jnp.int32, s.shape, 1) & (TQ - 1)
            s = jnp.where(key <= qry, s, -jnp.inf)
        s_ref[g] = s
        mx_ref[g:g + 1, :] = jnp.max(s, axis=0, keepdims=True)

    def softmax_pv(g, j, first):
        rows = slice(g * DA_V_DIM, (g + 1) * DA_V_DIM)
        arow = slice(g * ACC_ROWS, (g + 1) * ACC_ROWS)
        off = pl.multiple_of(j * TK, TK)
        if first:
            m_new = mx_ref[g:g + 1, :]
        else:
            m_prev = m_ref[g:g + 1, :]
            m_new = jnp.maximum(m_prev, mx_ref[g:g + 1, :])
        p = jnp.exp2(s_ref[g] - m_new).astype(BF16)
        v_ext = jnp.concatenate([vt_ref[rows, pl.ds(off, TK)], ones_rows], axis=0)
        pv = _dot(v_ext, p)
        if first:
            acc_ref[arow, :] = pv
        else:
            acc_ref[arow, :] = jnp.exp2(m_prev - m_new) * acc_ref[arow, :] + pv
        m_ref[g:g + 1, :] = m_new

    def run_tiles(tiles, diagonal, next_tile):
        chains = [(t, g) for t in tiles for g in range(HEADS_PER_STEP)]
        for n, (t, g) in enumerate(chains):
            ahead = n + SCORE_LOOKAHEAD
            if ahead < len(chains):
                scores(chains[ahead][1], chains[ahead][0], diagonal)
            else:
                scores(ahead - len(chains), next_tile, False)
            softmax_pv(g, t, first=diagonal)

    for g in range(SCORE_LOOKAHEAD):
        scores(g, i, True)
    run_tiles([i], diagonal=True, next_tile=0)

    def body(jj, carry):
        run_tiles([2 * jj, 2 * jj + 1], diagonal=False, next_tile=2 * jj + 2)
        return carry

    lax.fori_loop(0, i // 2, body, 0)

    @pl.when(i % 2 == 1)
    def _():
        run_tiles([i - 1], diagonal=False, next_tile=i)


    lam = (jnp.exp(jnp.sum(lq1_ref[...] * lk1_ref[...], axis=1, keepdims=True))
           - jnp.exp(jnp.sum(lq2_ref[...] * lk2_ref[...], axis=1, keepdims=True)) + LAMBDA_INIT)
    out_gain = sw_ref[...] * (1.0 - LAMBDA_INIT)
    for g in range(HEADS_PER_STEP):
        a0 = g * ACC_ROWS
        inv = 1.0 / acc_ref[a0 + DA_V_DIM:a0 + DA_V_DIM + 1, :]
        o = (acc_ref[a0:a0 + DA_V_DIM, 0:TQ] * inv[:, 0:TQ]
             - acc_ref[a0:a0 + DA_V_DIM, TQ:2 * TQ] * (lam * inv[:, TQ:2 * TQ]))
        on = o * lax.rsqrt(jnp.mean(o * o, axis=0, keepdims=True) + EPS)
        o_ref[:, g * DA_V_DIM:(g + 1) * DA_V_DIM] = (on.T * out_gain).astype(o_ref.dtype)


def _attention(qt, k, vt, lq1, lk1, lq2, lk2, sw, bsz, seqlen):
    assert TQ == TK
    nq = seqlen // TQ
    gw = HEADS_PER_STEP * DA_V_DIM
    small = (lq1, lk1, lq2, lk2, sw)
    return pl.pallas_call(
        _attn_kernel,
        grid=(bsz, DA_HEADS // HEADS_PER_STEP, nq),
        in_specs=[
            pl.BlockSpec((gw, TQ), lambda b, h, i: (h, b * nq + i)),
            pl.BlockSpec((seqlen, gw), lambda b, h, i: (b, h)),
            pl.BlockSpec((gw, seqlen), lambda b, h, i: (h, b)),
        ] + [_const_spec(a.shape) for a in small],
        out_specs=pl.BlockSpec((TQ, gw), lambda b, h, i: (b * nq + i, h)),
        out_shape=jax.ShapeDtypeStruct((bsz * seqlen, DA_WIDTH), BF16),
        scratch_shapes=[
            pltpu.VMEM((gw, 2 * TQ), BF16),
            pltpu.VMEM((HEADS_PER_STEP * ACC_ROWS, 2 * TQ), F32),
            pltpu.VMEM((HEADS_PER_STEP, 2 * TQ), F32),
            pltpu.VMEM((HEADS_PER_STEP, 2 * TQ), F32),
            pltpu.VMEM((HEADS_PER_STEP, TK, 2 * TQ), F32),
        ],
        compiler_params=pltpu.CompilerParams(
            dimension_semantics=("arbitrary", "arbitrary", "arbitrary"), vmem_limit_bytes=VMEM_LIMIT),
        name="diff_attn",
    )(qt, k, vt, *small)


def _mlp_kernel(x_ref, ys_ref, ya_ref, wo_ref, nw_ref, wg_ref, wu_ref, wd_ref, fw_ref, o_ref, h_ref):
    h_ref[...] = x_ref[...] + _dot(ys_ref[...], wo_ref[0:SSD_INNER, :]) + _dot(ya_ref[...], wo_ref[SSD_INNER:, :])
    h = h_ref[...]
    n2 = (h * _rms_scale(h) * nw_ref[...]).astype(BF16)
    ffn = None
    for f0, f1 in zip(FF_SPLITS[:-1], FF_SPLITS[1:]):
        gate = _dot(n2, wg_ref[:, f0:f1])
        up = _dot(n2, wu_ref[:, f0:f1])
        act = (_silu(gate) * up).astype(BF16)
        down = _dot(act, wd_ref[f0:f1, :])
        ffn = down if ffn is None else ffn + down
    out = h_ref[...] + ffn
    o_ref[...] = out * _rms_scale(out) * fw_ref[...]


def _mlp(x2, ys, ya, wo, nw, wg, wu, wd, fw):
    t = x2.shape[0]
    tm = TM_PROJ
    row = pl.BlockSpec((tm, D_MODEL), lambda i: (i, 0))
    consts = (wo, nw, wg, wu, wd, fw)
    return pl.pallas_call(
        _mlp_kernel,
        grid=(t // tm,),
        in_specs=[row, row, row] + [_const_spec(a.shape) for a in consts],
        out_specs=row,
        out_shape=jax.ShapeDtypeStruct((t, D_MODEL), F32),
        scratch_shapes=[pltpu.VMEM((tm, D_MODEL), F32)],
        compiler_params=pltpu.CompilerParams(
            dimension_semantics=("arbitrary",), vmem_limit_bytes=VMEM_LIMIT),
        name="mlp",
    )(x2, ys, ya, *consts)


def _pad_lanes(v, fill=0.0):
    return jnp.pad(v.astype(F32), (0, LANES - v.shape[0]), constant_values=fill)[None, :]


def kernel(x, mix_norm_w, w_in, conv_w, conv_b, dt_bias, a_log, d_skip, ssd_norm_w, lam_q1, lam_k1, lam_q2,
           lam_k2, subln_w, w_out, ffn_norm_w, w_gate, w_up, w_down, final_norm_w):
    bsz, seqlen, _ = x.shape
    x2 = x.reshape(bsz * seqlen, D_MODEL)

    w = w_in[0]
    o_dt = SSD_INNER + SSD_CONV_DIM
    o_q = o_dt + SSD_HEADS
    w_dt = jnp.pad(w[:, o_dt:o_q], ((0, 0), (0, LANES - SSD_HEADS)))
    wa = jnp.concatenate([w[:, :o_dt], w_dt, w[:, o_q:]], axis=1).astype(BF16)

    z, xbc, dt, k, qt, vt = _in_proj(x2, mix_norm_w[0][None, :], wa)

    idx = jnp.arange(SSD_CHUNK)
    tri = (idx[:, None] >= idx[None, :]).astype(BF16)
    tri3 = jnp.concatenate([tri, tri, tri], axis=1)
    sel = (jnp.arange(LANES)[:, None] == (jnp.arange(SSD_INNER)[None, :] // SSD_HEAD_DIM)).astype(BF16)
    exp2 = jnp.concatenate([sel, sel], axis=0)
    src = CONV_HALO + idx[None, :, None] - jnp.arange(1, SSD_CONV)[:, None, None]
    shift = (jnp.arange(CONV_HALO + SSD_CHUNK)[None, None, :] == src).astype(BF16)
    shift = shift.reshape((SSD_CONV - 1) * SSD_CHUNK, CONV_HALO + SSD_CHUNK)
    dskip_x = jnp.repeat(d_skip[0].astype(F32), SSD_HEAD_DIM)[None, :]

    y_ssd = _ssd(z, xbc, dt, conv_w[0], conv_b[0][None, :], _pad_lanes(dt_bias[0]), _pad_lanes(a_log[0]),
                 dskip_x, ssd_norm_w[0][None, :], tri3, exp2, shift, bsz, seqlen)
    y_da = _attention(qt, k, vt, lam_q1[0][None, :], lam_k1[0][None, :], lam_q2[0][None, :],
                      lam_k2[0][None, :], subln_w[0][None, :], bsz, seqlen)

    out = _mlp(x2, y_ssd, y_da, w_out[0].astype(BF16), ffn_norm_w[0][None, :], w_gate[0].astype(BF16),
               w_up[0].astype(BF16), w_down[0].astype(BF16), final_norm_w[None, :])
    return out.reshape(bsz, seqlen, D_MODEL)
```

```python
import functools
import math

import jax
import jax.numpy as jnp
from jax import lax
from jax.experimental import pallas as pl
from jax.experimental.pallas import tpu as pltpu

F32 = jnp.float32
BF16 = jnp.bfloat16

EPS = 1e-5
D_MODEL = 1024
SSD_HEADS = 16
SSD_HEAD_DIM = 64
SSD_INNER = SSD_HEADS * SSD_HEAD_DIM
SSD_GROUPS = 2
SSD_GROUP_WIDTH = SSD_INNER // SSD_GROUPS
SSD_STATE = 128
SSD_CONV = 4
SSD_CHUNK = 128
SSD_BC = SSD_GROUPS * SSD_STATE
SSD_CONV_DIM = SSD_INNER + 2 * SSD_BC
DA_HEADS = 8
DA_HEAD_DIM = 64
DA_V_DIM = 2 * DA_HEAD_DIM
DA_WIDTH = DA_HEADS * DA_V_DIM
D_FF = 2816
LAMBDA_INIT = 0.8 - 0.6 * math.exp(-0.3 * 0)

LANES = 128
CONV_HALO = 16
VMEM_LIMIT = 56 * 1024 * 1024

TM_PROJ = 512
TM_IN_PROJ = 1024
TQ = 256
TK = 256
MXU_WIDTH = 256
FF_SPLITS = (0, 6 * MXU_WIDTH, D_FF)
HEADS_PER_STEP = 8
SSD_ROWS_PER_STEP = 4
SCORE_LOOKAHEAD = 2
ONES_ROWS = 16
ACC_ROWS = DA_V_DIM + ONES_ROWS
LOG2E = math.log2(math.e)


def _const_spec(shape):
    nd = len(shape)
    return pl.BlockSpec(shape, lambda *_: (0,) * nd, pipeline_mode=pl.Buffered(1))


def _rms_scale(xf):
    return lax.rsqrt(jnp.mean(xf * xf, axis=-1, keepdims=True) + EPS)


def _dot(a, b):
    return jnp.dot(a, b, preferred_element_type=F32)


def _dot_nt(a, b):
    return lax.dot_general(a, b, (((1,), (1,)), ((), ())), preferred_element_type=F32)


def _split3(a):
    hi = a.astype(BF16)
    r1 = a - hi.astype(F32)
    mid = r1.astype(BF16)
    lo = (r1 - mid.astype(F32)).astype(BF16)
    return hi, mid, lo


def _silu_of_twice(h):
    return h + h * jnp.tanh(h)


def _silu(x):
    return _silu_of_twice(0.5 * x)


def _in_proj_kernel(x_ref, nw_ref, wzx_ref, wdt_ref, wqkv_ref, z_ref, xbc_ref, dt_ref, k_ref, qt_ref, vt_ref):
    xf = x_ref[...]
    xn = (xf * _rms_scale(xf) * nw_ref[...]).astype(BF16)
    for w_ref, c0, ref, width, scale, transpose in (
            (wzx_ref, 0, z_ref, SSD_INNER, 0.5, False),
            (wzx_ref, SSD_INNER, xbc_ref, SSD_CONV_DIM, None, False),
            (wdt_ref, 0, dt_ref, LANES, None, False),
            (wqkv_ref, 0, qt_ref, DA_WIDTH, DA_HEAD_DIM ** -0.5 * LOG2E, True),
            (wqkv_ref, DA_WIDTH, k_ref, DA_WIDTH, None, False),
            (wqkv_ref, 2 * DA_WIDTH, vt_ref, DA_WIDTH, None, True)):
        r = _dot(xn, w_ref[:, c0:c0 + width])
        r = r if scale is None else r * scale
        ref[...] = (r.T if transpose else r).astype(ref.dtype)


def _in_proj(x2, nw, w_zx, w_dt, w_qkv):
    t = x2.shape[0]
    tm = TM_IN_PROJ
    row = lambda w: pl.BlockSpec((tm, w), lambda i: (i, 0))
    col = pl.BlockSpec((DA_WIDTH, tm), lambda i: (0, i))
    return pl.pallas_call(
        _in_proj_kernel,
        grid=(t // tm,),
        in_specs=[row(D_MODEL)] + [_const_spec(a.shape) for a in (nw, w_zx, w_dt, w_qkv)],
        out_specs=[row(SSD_INNER), row(SSD_CONV_DIM), row(LANES), row(DA_WIDTH), col, col],
        out_shape=[
            jax.ShapeDtypeStruct((t, SSD_INNER), BF16),
            jax.ShapeDtypeStruct((t, SSD_CONV_DIM), BF16),
            jax.ShapeDtypeStruct((t, LANES), F32),
            jax.ShapeDtypeStruct((t, DA_WIDTH), BF16),
            jax.ShapeDtypeStruct((DA_WIDTH, t), BF16),
            jax.ShapeDtypeStruct((DA_WIDTH, t), BF16),
        ],
        compiler_params=pltpu.CompilerParams(
            dimension_semantics=("arbitrary",), vmem_limit_bytes=VMEM_LIMIT),
        name="in_proj",
    )(x2, nw, w_zx, w_dt, w_qkv)


def _ssd_kernel(z_ref, xbc_ref, dt_ref, cw_ref, cb_ref, dtb_ref, alog_ref, dskip_ref, nw_ref,
                tri3_ref, exp2_ref, shift_ref, y_ref, ext_ref, conv_ref, state_ref, yacc_ref):
    L = SSD_CHUNK
    c = pl.program_id(1)

    @pl.when(c == 0)
    def _():
        state_ref[...] = jnp.zeros_like(state_ref)
        ext_ref[:, 0:CONV_HALO, :] = jnp.zeros((ext_ref.shape[0], CONV_HALO, SSD_CONV_DIM), BF16)

    @pl.when(c > 0)
    def _():
        ext_ref[:, 0:CONV_HALO, :] = ext_ref[:, L:L + CONV_HALO, :]

    consts = (cw_ref, cb_ref, dtb_ref, alog_ref, dskip_ref, nw_ref, tri3_ref, exp2_ref, shift_ref)
    _round_robin([
        _ssd_chunk(z_ref.at[b], xbc_ref.at[b], dt_ref.at[b], *consts, y_ref.at[b], ext_ref.at[b],
                   conv_ref.at[b], state_ref.at[b], yacc_ref.at[b])
        for b in range(z_ref.shape[0])])


def _round_robin(stages):
    stages = list(stages)
    while stages:
        for gen in list(stages):
            if next(gen, StopIteration) is StopIteration:
                stages.remove(gen)


def _ssd_chunk(z_ref, xbc_ref, dt_ref, cw_ref, cb_ref, dtb_ref, alog_ref, dskip_ref, nw_ref,
               tri3_ref, exp2_ref, shift_ref, y_ref, ext_ref, conv_ref, state_ref, yacc_ref):
    L = SSD_CHUNK
    ext_ref[CONV_HALO:CONV_HALO + L, :] = xbc_ref[...]

    for c0 in range(0, SSD_CONV_DIM, 512):
        e = ext_ref[:, c0:c0 + 512]
        cwh = 0.5 * cw_ref[:, c0:c0 + 512]
        acc = 0.5 * cb_ref[:, c0:c0 + 512] + cwh[SSD_CONV - 1:SSD_CONV, :] * e[CONV_HALO:, :].astype(F32)
        shifted = _dot(shift_ref[...], e)
        for back in range(1, SSD_CONV):
            j = SSD_CONV - 1 - back
            acc = acc + cwh[j:j + 1, :] * shifted[(back - 1) * L:back * L, :]
        conv_ref[:, c0:c0 + 512] = _silu_of_twice(acc)

    dtr = dt_ref[...] + dtb_ref[...]
    dtv = jnp.maximum(dtr, 0.0) + jnp.log(1.0 + jnp.exp(-jnp.abs(dtr)))
    adt = dtv * (-jnp.exp(alog_ref[...]))
    cs = _dot(tri3_ref[...], jnp.concatenate(_split3(adt), axis=0)) * LOG2E
    yield
    csd_t = (cs - jnp.log2(dtv)).T
    cs_last = cs[L - 1:L, :]
    ecs = jnp.exp2(cs)
    w_state = dtv * jnp.exp2(cs_last - cs)

    def expand(a):
        hi = a.astype(BF16)
        lo = (a - hi.astype(F32)).astype(BF16)
        return _dot(jnp.concatenate([hi, lo], axis=1), exp2_ref[...])

    ecs_x = expand(ecs)
    wst_x = expand(w_state)
    yield

    row = lax.broadcasted_iota(jnp.int32, (L, L), 0)
    colm = lax.broadcasted_iota(jnp.int32, (L, L), 1)
    tril = row >= colm
    lane = lax.broadcasted_iota(jnp.int32, (L, LANES), 1)
    lo_half = lane < SSD_HEAD_DIM

    for g in range(SSD_GROUPS):
        gx = g * SSD_GROUP_WIDTH
        bm = conv_ref[:, SSD_INNER + g * SSD_STATE:SSD_INNER + (g + 1) * SSD_STATE]
        cm = conv_ref[:, SSD_INNER + SSD_BC + g * SSD_STATE:SSD_INNER + SSD_BC + (g + 1) * SSD_STATE]
        cm16 = cm.astype(BF16)
        cb = _dot_nt(cm16, bm.astype(BF16))
        xs_g = conv_ref[:, gx:gx + SSD_GROUP_WIDTH]

        y_off = _dot(cm16, state_ref[:, gx:gx + SSD_GROUP_WIDTH].astype(BF16))
        yield
        yacc_ref[:, gx:gx + SSD_GROUP_WIDTH] = (
            y_off * ecs_x[:, gx:gx + SSD_GROUP_WIDTH] + xs_g * dskip_ref[:, gx:gx + SSD_GROUP_WIDTH])

        for pair in range(SSD_HEADS // SSD_GROUPS // 2):
            ms = []
            for h in (g * 8 + 2 * pair, g * 8 + 2 * pair + 1):
                seg = cs[:, h:h + 1] - csd_t[h:h + 1, :]
                ms.append((cb * jnp.exp2(jnp.where(tril, seg, -jnp.inf))).astype(BF16))
            x_pair = conv_ref[:, gx + pair * LANES:gx + (pair + 1) * LANES]
            x_blk = jnp.concatenate(
                [jnp.where(lo_half, x_pair, 0.0), jnp.where(lo_half, 0.0, x_pair)], axis=0).astype(BF16)
            sl = slice(gx + pair * LANES, gx + (pair + 1) * LANES)
            y_diag = _dot(jnp.concatenate(ms, axis=1), x_blk)
            yield
            yacc_ref[:, sl] = yacc_ref[:, sl] + y_diag

        xd = (xs_g * wst_x[:, gx:gx + SSD_GROUP_WIDTH]).astype(BF16)
        contrib = _dot(bm.T.astype(BF16), xd)
        yield
        state_ref[:, gx:gx + SSD_GROUP_WIDTH] = (
            state_ref[:, gx:gx + SSD_GROUP_WIDTH] * ecs_x[L - 1:L, gx:gx + SSD_GROUP_WIDTH] + contrib)

        z_half = z_ref[:, gx:gx + SSD_GROUP_WIDTH].astype(F32)
        gy = yacc_ref[:, gx:gx + SSD_GROUP_WIDTH] * _silu_of_twice(z_half)
        y_ref[:, gx:gx + SSD_GROUP_WIDTH] = (
            gy * _rms_scale(gy) * nw_ref[:, gx:gx + SSD_GROUP_WIDTH]).astype(y_ref.dtype)


def _ssd(z, xbc, dt, cw, cb, dtb, alog, dskip_x, nw, tri3, exp2, shift, bsz, seqlen):
    nc = seqlen // SSD_CHUNK
    L = SSD_CHUNK
    rb = SSD_ROWS_PER_STEP
    row = lambda w: pl.BlockSpec((rb, L, w), lambda r, c: (r, c, 0))
    per_batch = lambda a: a.reshape(bsz, seqlen, a.shape[-1])
    consts = (cw, cb, dtb, alog, dskip_x, nw, tri3, exp2, shift)
    y = pl.pallas_call(
        _ssd_kernel,
        grid=(bsz // rb, nc),
        in_specs=[row(SSD_INNER), row(SSD_CONV_DIM), row(LANES)] + [_const_spec(a.shape) for a in consts],
        out_specs=row(SSD_INNER),
        out_shape=jax.ShapeDtypeStruct((bsz, seqlen, SSD_INNER), BF16),
        scratch_shapes=[
            pltpu.VMEM((rb, CONV_HALO + L, SSD_CONV_DIM), BF16),
            pltpu.VMEM((rb, L, SSD_CONV_DIM), F32),
            pltpu.VMEM((rb, SSD_STATE, SSD_INNER), F32),
            pltpu.VMEM((rb, L, SSD_INNER), F32),
        ],
        compiler_params=pltpu.CompilerParams(
            dimension_semantics=("arbitrary", "arbitrary"), vmem_limit_bytes=VMEM_LIMIT),
        name="ssd",
    )(per_batch(z), per_batch(xbc), per_batch(dt), *consts)
    return y.reshape(bsz * seqlen, SSD_INNER)


def _attn_kernel(qt_ref, k_ref, vt_ref, lq1_ref, lk1_ref, lq2_ref, lk2_ref, sw_ref, o_ref,
                 qs_ref, acc_ref, m_ref, mx_ref, s_ref):
    i = pl.program_id(2)
    d_idx = lax.broadcasted_iota(jnp.int32, (DA_V_DIM, TQ), 0)
    ones_rows = jnp.ones((ONES_ROWS, TK), BF16)

    for g in range(HEADS_PER_STEP):
        rows = slice(g * DA_V_DIM, (g + 1) * DA_V_DIM)
        q = qt_ref[rows, :]
        zero = jnp.zeros_like(q)
        qs_ref[rows, 0:TQ] = jnp.where(d_idx < DA_HEAD_DIM, q, zero)
        qs_ref[rows, TQ:2 * TQ] = jnp.where(d_idx < DA_HEAD_DIM, zero, q)

    def scores(g, j, diagonal):
        rows = slice(g * DA_V_DIM, (g + 1) * DA_V_DIM)
        off = pl.multiple_of(j * TK, TK)
        s = _dot(k_ref[pl.ds(off, TK), rows], qs_ref[rows, :])
        if diagonal:
            key = lax.broadcasted_iota(jnp.int32, s.shape, 0)
            qry = lax.broadcasted_iota(jnp.int32, s.shape, 1) & (TQ - 1)
            s = jnp.where(key <= qry, s, -jnp.inf)
        s_ref[g] = s
        mx_ref[g:g + 1, :] = jnp.max(s, axis=0, keepdims=True)

    def softmax_pv(g, j, first):
        rows = slice(g * DA_V_DIM, (g + 1) * DA_V_DIM)
        arow = slice(g * ACC_ROWS, (g + 1) * ACC_ROWS)
        off = pl.multiple_of(j * TK, TK)
        if first:
            m_new = mx_ref[g:g + 1, :]
        else:
            m_prev = m_ref[g:g + 1, :]
            m_new = jnp.maximum(m_prev, mx_ref[g:g + 1, :])
        p = jnp.exp2(s_ref[g] - m_new).astype(BF16)
        v_ext = jnp.concatenate([vt_ref[rows, pl.ds(off, TK)], ones_rows], axis=0)
        pv = _dot(v_ext, p)
        if first:
            acc_ref[arow, :] = pv
        else:
            acc_ref[arow, :] = jnp.exp2(m_prev - m_new) * acc_ref[arow, :] + pv
        m_ref[g:g + 1, :] = m_new

    def run_tiles(tiles, diagonal, next_tile):
        chains = [(t, g) for t in tiles for g in range(HEADS_PER_STEP)]
        for n, (t, g) in enumerate(chains):
            ahead = n + SCORE_LOOKAHEAD
            if ahead < len(chains):
                scores(chains[ahead][1], chains[ahead][0], diagonal)
            else:
                scores(ahead - len(chains), next_tile, False)
            softmax_pv(g, t, first=diagonal)

    for g in range(SCORE_LOOKAHEAD):
        scores(g, i, True)
    run_tiles([i], diagonal=True, next_tile=0)

    def body(jj, carry):
        run_tiles([2 * jj, 2 * jj + 1], diagonal=False, next_tile=2 * jj + 2)
        return carry

    lax.fori_loop(0, i // 2, body, 0)

    @pl.when(i % 2 == 1)
    def _():
        run_tiles([i - 1], diagonal=False, next_tile=i)


    lam = (jnp.exp(jnp.sum(lq1_ref[...] * lk1_ref[...], axis=1, keepdims=True))
           - jnp.exp(jnp.sum(lq2_ref[...] * lk2_ref[...], axis=1, keepdims=True)) + LAMBDA_INIT)
    out_gain = sw_ref[...] * (1.0 - LAMBDA_INIT)
    for g in range(HEADS_PER_STEP):
        a0 = g * ACC_ROWS
        inv = 1.0 / acc_ref[a0 + DA_V_DIM:a0 + DA_V_DIM + 1, :]
        o = (acc_ref[a0:a0 + DA_V_DIM, 0:TQ] * inv[:, 0:TQ]
             - acc_ref[a0:a0 + DA_V_DIM, TQ:2 * TQ] * (lam * inv[:, TQ:2 * TQ]))
        on = o * lax.rsqrt(jnp.mean(o * o, axis=0, keepdims=True) + EPS)
        o_ref[:, g * DA_V_DIM:(g + 1) * DA_V_DIM] = (on.T * out_gain).astype(o_ref.dtype)


def _attention(qt, k, vt, lq1, lk1, lq2, lk2, sw, bsz, seqlen):
    assert TQ == TK
    nq = seqlen // TQ
    gw = HEADS_PER_STEP * DA_V_DIM
    small = (lq1, lk1, lq2, lk2, sw)
    return pl.pallas_call(
        _attn_kernel,
        grid=(bsz, DA_HEADS // HEADS_PER_STEP, nq),
        in_specs=[
            pl.BlockSpec((gw, TQ), lambda b, h, i: (h, b * nq + i)),
            pl.BlockSpec((seqlen, gw), lambda b, h, i: (b, h)),
            pl.BlockSpec((gw, seqlen), lambda b, h, i: (h, b)),
        ] + [_const_spec(a.shape) for a in small],
        out_specs=pl.BlockSpec((TQ, gw), lambda b, h, i: (b * nq + i, h)),
        out_shape=jax.ShapeDtypeStruct((bsz * seqlen, DA_WIDTH), BF16),
        scratch_shapes=[
            pltpu.VMEM((gw, 2 * TQ), BF16),
            pltpu.VMEM((HEADS_PER_STEP * ACC_ROWS, 2 * TQ), F32),
            pltpu.VMEM((HEADS_PER_STEP, 2 * TQ), F32),
            pltpu.VMEM((HEADS_PER_STEP, 2 * TQ), F32),
            pltpu.VMEM((HEADS_PER_STEP, TK, 2 * TQ), F32),
        ],
        compiler_params=pltpu.CompilerParams(
            dimension_semantics=("arbitrary", "arbitrary", "arbitrary"), vmem_limit_bytes=VMEM_LIMIT),
        name="diff_attn",
    )(qt, k, vt, *small)


def _mlp_kernel(x_ref, ys_ref, ya_ref, wo_ref, nw_ref, wg_ref, wu_ref, wd_ref, fw_ref, o_ref, h_ref):
    h_ref[...] = x_ref[...] + _dot(ys_ref[...], wo_ref[0:SSD_INNER, :]) + _dot(ya_ref[...], wo_ref[SSD_INNER:, :])
    h = h_ref[...]
    n2 = (h * _rms_scale(h) * nw_ref[...]).astype(BF16)
    ffn = None
    for f0, f1 in zip(FF_SPLITS[:-1], FF_SPLITS[1:]):
        gate = _dot(n2, wg_ref[:, f0:f1])
        up = _dot(n2, wu_ref[:, f0:f1])
        act = (_silu(gate) * up).astype(BF16)
        down = _dot(act, wd_ref[f0:f1, :])
        ffn = down if ffn is None else ffn + down
    out = h_ref[...] + ffn
    o_ref[...] = out * _rms_scale(out) * fw_ref[...]


def _mlp(x2, ys, ya, wo, nw, wg, wu, wd, fw):
    t = x2.shape[0]
    tm = TM_PROJ
    row = pl.BlockSpec((tm, D_MODEL), lambda i: (i, 0))
    consts = (wo, nw, wg, wu, wd, fw)
    return pl.pallas_call(
        _mlp_kernel,
        grid=(t // tm,),
        in_specs=[row, row, row] + [_const_spec(a.shape) for a in consts],
        out_specs=row,
        out_shape=jax.ShapeDtypeStruct((t, D_MODEL), F32),
        scratch_shapes=[pltpu.VMEM((tm, D_MODEL), F32)],
        compiler_params=pltpu.CompilerParams(
            dimension_semantics=("arbitrary",), vmem_limit_bytes=VMEM_LIMIT),
        name="mlp",
    )(x2, ys, ya, *consts)


def _pad_lanes(v, fill=0.0):
    return jnp.pad(v.astype(F32), (0, LANES - v.shape[0]), constant_values=fill)[None, :]


def kernel(x, mix_norm_w, w_in, conv_w, conv_b, dt_bias, a_log, d_skip, ssd_norm_w, lam_q1, lam_k1, lam_q2,
           lam_k2, subln_w, w_out, ffn_norm_w, w_gate, w_up, w_down, final_norm_w):
    bsz, seqlen, _ = x.shape
    x2 = x.reshape(bsz * seqlen, D_MODEL)

    w = w_in[0]
    o_dt = SSD_INNER + SSD_CONV_DIM
    o_q = o_dt + SSD_HEADS
    w_dt = jnp.pad(w[:, o_dt:o_q], ((0, 0), (0, LANES - SSD_HEADS))).astype(BF16)

    z, xbc, dt, k, qt, vt = _in_proj(x2, mix_norm_w[0][None, :], w[:, :o_dt].astype(BF16), w_dt,
                                     w[:, o_q:].astype(BF16))

    idx = jnp.arange(SSD_CHUNK)
    tri = (idx[:, None] >= idx[None, :]).astype(BF16)
    tri3 = jnp.concatenate([tri, tri, tri], axis=1)
    sel = (jnp.arange(LANES)[:, None] == (jnp.arange(SSD_INNER)[None, :] // SSD_HEAD_DIM)).astype(BF16)
    exp2 = jnp.concatenate([sel, sel], axis=0)
    src = CONV_HALO + idx[None, :, None] - jnp.arange(1, SSD_CONV)[:, None, None]
    shift = (jnp.arange(CONV_HALO + SSD_CHUNK)[None, None, :] == src).astype(BF16)
    shift = shift.reshape((SSD_CONV - 1) * SSD_CHUNK, CONV_HALO + SSD_CHUNK)
    dskip_x = jnp.repeat(d_skip[0].astype(F32), SSD_HEAD_DIM)[None, :]

    y_ssd = _ssd(z, xbc, dt, conv_w[0], conv_b[0][None, :], _pad_lanes(dt_bias[0]), _pad_lanes(a_log[0]),
                 dskip_x, ssd_norm_w[0][None, :], tri3, exp2, shift, bsz, seqlen)
    y_da = _attention(qt, k, vt, lam_q1[0][None, :], lam_k1[0][None, :], lam_q2[0][None, :],
                      lam_k2[0][None, :], subln_w[0][None, :], bsz, seqlen)

    out = _mlp(x2, y_ssd, y_da, w_out[0].astype(BF16), ffn_norm_w[0][None, :], w_gate[0].astype(BF16),
               w_up[0].astype(BF16), w_down[0].astype(BF16), final_norm_w[None, :])
    return out.reshape(bsz, seqlen, D_MODEL)
```

```python
import math

import jax
import jax.numpy as jnp
from jax import lax
from jax.experimental import pallas as pl
from jax.experimental.pallas import tpu as pltpu

F32 = jnp.float32
BF16 = jnp.bfloat16

EPS = 1e-5
D_MODEL = 1024
SSD_HEADS = 16
SSD_HEAD_DIM = 64
SSD_INNER = SSD_HEADS * SSD_HEAD_DIM
SSD_GROUPS = 2
SSD_GROUP_WIDTH = SSD_INNER // SSD_GROUPS
SSD_STATE = 128
SSD_CONV = 4
SSD_CHUNK = 128
SSD_BC = SSD_GROUPS * SSD_STATE
SSD_CONV_DIM = SSD_INNER + 2 * SSD_BC
DA_HEADS = 8
DA_HEAD_DIM = 64
DA_V_DIM = 2 * DA_HEAD_DIM
DA_WIDTH = DA_HEADS * DA_V_DIM
D_FF = 2816
LAMBDA_INIT = 0.8 - 0.6 * math.exp(-0.3 * 0)

LANES = 128
CONV_HALO = 16
VMEM_LIMIT = 56 * 1024 * 1024

TM_PROJ = 512
TM_IN_PROJ = 1024
TQ = 256
TK = 256
MXU_WIDTH = 256
FF_SPLITS = (0, 6 * MXU_WIDTH, D_FF)
HEADS_PER_STEP = 8
SSD_ROWS_PER_STEP = 4
SCORE_LOOKAHEAD = 2
ONES_ROWS = 16
ACC_ROWS = DA_V_DIM + ONES_ROWS
LOG2E = math.log2(math.e)


def _const_spec(shape):
    nd = len(shape)
    return pl.BlockSpec(shape, lambda *_: (0,) * nd, pipeline_mode=pl.Buffered(1))


def _rms_scale(xf):
    return lax.rsqrt(jnp.mean(xf * xf, axis=-1, keepdims=True) + EPS)


def _dot(a, b):
    return jnp.dot(a, b, preferred_element_type=F32)


def _dot_nt(a, b):
    return lax.dot_general(a, b, (((1,), (1,)), ((), ())), preferred_element_type=F32)


def _split3(a):
    hi = a.astype(BF16)
    r1 = a - hi.astype(F32)
    mid = r1.astype(BF16)
    lo = (r1 - mid.astype(F32)).astype(BF16)
    return hi, mid, lo


def _silu_of_twice(h):
    return h + h * jnp.tanh(h)


def _silu(x):
    return _silu_of_twice(0.5 * x)


def _in_proj_kernel(x_ref, nw_ref, wt_ref, z_ref, xbc_ref, dt_ref, k_ref, qt_ref, vt_ref):
    xf = x_ref[...]
    xn = (xf * _rms_scale(xf) * nw_ref[...]).astype(BF16)
    r0 = 0
    for ref, width, scale, transposed in (
            (z_ref, SSD_INNER, 0.5, False), (xbc_ref, SSD_CONV_DIM, None, False), (dt_ref, SSD_HEADS, None, True),
            (qt_ref, DA_WIDTH, DA_HEAD_DIM ** -0.5 * LOG2E, True), (k_ref, DA_WIDTH, None, False),
            (vt_ref, DA_WIDTH, None, True)):
        w = wt_ref[r0:r0 + width, :]
        r = _dot_nt(w, xn) if transposed else _dot_nt(xn, w)
        ref[...] = (r if scale is None else r * scale).astype(ref.dtype)
        r0 += width


def _in_proj(x2, nw, wt):
    t = x2.shape[0]
    tm = TM_IN_PROJ
    row = lambda w: pl.BlockSpec((tm, w), lambda i: (i, 0))
    col = pl.BlockSpec((DA_WIDTH, tm), lambda i: (0, i))
    return pl.pallas_call(
        _in_proj_kernel,
        grid=(t // tm,),
        in_specs=[row(D_MODEL), _const_spec(nw.shape), _const_spec(wt.shape)],
        out_specs=[row(SSD_INNER), row(SSD_CONV_DIM), pl.BlockSpec((SSD_HEADS, tm), lambda i: (0, i)),
                   row(DA_WIDTH), col, col],
        out_shape=[
            jax.ShapeDtypeStruct((t, SSD_INNER), BF16),
            jax.ShapeDtypeStruct((t, SSD_CONV_DIM), BF16),
            jax.ShapeDtypeStruct((SSD_HEADS, t), F32),
            jax.ShapeDtypeStruct((t, DA_WIDTH), BF16),
            jax.ShapeDtypeStruct((DA_WIDTH, t), BF16),
            jax.ShapeDtypeStruct((DA_WIDTH, t), BF16),
        ],
        compiler_params=pltpu.CompilerParams(
            dimension_semantics=("arbitrary",), vmem_limit_bytes=VMEM_LIMIT),
        name="in_proj",
    )(x2, nw, wt)


def _ssd_kernel(z_ref, xbc_ref, dt_ref, cw_ref, cb_ref, dtb_ref, alog_ref, dskip_ref, nw_ref,
                tri3_ref, exp2_ref, shift_ref, y_ref, ext_ref, conv_ref, state_ref, yacc_ref):
    L = SSD_CHUNK
    c = pl.program_id(1)

    @pl.when(c == 0)
    def _():
        state_ref[...] = jnp.zeros_like(state_ref)
        ext_ref[:, 0:CONV_HALO, :] = jnp.zeros((ext_ref.shape[0], CONV_HALO, SSD_CONV_DIM), BF16)

    @pl.when(c > 0)
    def _():
        ext_ref[:, 0:CONV_HALO, :] = ext_ref[:, L:L + CONV_HALO, :]

    consts = (cw_ref, cb_ref, dtb_ref, alog_ref, dskip_ref, nw_ref, tri3_ref, exp2_ref, shift_ref)
    _round_robin([
        _ssd_chunk(z_ref.at[b], xbc_ref.at[b], dt_ref.at[b], *consts, y_ref.at[b], ext_ref.at[b],
                   conv_ref.at[b], state_ref.at[b], yacc_ref.at[b])
        for b in range(z_ref.shape[0])])


def _round_robin(stages):
    stages = list(stages)
    while stages:
        for gen in list(stages):
            if next(gen, StopIteration) is StopIteration:
                stages.remove(gen)


def _ssd_chunk(z_ref, xbc_ref, dt_ref, cw_ref, cb_ref, dtb_ref, alog_ref, dskip_ref, nw_ref,
               tri3_ref, exp2_ref, shift_ref, y_ref, ext_ref, conv_ref, state_ref, yacc_ref):
    L = SSD_CHUNK
    ext_ref[CONV_HALO:CONV_HALO + L, :] = xbc_ref[...]

    for c0 in range(0, SSD_CONV_DIM, 512):
        e = ext_ref[:, c0:c0 + 512]
        cwh = 0.5 * cw_ref[:, c0:c0 + 512]
        acc = 0.5 * cb_ref[:, c0:c0 + 512] + cwh[SSD_CONV - 1:SSD_CONV, :] * e[CONV_HALO:, :].astype(F32)
        shifted = _dot(shift_ref[...], e)
        for back in range(1, SSD_CONV):
            j = SSD_CONV - 1 - back
            acc = acc + cwh[j:j + 1, :] * shifted[(back - 1) * L:back * L, :]
        conv_ref[:, c0:c0 + 512] = _silu_of_twice(acc)

    dtr = dt_ref[...] + dtb_ref[...]
    dtv = jnp.maximum(dtr, 0.0) + jnp.log(1.0 + jnp.exp(-jnp.abs(dtr)))
    adt = dtv * (-jnp.exp(alog_ref[...]))
    cs_t = _dot(jnp.concatenate(_split3(adt), axis=1), tri3_ref[...]) * LOG2E
    yield
    csd_t = cs_t - jnp.log2(dtv)
    w_state_t = dtv * jnp.exp2(cs_t[:, L - 1:L] - cs_t)

    def by_time(a_t):
        return jnp.concatenate([a_t, jnp.zeros((LANES - SSD_HEADS, L), F32)], axis=0).T

    cs = by_time(cs_t)
    ecs = by_time(jnp.exp2(cs_t))
    w_state = by_time(w_state_t)

    def expand(a):
        hi = a.astype(BF16)
        lo = (a - hi.astype(F32)).astype(BF16)
        return _dot(jnp.concatenate([hi, lo], axis=1), exp2_ref[...])

    ecs_x = expand(ecs)
    wst_x = expand(w_state)
    yield

    row = lax.broadcasted_iota(jnp.int32, (L, L), 0)
    colm = lax.broadcasted_iota(jnp.int32, (L, L), 1)
    tril = row >= colm
    lane = lax.broadcasted_iota(jnp.int32, (L, LANES), 1)
    lo_half = lane < SSD_HEAD_DIM

    for g in range(SSD_GROUPS):
        gx = g * SSD_GROUP_WIDTH
        bm = conv_ref[:, SSD_INNER + g * SSD_STATE:SSD_INNER + (g + 1) * SSD_STATE]
        cm = conv_ref[:, SSD_INNER + SSD_BC + g * SSD_STATE:SSD_INNER + SSD_BC + (g + 1) * SSD_STATE]
        cm16 = cm.astype(BF16)
        cb = _dot_nt(cm16, bm.astype(BF16))
        xs_g = conv_ref[:, gx:gx + SSD_GROUP_WIDTH]

        y_off = _dot(cm16, state_ref[:, gx:gx + SSD_GROUP_WIDTH].astype(BF16))
        yield
        yacc_ref[:, gx:gx + SSD_GROUP_WIDTH] = (
            y_off * ecs_x[:, gx:gx + SSD_GROUP_WIDTH] + xs_g * dskip_ref[:, gx:gx + SSD_GROUP_WIDTH])

        for pair in range(SSD_HEADS // SSD_GROUPS // 2):
            ms = []
            for h in (g * 8 + 2 * pair, g * 8 + 2 * pair + 1):
                seg = cs[:, h:h + 1] - csd_t[h:h + 1, :]
                ms.append((cb * jnp.exp2(jnp.where(tril, seg, -jnp.inf))).astype(BF16))
            x_pair = conv_ref[:, gx + pair * LANES:gx + (pair + 1) * LANES]
            x_blk = jnp.concatenate(
                [jnp.where(lo_half, x_pair, 0.0), jnp.where(lo_half, 0.0, x_pair)], axis=0).astype(BF16)
            sl = slice(gx + pair * LANES, gx + (pair + 1) * LANES)
            y_diag = _dot(jnp.concatenate(ms, axis=1), x_blk)
            yield
            yacc_ref[:, sl] = yacc_ref[:, sl] + y_diag

        xd = (xs_g * wst_x[:, gx:gx + SSD_GROUP_WIDTH]).astype(BF16)
        contrib = _dot(bm.T.astype(BF16), xd)
        yield
        state_ref[:, gx:gx + SSD_GROUP_WIDTH] = (
            state_ref[:, gx:gx + SSD_GROUP_WIDTH] * ecs_x[L - 1:L, gx:gx + SSD_GROUP_WIDTH] + contrib)

        z_half = z_ref[:, gx:gx + SSD_GROUP_WIDTH].astype(F32)
        gy = yacc_ref[:, gx:gx + SSD_GROUP_WIDTH] * _silu_of_twice(z_half)
        y_ref[:, gx:gx + SSD_GROUP_WIDTH] = (
            gy * _rms_scale(gy) * nw_ref[:, gx:gx + SSD_GROUP_WIDTH]).astype(y_ref.dtype)


def _ssd(z, xbc, dt_t, cw, cb, dtb, alog, dskip_x, nw, tri3, exp2, shift, bsz, seqlen):
    nc = seqlen // SSD_CHUNK
    L = SSD_CHUNK
    rb = SSD_ROWS_PER_STEP
    row = lambda w: pl.BlockSpec((rb, L, w), lambda r, c: (r, c, 0))
    dt_spec = pl.BlockSpec((rb, SSD_HEADS, L), lambda r, c: (r, 0, c))
    per_batch = lambda a: a.reshape(bsz, seqlen, a.shape[-1])
    consts = (cw, cb, dtb, alog, dskip_x, nw, tri3, exp2, shift)
    y = pl.pallas_call(
        _ssd_kernel,
        grid=(bsz // rb, nc),
        in_specs=[row(SSD_INNER), row(SSD_CONV_DIM), dt_spec] + [_const_spec(a.shape) for a in consts],
        out_specs=row(SSD_INNER),
        out_shape=jax.ShapeDtypeStruct((bsz, seqlen, SSD_INNER), BF16),
        scratch_shapes=[
            pltpu.VMEM((rb, CONV_HALO + L, SSD_CONV_DIM), BF16),
            pltpu.VMEM((rb, L, SSD_CONV_DIM), F32),
            pltpu.VMEM((rb, SSD_STATE, SSD_INNER), F32),
            pltpu.VMEM((rb, L, SSD_INNER), F32),
        ],
        compiler_params=pltpu.CompilerParams(
            dimension_semantics=("arbitrary", "arbitrary"), vmem_limit_bytes=VMEM_LIMIT),
        name="ssd",
    )(per_batch(z), per_batch(xbc), dt_t.reshape(SSD_HEADS, bsz, seqlen).transpose(1, 0, 2), *consts)
    return y.reshape(bsz * seqlen, SSD_INNER)


def _attn_kernel(qt_ref, k_ref, vt_ref, lq1_ref, lk1_ref, lq2_ref, lk2_ref, sw_ref, o_ref,
                 qs_ref, acc_ref, m_ref, mx_ref, s_ref):
    i = pl.program_id(2)
    d_idx = lax.broadcasted_iota(jnp.int32, (DA_V_DIM, TQ), 0)
    ones_rows = jnp.ones((ONES_ROWS, TK), BF16)

    for g in range(HEADS_PER_STEP):
        rows = slice(g * DA_V_DIM, (g + 1) * DA_V_DIM)
        q = qt_ref[rows, :]
        zero = jnp.zeros_like(q)
        qs_ref[rows, 0:TQ] = jnp.where(d_idx < DA_HEAD_DIM, q, zero)
        qs_ref[rows, TQ:2 * TQ] = jnp.where(d_idx < DA_HEAD_DIM, zero, q)

    def scores(g, j, diagonal):
        rows = slice(g * DA_V_DIM, (g + 1) * DA_V_DIM)
        off = pl.multiple_of(j * TK, TK)
        s = _dot(k_ref[pl.ds(off, TK), rows], qs_ref[rows, :])
        if diagonal:
            key = lax.broadcasted_iota(jnp.int32, s.shape, 0)
            qry = lax.broadcasted_iota(jnp.int32, s.shape, 1) & (TQ - 1)
            s = jnp.where(key <= qry, s, -jnp.inf)
        s_ref[g] = s
        mx_ref[g:g + 1, :] = jnp.max(s, axis=0, keepdims=True)

    def softmax_pv(g, j, first):
        rows = slice(g * DA_V_DIM, (g + 1) * DA_V_DIM)
        arow = slice(g * ACC_ROWS, (g + 1) * ACC_ROWS)
        off = pl.multiple_of(j * TK, TK)
        if first:
            m_new = mx_ref[g:g + 1, :]
        else:
            m_prev = m_ref[g:g + 1, :]
            m_new = jnp.maximum(m_prev, mx_ref[g:g + 1, :])
        p = jnp.exp2(s_ref[g] - m_new).astype(BF16)
        v_ext = jnp.concatenate([vt_ref[rows, pl.ds(off, TK)], ones_rows], axis=0)
        pv = _dot(v_ext, p)
        if first:
            acc_ref[arow, :] = pv
        else:
            acc_ref[arow, :] = jnp.exp2(m_prev - m_new) * acc_ref[arow, :] + pv
        m_ref[g:g + 1, :] = m_new

    def run_tiles(tiles, diagonal, next_tile):
        chains = [(t, g) for t in tiles for g in range(HEADS_PER_STEP)]
        for n, (t, g) in enumerate(chains):
            ahead = n + SCORE_LOOKAHEAD
            if ahead < len(chains):
                scores(chains[ahead][1], chains[ahead][0], diagonal)
            else:
                scores(ahead - len(chains), next_tile, False)
            softmax_pv(g, t, first=diagonal)

    for g in range(SCORE_LOOKAHEAD):
        scores(g, i, True)
    run_tiles([i], diagonal=True, next_tile=0)

    def body(jj, carry):
        run_tiles([2 * jj, 2 * jj + 1], diagonal=False, next_tile=2 * jj + 2)
        return carry

    lax.fori_loop(0, i // 2, body, 0)

    @pl.when(i % 2 == 1)
    def _():
        run_tiles([i - 1], diagonal=False, next_tile=i)

    lam = (jnp.exp(jnp.sum(lq1_ref[...] * lk1_ref[...], axis=1, keepdims=True))
           - jnp.exp(jnp.sum(lq2_ref[...] * lk2_ref[...], axis=1, keepdims=True)) + LAMBDA_INIT)
    out_gain = sw_ref[...] * (1.0 - LAMBDA_INIT)
    for g in range(HEADS_PER_STEP):
        a0 = g * ACC_ROWS
        inv = 1.0 / acc_ref[a0 + DA_V_DIM:a0 + DA_V_DIM + 1, :]
        o = (acc_ref[a0:a0 + DA_V_DIM, 0:TQ] * inv[:, 0:TQ]
             - acc_ref[a0:a0 + DA_V_DIM, TQ:2 * TQ] * (lam * inv[:, TQ:2 * TQ]))
        on = o * lax.rsqrt(jnp.mean(o * o, axis=0, keepdims=True) + EPS)
        o_ref[:, g * DA_V_DIM:(g + 1) * DA_V_DIM] = (on.T * out_gain).astype(o_ref.dtype)


def _attention(qt, k, vt, lq1, lk1, lq2, lk2, sw, bsz, seqlen):
    assert TQ == TK
    nq = seqlen // TQ
    gw = HEADS_PER_STEP * DA_V_DIM
    small = (lq1, lk1, lq2, lk2, sw)
    return pl.pallas_call(
        _attn_kernel,
        grid=(bsz, DA_HEADS // HEADS_PER_STEP, nq),
        in_specs=[
            pl.BlockSpec((gw, TQ), lambda b, h, i: (h, b * nq + i)),
            pl.BlockSpec((seqlen, gw), lambda b, h, i: (b, h)),
            pl.BlockSpec((gw, seqlen), lambda b, h, i: (h, b)),
        ] + [_const_spec(a.shape) for a in small],
        out_specs=pl.BlockSpec((TQ, gw), lambda b, h, i: (b * nq + i, h)),
        out_shape=jax.ShapeDtypeStruct((bsz * seqlen, DA_WIDTH), BF16),
        scratch_shapes=[
            pltpu.VMEM((gw, 2 * TQ), BF16),
            pltpu.VMEM((HEADS_PER_STEP * ACC_ROWS, 2 * TQ), F32),
            pltpu.VMEM((HEADS_PER_STEP, 2 * TQ), F32),
            pltpu.VMEM((HEADS_PER_STEP, 2 * TQ), F32),
            pltpu.VMEM((HEADS_PER_STEP, TK, 2 * TQ), F32),
        ],
        compiler_params=pltpu.CompilerParams(
            dimension_semantics=("arbitrary", "arbitrary", "arbitrary"), vmem_limit_bytes=VMEM_LIMIT),
        name="diff_attn",
    )(qt, k, vt, *small)


def _mlp_kernel(x_ref, ys_ref, ya_ref, wo_ref, nw_ref, wg_ref, wu_ref, wd_ref, fw_ref, o_ref, h_ref):
    h_ref[...] = x_ref[...] + _dot(ys_ref[...], wo_ref[0:SSD_INNER, :]) + _dot(ya_ref[...], wo_ref[SSD_INNER:, :])
    h = h_ref[...]
    n2 = (h * _rms_scale(h) * nw_ref[...]).astype(BF16)
    ffn = None
    for f0, f1 in zip(FF_SPLITS[:-1], FF_SPLITS[1:]):
        gate = _dot(n2, wg_ref[:, f0:f1])
        up = _dot(n2, wu_ref[:, f0:f1])
        act = (_silu(gate) * up).astype(BF16)
        down = _dot(act, wd_ref[f0:f1, :])
        ffn = down if ffn is None else ffn + down
    out = h_ref[...] + ffn
    o_ref[...] = out * _rms_scale(out) * fw_ref[...]


def _mlp(x2, ys, ya, wo, nw, wg, wu, wd, fw):
    t = x2.shape[0]
    tm = TM_PROJ
    row = pl.BlockSpec((tm, D_MODEL), lambda i: (i, 0))
    consts = (wo, nw, wg, wu, wd, fw)
    return pl.pallas_call(
        _mlp_kernel,
        grid=(t // tm,),
        in_specs=[row, row, row] + [_const_spec(a.shape) for a in consts],
        out_specs=row,
        out_shape=jax.ShapeDtypeStruct((t, D_MODEL), F32),
        scratch_shapes=[pltpu.VMEM((tm, D_MODEL), F32)],
        compiler_params=pltpu.CompilerParams(
            dimension_semantics=("arbitrary",), vmem_limit_bytes=VMEM_LIMIT),
        name="mlp",
    )(x2, ys, ya, *consts)


def kernel(x, mix_norm_w, w_in, conv_w, conv_b, dt_bias, a_log, d_skip, ssd_norm_w, lam_q1, lam_k1, lam_q2,
           lam_k2, subln_w, w_out, ffn_norm_w, w_gate, w_up, w_down, final_norm_w):
    bsz, seqlen, _ = x.shape
    x2 = x.reshape(bsz * seqlen, D_MODEL)

    z, xbc, dt_t, k, qt, vt = _in_proj(x2, mix_norm_w[0][None, :], w_in[0].T.astype(BF16))

    idx = jnp.arange(SSD_CHUNK)
    tri = (idx[:, None] >= idx[None, :]).astype(BF16)
    tri3 = jnp.concatenate([tri.T, tri.T, tri.T], axis=0)
    sel = (jnp.arange(LANES)[:, None] == (jnp.arange(SSD_INNER)[None, :] // SSD_HEAD_DIM)).astype(BF16)
    exp2 = jnp.concatenate([sel, sel], axis=0)
    src = CONV_HALO + idx[None, :, None] - jnp.arange(1, SSD_CONV)[:, None, None]
    shift = (jnp.arange(CONV_HALO + SSD_CHUNK)[None, None, :] == src).astype(BF16)
    shift = shift.reshape((SSD_CONV - 1) * SSD_CHUNK, CONV_HALO + SSD_CHUNK)
    dskip_x = jnp.repeat(d_skip[0].astype(F32), SSD_HEAD_DIM)[None, :]

    per_head = lambda v: jnp.broadcast_to(v.astype(F32)[:, None], (SSD_HEADS, SSD_CHUNK))
    y_ssd = _ssd(z, xbc, dt_t, conv_w[0], conv_b[0][None, :], per_head(dt_bias[0]), per_head(a_log[0]),
                 dskip_x, ssd_norm_w[0][None, :], tri3, exp2, shift, bsz, seqlen)
    y_da = _attention(qt, k, vt, lam_q1[0][None, :], lam_k1[0][None, :], lam_q2[0][None, :],
                      lam_k2[0][None, :], subln_w[0][None, :], bsz, seqlen)

    out = _mlp(x2, y_ssd, y_da, w_out[0].astype(BF16), ffn_norm_w[0][None, :], w_gate[0].astype(BF16),
               w_up[0].astype(BF16), w_down[0].astype(BF16), final_norm_w[None, :])
    return out.reshape(bsz, seqlen, D_MODEL)
```

```python
import math

import jax
import jax.numpy as jnp
from jax import lax
from jax.experimental import pallas as pl
from jax.experimental.pallas import tpu as pltpu

F32 = jnp.float32
BF16 = jnp.bfloat16

EPS = 1e-5
D_MODEL = 1024
SSD_HEADS = 16
SSD_HEAD_DIM = 64
SSD_INNER = SSD_HEADS * SSD_HEAD_DIM
SSD_GROUPS = 2
SSD_GROUP_WIDTH = SSD_INNER // SSD_GROUPS
SSD_STATE = 128
SSD_CONV = 4
SSD_CHUNK = 128
SSD_BC = SSD_GROUPS * SSD_STATE
SSD_CONV_DIM = SSD_INNER + 2 * SSD_BC
DA_HEADS = 8
DA_HEAD_DIM = 64
DA_V_DIM = 2 * DA_HEAD_DIM
DA_WIDTH = DA_HEADS * DA_V_DIM
D_FF = 2816
LAMBDA_INIT = 0.8 - 0.6 * math.exp(-0.3 * 0)

LANES = 128
CONV_HALO = 16
VMEM_LIMIT = 56 * 1024 * 1024

TM_PROJ = 512
TM_IN_PROJ = 1024
TQ = 256
TK = 256
MXU_WIDTH = 256
FF_SPLITS = (0, 6 * MXU_WIDTH, D_FF)
HEADS_PER_STEP = 8
SSD_ROWS_PER_STEP = 4
SCORE_LOOKAHEAD = 2
ONES_ROWS = 16
ACC_ROWS = DA_V_DIM + ONES_ROWS
LOG2E = math.log2(math.e)


def _const_spec(shape):
    nd = len(shape)
    return pl.BlockSpec(shape, lambda *_: (0,) * nd, pipeline_mode=pl.Buffered(1))


def _rms_scale(xf):
    return lax.rsqrt(jnp.mean(xf * xf, axis=-1, keepdims=True) + EPS)


def _dot(a, b):
    return jnp.dot(a, b, preferred_element_type=F32)


def _dot_nt(a, b):
    return lax.dot_general(a, b, (((1,), (1,)), ((), ())), preferred_element_type=F32)


def _split3(a):
    hi = a.astype(BF16)
    r1 = a - hi.astype(F32)
    mid = r1.astype(BF16)
    lo = (r1 - mid.astype(F32)).astype(BF16)
    return hi, mid, lo


def _silu_of_twice(h):
    return h + h * jnp.tanh(h)


def _silu(x):
    return _silu_of_twice(0.5 * x)


def _in_proj_kernel(x_ref, nw_ref, wt_ref, *refs):
    n_cast = (len(refs) - 6) // 2
    cast_in, (z_ref, xbc_ref, dt_ref, k_ref, qt_ref, vt_ref), cast_out = (
        refs[:n_cast], refs[n_cast:n_cast + 6], refs[n_cast + 6:])
    for src, dst in zip(cast_in, cast_out):
        dst[...] = src[...].astype(BF16)
    xf = x_ref[...]
    xn = (xf * _rms_scale(xf) * nw_ref[...]).astype(BF16)
    r0 = 0
    for ref, width, scale, transposed in (
            (z_ref, SSD_INNER, 0.5, False), (xbc_ref, SSD_CONV_DIM, None, False), (dt_ref, SSD_HEADS, None, True),
            (qt_ref, DA_WIDTH, DA_HEAD_DIM ** -0.5 * LOG2E, True), (k_ref, DA_WIDTH, None, False),
            (vt_ref, DA_WIDTH, None, True)):
        w = wt_ref[r0:r0 + width, :]
        r = _dot_nt(w, xn) if transposed else _dot_nt(xn, w)
        ref[...] = (r if scale is None else r * scale).astype(ref.dtype)
        r0 += width


def _in_proj(x2, nw, wt, to_cast):
    t = x2.shape[0]
    tm = TM_IN_PROJ
    steps = t // tm
    row = lambda w: pl.BlockSpec((tm, w), lambda i: (i, 0))
    col = pl.BlockSpec((DA_WIDTH, tm), lambda i: (0, i))
    cast_specs = [pl.BlockSpec((a.shape[0] // steps, a.shape[1]), lambda i: (i, 0)) for a in to_cast]
    outs = pl.pallas_call(
        _in_proj_kernel,
        grid=(steps,),
        in_specs=[row(D_MODEL), _const_spec(nw.shape), _const_spec(wt.shape)] + cast_specs,
        out_specs=[row(SSD_INNER), row(SSD_CONV_DIM), pl.BlockSpec((SSD_HEADS, tm), lambda i: (0, i)),
                   row(DA_WIDTH), col, col] + cast_specs,
        out_shape=[
            jax.ShapeDtypeStruct((t, SSD_INNER), BF16),
            jax.ShapeDtypeStruct((t, SSD_CONV_DIM), BF16),
            jax.ShapeDtypeStruct((SSD_HEADS, t), F32),
            jax.ShapeDtypeStruct((t, DA_WIDTH), BF16),
            jax.ShapeDtypeStruct((DA_WIDTH, t), BF16),
            jax.ShapeDtypeStruct((DA_WIDTH, t), BF16),
        ] + [jax.ShapeDtypeStruct(a.shape, BF16) for a in to_cast],
        compiler_params=pltpu.CompilerParams(
            dimension_semantics=("arbitrary",), vmem_limit_bytes=VMEM_LIMIT),
        name="in_proj",
    )(x2, nw, wt, *to_cast)
    return outs[:6], outs[6:]


def _ssd_kernel(z_ref, xbc_ref, dt_ref, cw_ref, cb_ref, dtb_ref, alog_ref, dskip_ref, nw_ref,
                tri3_ref, exp2_ref, shift_ref, y_ref, ext_ref, conv_ref, state_ref, yacc_ref):
    L = SSD_CHUNK
    c = pl.program_id(1)

    @pl.when(c == 0)
    def _():
        state_ref[...] = jnp.zeros_like(state_ref)
        ext_ref[:, 0:CONV_HALO, :] = jnp.zeros((ext_ref.shape[0], CONV_HALO, SSD_CONV_DIM), BF16)

    @pl.when(c > 0)
    def _():
        ext_ref[:, 0:CONV_HALO, :] = ext_ref[:, L:L + CONV_HALO, :]

    consts = (cw_ref, cb_ref, dtb_ref, alog_ref, dskip_ref, nw_ref, tri3_ref, exp2_ref, shift_ref)
    _round_robin([
        _ssd_chunk(z_ref.at[b], xbc_ref.at[b], dt_ref.at[b], *consts, y_ref.at[b], ext_ref.at[b],
                   conv_ref.at[b], state_ref.at[b], yacc_ref.at[b])
        for b in range(z_ref.shape[0])])


def _round_robin(stages):
    stages = list(stages)
    while stages:
        for gen in list(stages):
            if next(gen, StopIteration) is StopIteration:
                stages.remove(gen)


def _ssd_chunk(z_ref, xbc_ref, dt_ref, cw_ref, cb_ref, dtb_ref, alog_ref, dskip_ref, nw_ref,
               tri3_ref, exp2_ref, shift_ref, y_ref, ext_ref, conv_ref, state_ref, yacc_ref):
    L = SSD_CHUNK
    ext_ref[CONV_HALO:CONV_HALO + L, :] = xbc_ref[...]

    for c0 in range(0, SSD_CONV_DIM, 512):
        e = ext_ref[:, c0:c0 + 512]
        cwh = 0.5 * cw_ref[:, c0:c0 + 512]
        acc = 0.5 * cb_ref[:, c0:c0 + 512] + cwh[SSD_CONV - 1:SSD_CONV, :] * e[CONV_HALO:, :].astype(F32)
        shifted = _dot(shift_ref[...], e)
        for back in range(1, SSD_CONV):
            j = SSD_CONV - 1 - back
            acc = acc + cwh[j:j + 1, :] * shifted[(back - 1) * L:back * L, :]
        conv_ref[:, c0:c0 + 512] = _silu_of_twice(acc)

    dtr = dt_ref[...] + dtb_ref[...]
    dtv = jnp.maximum(dtr, 0.0) + jnp.log(1.0 + jnp.exp(-jnp.abs(dtr)))
    adt = dtv * (-jnp.exp(alog_ref[...]))
    cs_t = _dot(jnp.concatenate(_split3(adt), axis=1), tri3_ref[...]) * LOG2E
    yield
    csd_t = cs_t - jnp.log2(dtv)
    w_state_t = dtv * jnp.exp2(cs_t[:, L - 1:L] - cs_t)

    def by_time(a_t):
        return jnp.concatenate([a_t, jnp.zeros((LANES - SSD_HEADS, L), F32)], axis=0).T

    cs = by_time(cs_t)
    ecs = by_time(jnp.exp2(cs_t))
    w_state = by_time(w_state_t)

    def expand(a):
        hi = a.astype(BF16)
        lo = (a - hi.astype(F32)).astype(BF16)
        return _dot(jnp.concatenate([hi, lo], axis=1), exp2_ref[...])

    ecs_x = expand(ecs)
    wst_x = expand(w_state)
    yield

    row = lax.broadcasted_iota(jnp.int32, (L, L), 0)
    colm = lax.broadcasted_iota(jnp.int32, (L, L), 1)
    tril = row >= colm
    lane = lax.broadcasted_iota(jnp.int32, (L, LANES), 1)
    lo_half = lane < SSD_HEAD_DIM

    for g in range(SSD_GROUPS):
        gx = g * SSD_GROUP_WIDTH
        bm = conv_ref[:, SSD_INNER + g * SSD_STATE:SSD_INNER + (g + 1) * SSD_STATE]
        cm = conv_ref[:, SSD_INNER + SSD_BC + g * SSD_STATE:SSD_INNER + SSD_BC + (g + 1) * SSD_STATE]
        cm16 = cm.astype(BF16)
        cb = _dot_nt(cm16, bm.astype(BF16))
        xs_g = conv_ref[:, gx:gx + SSD_GROUP_WIDTH]

        y_off = _dot(cm16, state_ref[:, gx:gx + SSD_GROUP_WIDTH].astype(BF16))
        yield
        yacc_ref[:, gx:gx + SSD_GROUP_WIDTH] = (
            y_off * ecs_x[:, gx:gx + SSD_GROUP_WIDTH] + xs_g * dskip_ref[:, gx:gx + SSD_GROUP_WIDTH])

        for pair in range(SSD_HEADS // SSD_GROUPS // 2):
            ms = []
            for h in (g * 8 + 2 * pair, g * 8 + 2 * pair + 1):
                seg = cs[:, h:h + 1] - csd_t[h:h + 1, :]
                ms.append((cb * jnp.exp2(jnp.where(tril, seg, -jnp.inf))).astype(BF16))
            x_pair = conv_ref[:, gx + pair * LANES:gx + (pair + 1) * LANES]
            x_blk = jnp.concatenate(
                [jnp.where(lo_half, x_pair, 0.0), jnp.where(lo_half, 0.0, x_pair)], axis=0).astype(BF16)
            sl = slice(gx + pair * LANES, gx + (pair + 1) * LANES)
            y_diag = _dot(jnp.concatenate(ms, axis=1), x_blk)
            yield
            yacc_ref[:, sl] = yacc_ref[:, sl] + y_diag

        xd = (xs_g * wst_x[:, gx:gx + SSD_GROUP_WIDTH]).astype(BF16)
        contrib = _dot(bm.T.astype(BF16), xd)
        yield
        state_ref[:, gx:gx + SSD_GROUP_WIDTH] = (
            state_ref[:, gx:gx + SSD_GROUP_WIDTH] * ecs_x[L - 1:L, gx:gx + SSD_GROUP_WIDTH] + contrib)

        z_half = z_ref[:, gx:gx + SSD_GROUP_WIDTH].astype(F32)
        gy = yacc_ref[:, gx:gx + SSD_GROUP_WIDTH] * _silu_of_twice(z_half)
        y_ref[:, gx:gx + SSD_GROUP_WIDTH] = (
            gy * _rms_scale(gy) * nw_ref[:, gx:gx + SSD_GROUP_WIDTH]).astype(y_ref.dtype)


def _ssd(z, xbc, dt_t, cw, cb, dtb, alog, dskip_x, nw, tri3, exp2, shift, bsz, seqlen):
    nc = seqlen // SSD_CHUNK
    L = SSD_CHUNK
    rb = SSD_ROWS_PER_STEP
    row = lambda w: pl.BlockSpec((rb, L, w), lambda r, c: (r, c, 0))
    dt_spec = pl.BlockSpec((rb, SSD_HEADS, L), lambda r, c: (r, 0, c))
    per_batch = lambda a: a.reshape(bsz, seqlen, a.shape[-1])
    consts = (cw, cb, dtb, alog, dskip_x, nw, tri3, exp2, shift)
    y = pl.pallas_call(
        _ssd_kernel,
        grid=(bsz // rb, nc),
        in_specs=[row(SSD_INNER), row(SSD_CONV_DIM), dt_spec] + [_const_spec(a.shape) for a in consts],
        out_specs=row(SSD_INNER),
        out_shape=jax.ShapeDtypeStruct((bsz, seqlen, SSD_INNER), BF16),
        scratch_shapes=[
            pltpu.VMEM((rb, CONV_HALO + L, SSD_CONV_DIM), BF16),
            pltpu.VMEM((rb, L, SSD_CONV_DIM), F32),
            pltpu.VMEM((rb, SSD_STATE, SSD_INNER), F32),
            pltpu.VMEM((rb, L, SSD_INNER), F32),
        ],
        compiler_params=pltpu.CompilerParams(
            dimension_semantics=("arbitrary", "arbitrary"), vmem_limit_bytes=VMEM_LIMIT),
        name="ssd",
    )(per_batch(z), per_batch(xbc), dt_t.reshape(SSD_HEADS, bsz, seqlen).transpose(1, 0, 2), *consts)
    return y.reshape(bsz * seqlen, SSD_INNER)


def _attn_kernel(qt_ref, k_ref, vt_ref, lq1_ref, lk1_ref, lq2_ref, lk2_ref, o_ref,
                 qs_ref, acc_ref, m_ref, mx_ref, s_ref):
    i = pl.program_id(2)
    d_idx = lax.broadcasted_iota(jnp.int32, (DA_V_DIM, TQ), 0)
    ones_rows = jnp.ones((ONES_ROWS, TK), BF16)

    for g in range(HEADS_PER_STEP):
        rows = slice(g * DA_V_DIM, (g + 1) * DA_V_DIM)
        q = qt_ref[rows, :]
        zero = jnp.zeros_like(q)
        qs_ref[rows, 0:TQ] = jnp.where(d_idx < DA_HEAD_DIM, q, zero)
        qs_ref[rows, TQ:2 * TQ] = jnp.where(d_idx < DA_HEAD_DIM, zero, q)

    def scores(g, j, diagonal):
        rows = slice(g * DA_V_DIM, (g + 1) * DA_V_DIM)
        off = pl.multiple_of(j * TK, TK)
        s = _dot(k_ref[pl.ds(off, TK), rows], qs_ref[rows, :])
        if diagonal:
            key = lax.broadcasted_iota(jnp.int32, s.shape, 0)
            qry = lax.broadcasted_iota(jnp.int32, s.shape, 1) & (TQ - 1)
            s = jnp.where(key <= qry, s, -jnp.inf)
        s_ref[g] = s
        mx_ref[g:g + 1, :] = jnp.max(s, axis=0, keepdims=True)

    def softmax_pv(g, j, first):
        rows = slice(g * DA_V_DIM, (g + 1) * DA_V_DIM)
        arow = slice(g * ACC_ROWS, (g + 1) * ACC_ROWS)
        off = pl.multiple_of(j * TK, TK)
        if first:
            m_new = mx_ref[g:g + 1, :]
        else:
            m_prev = m_ref[g:g + 1, :]
            m_new = jnp.maximum(m_prev, mx_ref[g:g + 1, :])
        p = jnp.exp2(s_ref[g] - m_new).astype(BF16)
        v_ext = jnp.concatenate([vt_ref[rows, pl.ds(off, TK)], ones_rows], axis=0)
        pv = _dot(v_ext, p)
        if first:
            acc_ref[arow, :] = pv
        else:
            acc_ref[arow, :] = jnp.exp2(m_prev - m_new) * acc_ref[arow, :] + pv
        m_ref[g:g + 1, :] = m_new

    def run_tiles(tiles, diagonal, next_tile):
        chains = [(t, g) for t in tiles for g in range(HEADS_PER_STEP)]
        for n, (t, g) in enumerate(chains):
            ahead = n + SCORE_LOOKAHEAD
            if ahead < len(chains):
                scores(chains[ahead][1], chains[ahead][0], diagonal)
            else:
                scores(ahead - len(chains), next_tile, False)
            softmax_pv(g, t, first=diagonal)

    for g in range(SCORE_LOOKAHEAD):
        scores(g, i, True)
    run_tiles([i], diagonal=True, next_tile=0)

    def body(jj, carry):
        run_tiles([2 * jj, 2 * jj + 1], diagonal=False, next_tile=2 * jj + 2)
        return carry

    lax.fori_loop(0, i // 2, body, 0)

    @pl.when(i % 2 == 1)
    def _():
        run_tiles([i - 1], diagonal=False, next_tile=i)

    lam = (jnp.exp(jnp.sum(lq1_ref[...] * lk1_ref[...], axis=1, keepdims=True))
           - jnp.exp(jnp.sum(lq2_ref[...] * lk2_ref[...], axis=1, keepdims=True)) + LAMBDA_INIT)
    for g in range(HEADS_PER_STEP):
        a0 = g * ACC_ROWS
        inv = 1.0 / acc_ref[a0 + DA_V_DIM:a0 + DA_V_DIM + 1, :]
        o = (acc_ref[a0:a0 + DA_V_DIM, 0:TQ] * inv[:, 0:TQ]
             - acc_ref[a0:a0 + DA_V_DIM, TQ:2 * TQ] * (lam * inv[:, TQ:2 * TQ]))
        o_ref[:, g * DA_V_DIM:(g + 1) * DA_V_DIM] = o.T.astype(o_ref.dtype)


def _attention(qt, k, vt, lq1, lk1, lq2, lk2, bsz, seqlen):
    assert TQ == TK
    nq = seqlen // TQ
    gw = HEADS_PER_STEP * DA_V_DIM
    small = (lq1, lk1, lq2, lk2)
    return pl.pallas_call(
        _attn_kernel,
        grid=(bsz, DA_HEADS // HEADS_PER_STEP, nq),
        in_specs=[
            pl.BlockSpec((gw, TQ), lambda b, h, i: (h, b * nq + i)),
            pl.BlockSpec((seqlen, gw), lambda b, h, i: (b, h)),
            pl.BlockSpec((gw, seqlen), lambda b, h, i: (h, b)),
        ] + [_const_spec(a.shape) for a in small],
        out_specs=pl.BlockSpec((TQ, gw), lambda b, h, i: (b * nq + i, h)),
        out_shape=jax.ShapeDtypeStruct((bsz * seqlen, DA_WIDTH), BF16),
        scratch_shapes=[
            pltpu.VMEM((gw, 2 * TQ), BF16),
            pltpu.VMEM((HEADS_PER_STEP * ACC_ROWS, 2 * TQ), F32),
            pltpu.VMEM((HEADS_PER_STEP, 2 * TQ), F32),
            pltpu.VMEM((HEADS_PER_STEP, 2 * TQ), F32),
            pltpu.VMEM((HEADS_PER_STEP, TK, 2 * TQ), F32),
        ],
        compiler_params=pltpu.CompilerParams(
            dimension_semantics=("arbitrary", "arbitrary", "arbitrary"), vmem_limit_bytes=VMEM_LIMIT),
        name="diff_attn",
    )(qt, k, vt, *small)


def _mlp_kernel(x_ref, ys_ref, ya_ref, sw_ref, wo_ref, nw_ref, wg_ref, wu_ref, wd_ref, fw_ref, o_ref, h_ref):
    gain = sw_ref[...] * (1.0 - LAMBDA_INIT)
    heads = []
    for g in range(DA_HEADS):
        o = ya_ref[:, g * DA_V_DIM:(g + 1) * DA_V_DIM].astype(F32)
        heads.append((o * _rms_scale(o) * gain).astype(BF16))
    ya = jnp.concatenate(heads, axis=1)
    h_ref[...] = x_ref[...] + _dot(ys_ref[...], wo_ref[0:SSD_INNER, :]) + _dot(ya, wo_ref[SSD_INNER:, :])
    h = h_ref[...]
    n2 = (h * _rms_scale(h) * nw_ref[...]).astype(BF16)
    ffn = None
    for f0, f1 in zip(FF_SPLITS[:-1], FF_SPLITS[1:]):
        gate = _dot(n2, wg_ref[:, f0:f1])
        up = _dot(n2, wu_ref[:, f0:f1])
        act = (_silu(gate) * up).astype(BF16)
        down = _dot(act, wd_ref[f0:f1, :])
        ffn = down if ffn is None else ffn + down
    out = h_ref[...] + ffn
    o_ref[...] = out * _rms_scale(out) * fw_ref[...]


def _mlp(x2, ys, ya, sw, wo, nw, wg, wu, wd, fw):
    t = x2.shape[0]
    tm = TM_PROJ
    row = pl.BlockSpec((tm, D_MODEL), lambda i: (i, 0))
    consts = (sw, wo, nw, wg, wu, wd, fw)
    return pl.pallas_call(
        _mlp_kernel,
        grid=(t // tm,),
        in_specs=[row, row, row] + [_const_spec(a.shape) for a in consts],
        out_specs=row,
        out_shape=jax.ShapeDtypeStruct((t, D_MODEL), F32),
        scratch_shapes=[pltpu.VMEM((tm, D_MODEL), F32)],
        compiler_params=pltpu.CompilerParams(
            dimension_semantics=("arbitrary",), vmem_limit_bytes=VMEM_LIMIT),
        name="mlp",
    )(x2, ys, ya, *consts)


def kernel(x, mix_norm_w, w_in, conv_w, conv_b, dt_bias, a_log, d_skip, ssd_norm_w, lam_q1, lam_k1, lam_q2,
           lam_k2, subln_w, w_out, ffn_norm_w, w_gate, w_up, w_down, final_norm_w):
    bsz, seqlen, _ = x.shape
    x2 = x.reshape(bsz * seqlen, D_MODEL)

    (z, xbc, dt_t, k, qt, vt), (wo16, wg16, wu16, wd16) = _in_proj(
        x2, mix_norm_w[0][None, :], w_in[0].T.astype(BF16), (w_out[0], w_gate[0], w_up[0], w_down[0]))

    idx = jnp.arange(SSD_CHUNK)
    tri = (idx[:, None] >= idx[None, :]).astype(BF16)
    tri3 = jnp.concatenate([tri.T, tri.T, tri.T], axis=0)
    sel = (jnp.arange(LANES)[:, None] == (jnp.arange(SSD_INNER)[None, :] // SSD_HEAD_DIM)).astype(BF16)
    exp2 = jnp.concatenate([sel, sel], axis=0)
    src = CONV_HALO + idx[None, :, None] - jnp.arange(1, SSD_CONV)[:, None, None]
    shift = (jnp.arange(CONV_HALO + SSD_CHUNK)[None, None, :] == src).astype(BF16)
    shift = shift.reshape((SSD_CONV - 1) * SSD_CHUNK, CONV_HALO + SSD_CHUNK)
    dskip_x = jnp.repeat(d_skip[0].astype(F32), SSD_HEAD_DIM)[None, :]

    per_head = lambda v: jnp.broadcast_to(v.astype(F32)[:, None], (SSD_HEADS, SSD_CHUNK))
    y_ssd = _ssd(z, xbc, dt_t, conv_w[0], conv_b[0][None, :], per_head(dt_bias[0]), per_head(a_log[0]),
                 dskip_x, ssd_norm_w[0][None, :], tri3, exp2, shift, bsz, seqlen)
    y_da = _attention(qt, k, vt, lam_q1[0][None, :], lam_k1[0][None, :], lam_q2[0][None, :],
                      lam_k2[0][None, :], bsz, seqlen)

    out = _mlp(x2, y_ssd, y_da, subln_w[0][None, :], wo16, ffn_norm_w[0][None, :], wg16, wu16, wd16,
               final_norm_w[None, :])
    return out.reshape(bsz, seqlen, D_MODEL)
```

```python
import math

import jax
import jax.numpy as jnp
from jax import lax
from jax.experimental import pallas as pl
from jax.experimental.pallas import tpu as pltpu

F32 = jnp.float32
BF16 = jnp.bfloat16

EPS = 1e-5
D_MODEL = 1024
SSD_HEADS = 16
SSD_HEAD_DIM = 64
SSD_INNER = SSD_HEADS * SSD_HEAD_DIM
SSD_GROUPS = 2
SSD_GROUP_WIDTH = SSD_INNER // SSD_GROUPS
SSD_STATE = 128
SSD_CONV = 4
SSD_CHUNK = 128
SSD_BC = SSD_GROUPS * SSD_STATE
SSD_CONV_DIM = SSD_INNER + 2 * SSD_BC
DA_HEADS = 8
DA_HEAD_DIM = 64
DA_V_DIM = 2 * DA_HEAD_DIM
DA_WIDTH = DA_HEADS * DA_V_DIM
D_FF = 2816
LAMBDA_INIT = 0.8 - 0.6 * math.exp(-0.3 * 0)

LANES = 128
CONV_HALO = 16
CONV_LANES = 512
VMEM_LIMIT = 56 * 1024 * 1024

TM_PROJ = 512
TM_IN_PROJ = 1024
TQ = 256
TK = 256
MXU_WIDTH = 256
FF_SPLITS = (0, 6 * MXU_WIDTH, D_FF)
HEADS_PER_STEP = 8
SSD_ROWS_PER_STEP = 4
SSD_CHUNKS_PER_STEP = 2
SCORE_LOOKAHEAD = 2
ONES_ROWS = 16
ACC_ROWS = DA_V_DIM + ONES_ROWS
LOG2E = math.log2(math.e)


def _const_spec(shape):
    nd = len(shape)
    return pl.BlockSpec(shape, lambda *_: (0,) * nd, pipeline_mode=pl.Buffered(1))


def _rms_scale(xf):
    return lax.rsqrt(jnp.mean(xf * xf, axis=-1, keepdims=True) + EPS)


def _dot(a, b):
    return jnp.dot(a, b, preferred_element_type=F32)


def _dot_nt(a, b):
    return lax.dot_general(a, b, (((1,), (1,)), ((), ())), preferred_element_type=F32)


def _split3(a):
    hi = a.astype(BF16)
    r1 = a - hi.astype(F32)
    mid = r1.astype(BF16)
    lo = (r1 - mid.astype(F32)).astype(BF16)
    return hi, mid, lo


def _silu_of_twice(h):
    return h + h * jnp.tanh(h)


def _silu(x):
    return _silu_of_twice(0.5 * x)


def _in_proj_kernel(x_ref, nw_ref, wt_ref, *refs):
    n_cast = (len(refs) - 6) // 2
    cast_in, (z_ref, xbc_ref, dt_ref, k_ref, qt_ref, vt_ref), cast_out = (
        refs[:n_cast], refs[n_cast:n_cast + 6], refs[n_cast + 6:])
    for src, dst in zip(cast_in, cast_out):
        dst[...] = src[...].astype(BF16)
    xf = x_ref[...]
    xn = (xf * _rms_scale(xf) * nw_ref[...]).astype(BF16)
    r0 = 0
    for ref, width, scale, transposed in (
            (z_ref, SSD_INNER, 0.5, False), (xbc_ref, SSD_CONV_DIM, None, False), (dt_ref, SSD_HEADS, None, True),
            (qt_ref, DA_WIDTH, DA_HEAD_DIM ** -0.5 * LOG2E, True), (k_ref, DA_WIDTH, None, False),
            (vt_ref, DA_WIDTH, None, True)):
        w = wt_ref[r0:r0 + width, :]
        r = _dot_nt(w, xn) if transposed else _dot_nt(xn, w)
        ref[...] = (r if scale is None else r * scale).astype(ref.dtype)
        r0 += width


def _in_proj(x2, nw, wt, to_cast):
    t = x2.shape[0]
    tm = TM_IN_PROJ
    steps = t // tm
    row = lambda w: pl.BlockSpec((tm, w), lambda i: (i, 0))
    col = pl.BlockSpec((DA_WIDTH, tm), lambda i: (0, i))
    cast_specs = [pl.BlockSpec((a.shape[0] // steps, a.shape[1]), lambda i: (i, 0)) for a in to_cast]
    outs = pl.pallas_call(
        _in_proj_kernel,
        grid=(steps,),
        in_specs=[row(D_MODEL), _const_spec(nw.shape), _const_spec(wt.shape)] + cast_specs,
        out_specs=[row(SSD_INNER), row(SSD_CONV_DIM), pl.BlockSpec((SSD_HEADS, tm), lambda i: (0, i)),
                   row(DA_WIDTH), col, col] + cast_specs,
        out_shape=[
            jax.ShapeDtypeStruct((t, SSD_INNER), BF16),
            jax.ShapeDtypeStruct((t, SSD_CONV_DIM), BF16),
            jax.ShapeDtypeStruct((SSD_HEADS, t), F32),
            jax.ShapeDtypeStruct((t, DA_WIDTH), BF16),
            jax.ShapeDtypeStruct((DA_WIDTH, t), BF16),
            jax.ShapeDtypeStruct((DA_WIDTH, t), BF16),
        ] + [jax.ShapeDtypeStruct(a.shape, BF16) for a in to_cast],
        compiler_params=pltpu.CompilerParams(
            dimension_semantics=("arbitrary",), vmem_limit_bytes=VMEM_LIMIT),
        name="in_proj",
    )(x2, nw, wt, *to_cast)
    return outs[:6], outs[6:]


def _ssd_kernel(z_ref, xbc_ref, dt_ref, cw_ref, cb_ref, dtb_ref, alog_ref, dskip_ref, nw_ref,
                tri3_ref, exp2_ref, shift_ref, y_ref, ext_ref, conv_ref, state_ref, yacc_ref):
    L = SSD_CHUNK
    c = pl.program_id(1)

    @pl.when(c == 0)
    def _():
        state_ref[...] = jnp.zeros_like(state_ref)
        ext_ref[:, 0:CONV_HALO, :] = jnp.zeros((ext_ref.shape[0], CONV_HALO, SSD_CONV_DIM), BF16)

    @pl.when(c > 0)
    def _():
        ext_ref[:, 0:CONV_HALO, :] = ext_ref[:, SSD_CHUNKS_PER_STEP * L:SSD_CHUNKS_PER_STEP * L + CONV_HALO, :]

    consts = (cw_ref, cb_ref, dtb_ref, alog_ref, dskip_ref, nw_ref, tri3_ref, exp2_ref, shift_ref)

    def row_chunks(b):
        for n in range(SSD_CHUNKS_PER_STEP):
            t = pl.ds(n * L, L)
            yield from _ssd_chunk(z_ref.at[b, t], xbc_ref.at[b, t], dt_ref.at[b, :, t], *consts, y_ref.at[b, t],
                                  ext_ref.at[b, pl.ds(n * L, CONV_HALO + L)], conv_ref.at[b], state_ref.at[b],
                                  yacc_ref.at[b])

    _round_robin([row_chunks(b) for b in range(z_ref.shape[0])])


def _round_robin(stages):
    stages = list(stages)
    while stages:
        for gen in list(stages):
            if next(gen, StopIteration) is StopIteration:
                stages.remove(gen)


def _ssd_chunk(z_ref, xbc_ref, dt_ref, cw_ref, cb_ref, dtb_ref, alog_ref, dskip_ref, nw_ref,
               tri3_ref, exp2_ref, shift_ref, y_ref, ext_ref, conv_ref, state_ref, yacc_ref):
    L = SSD_CHUNK
    ext_ref[CONV_HALO:CONV_HALO + L, :] = xbc_ref[...]

    for c0 in range(0, SSD_CONV_DIM, CONV_LANES):
        cols = slice(c0, c0 + CONV_LANES)
        e = ext_ref[:, cols]
        cwh = 0.5 * cw_ref[:, cols]
        acc = 0.5 * cb_ref[:, cols] + cwh[SSD_CONV - 1:SSD_CONV, :] * e[CONV_HALO:, :].astype(F32)
        shifted = _dot(shift_ref[...], e)
        for back in range(1, SSD_CONV):
            j = SSD_CONV - 1 - back
            acc = acc + cwh[j:j + 1, :] * shifted[(back - 1) * L:back * L, :]
        conv_ref[:, cols] = _silu_of_twice(acc)

    dtr = dt_ref[...] + dtb_ref[...]
    dtv = jnp.maximum(dtr, 0.0) + jnp.log(1.0 + jnp.exp(-jnp.abs(dtr)))
    adt = dtv * (-jnp.exp(alog_ref[...]))
    cs_t = _dot(jnp.concatenate(_split3(adt), axis=1), tri3_ref[...]) * LOG2E
    yield
    csd_t = cs_t - jnp.log2(dtv)
    w_state_t = dtv * jnp.exp2(cs_t[:, L - 1:L] - cs_t)

    def by_time(a_t):
        return jnp.concatenate([a_t, jnp.zeros((LANES - SSD_HEADS, L), F32)], axis=0).T

    cs = by_time(cs_t)
    ecs = by_time(jnp.exp2(cs_t))
    w_state = by_time(w_state_t)

    def expand(a):
        hi = a.astype(BF16)
        lo = (a - hi.astype(F32)).astype(BF16)
        return _dot(jnp.concatenate([hi, lo], axis=1), exp2_ref[...])

    ecs_x = expand(ecs)
    wst_x = expand(w_state)
    yield

    row = lax.broadcasted_iota(jnp.int32, (L, L), 0)
    colm = lax.broadcasted_iota(jnp.int32, (L, L), 1)
    tril = row >= colm
    lane = lax.broadcasted_iota(jnp.int32, (L, LANES), 1)
    lo_half = lane < SSD_HEAD_DIM

    for g in range(SSD_GROUPS):
        gx = g * SSD_GROUP_WIDTH
        bm = conv_ref[:, SSD_INNER + g * SSD_STATE:SSD_INNER + (g + 1) * SSD_STATE]
        cm = conv_ref[:, SSD_INNER + SSD_BC + g * SSD_STATE:SSD_INNER + SSD_BC + (g + 1) * SSD_STATE]
        cm16 = cm.astype(BF16)
        cb = _dot_nt(cm16, bm.astype(BF16))
        xs_g = conv_ref[:, gx:gx + SSD_GROUP_WIDTH]

        y_off = _dot(cm16, state_ref[:, gx:gx + SSD_GROUP_WIDTH].astype(BF16))
        yield
        yacc_ref[:, gx:gx + SSD_GROUP_WIDTH] = (
            y_off * ecs_x[:, gx:gx + SSD_GROUP_WIDTH] + xs_g * dskip_ref[:, gx:gx + SSD_GROUP_WIDTH])

        for pair in range(SSD_HEADS // SSD_GROUPS // 2):
            ms = []
            h0 = g * (SSD_HEADS // SSD_GROUPS) + 2 * pair
            for h in (h0, h0 + 1):
                seg = cs[:, h:h + 1] - csd_t[h:h + 1, :]
                ms.append((cb * jnp.exp2(jnp.where(tril, seg, -jnp.inf))).astype(BF16))
            x_pair = conv_ref[:, gx + pair * LANES:gx + (pair + 1) * LANES]
            x_blk = jnp.concatenate(
                [jnp.where(lo_half, x_pair, 0.0), jnp.where(lo_half, 0.0, x_pair)], axis=0).astype(BF16)
            sl = slice(gx + pair * LANES, gx + (pair + 1) * LANES)
            y_diag = _dot(jnp.concatenate(ms, axis=1), x_blk)
            yield
            yacc_ref[:, sl] = yacc_ref[:, sl] + y_diag

        xd = (xs_g * wst_x[:, gx:gx + SSD_GROUP_WIDTH]).astype(BF16)
        contrib = _dot(bm.T.astype(BF16), xd)
        yield
        state_ref[:, gx:gx + SSD_GROUP_WIDTH] = (
            state_ref[:, gx:gx + SSD_GROUP_WIDTH] * ecs_x[L - 1:L, gx:gx + SSD_GROUP_WIDTH] + contrib)

        z_half = z_ref[:, gx:gx + SSD_GROUP_WIDTH].astype(F32)
        gy = yacc_ref[:, gx:gx + SSD_GROUP_WIDTH] * _silu_of_twice(z_half)
        y_ref[:, gx:gx + SSD_GROUP_WIDTH] = (
            gy * _rms_scale(gy) * nw_ref[:, gx:gx + SSD_GROUP_WIDTH]).astype(y_ref.dtype)


def _ssd(z, xbc, dt_t, cw, cb, dtb, alog, dskip_x, nw, tri3, exp2, shift, bsz, seqlen):
    L = SSD_CHUNK * SSD_CHUNKS_PER_STEP
    nc = seqlen // L
    rb = SSD_ROWS_PER_STEP
    row = lambda w: pl.BlockSpec((rb, L, w), lambda r, c: (r, c, 0))
    dt_spec = pl.BlockSpec((rb, SSD_HEADS, L), lambda r, c: (r, 0, c))
    per_batch = lambda a: a.reshape(bsz, seqlen, a.shape[-1])
    consts = (cw, cb, dtb, alog, dskip_x, nw, tri3, exp2, shift)
    y = pl.pallas_call(
        _ssd_kernel,
        grid=(bsz // rb, nc),
        in_specs=[row(SSD_INNER), row(SSD_CONV_DIM), dt_spec] + [_const_spec(a.shape) for a in consts],
        out_specs=row(SSD_INNER),
        out_shape=jax.ShapeDtypeStruct((bsz, seqlen, SSD_INNER), BF16),
        scratch_shapes=[
            pltpu.VMEM((rb, CONV_HALO + L, SSD_CONV_DIM), BF16),
            pltpu.VMEM((rb, SSD_CHUNK, SSD_CONV_DIM), F32),
            pltpu.VMEM((rb, SSD_STATE, SSD_INNER), F32),
            pltpu.VMEM((rb, SSD_CHUNK, SSD_INNER), F32),
        ],
        compiler_params=pltpu.CompilerParams(
            dimension_semantics=("arbitrary", "arbitrary"), vmem_limit_bytes=VMEM_LIMIT),
        name="ssd",
    )(per_batch(z), per_batch(xbc), dt_t.reshape(SSD_HEADS, bsz, seqlen).transpose(1, 0, 2), *consts)
    return y.reshape(bsz * seqlen, SSD_INNER)


def _attn_kernel(qt_ref, k_ref, vt_ref, lq1_ref, lk1_ref, lq2_ref, lk2_ref, o_ref,
                 qs_ref, acc_ref, m_ref, mx_ref, s_ref):
    i = pl.program_id(2)
    d_idx = lax.broadcasted_iota(jnp.int32, (DA_V_DIM, TQ), 0)
    ones_rows = jnp.ones((ONES_ROWS, TK), BF16)

    for g in range(HEADS_PER_STEP):
        rows = slice(g * DA_V_DIM, (g + 1) * DA_V_DIM)
        q = qt_ref[rows, :]
        zero = jnp.zeros_like(q)
        qs_ref[rows, 0:TQ] = jnp.where(d_idx < DA_HEAD_DIM, q, zero)
        qs_ref[rows, TQ:2 * TQ] = jnp.where(d_idx < DA_HEAD_DIM, zero, q)

    def scores(g, j, diagonal):
        rows = slice(g * DA_V_DIM, (g + 1) * DA_V_DIM)
        off = pl.multiple_of(j * TK, TK)
        s = _dot(k_ref[pl.ds(off, TK), rows], qs_ref[rows, :])
        if diagonal:
            key = lax.broadcasted_iota(jnp.int32, s.shape, 0)
            qry = lax.broadcasted_iota(jnp.int32, s.shape, 1) & (TQ - 1)
            s = jnp.where(key <= qry, s, -jnp.inf)
        s_ref[g] = s
        mx_ref[g:g + 1, :] = jnp.max(s, axis=0, keepdims=True)

    def softmax_pv(g, j, first):
        rows = slice(g * DA_V_DIM, (g + 1) * DA_V_DIM)
        arow = slice(g * ACC_ROWS, (g + 1) * ACC_ROWS)
        off = pl.multiple_of(j * TK, TK)
        if first:
            m_new = mx_ref[g:g + 1, :]
        else:
            m_prev = m_ref[g:g + 1, :]
            m_new = jnp.maximum(m_prev, mx_ref[g:g + 1, :])
        p = jnp.exp2(s_ref[g] - m_new).astype(BF16)
        v_ext = jnp.concatenate([vt_ref[rows, pl.ds(off, TK)], ones_rows], axis=0)
        pv = _dot(v_ext, p)
        if first:
            acc_ref[arow, :] = pv
        else:
            acc_ref[arow, :] = jnp.exp2(m_prev - m_new) * acc_ref[arow, :] + pv
        m_ref[g:g + 1, :] = m_new

    def run_tiles(tiles, diagonal, next_tile):
        chains = [(t, g) for t in tiles for g in range(HEADS_PER_STEP)]
        for n, (t, g) in enumerate(chains):
            ahead = n + SCORE_LOOKAHEAD
            if ahead < len(chains):
                scores(chains[ahead][1], chains[ahead][0], diagonal)
            else:
                scores(ahead - len(chains), next_tile, False)
            softmax_pv(g, t, first=diagonal)

    for g in range(SCORE_LOOKAHEAD):
        scores(g, i, True)
    run_tiles([i], diagonal=True, next_tile=0)

    def body(jj, carry):
        run_tiles([2 * jj, 2 * jj + 1], diagonal=False, next_tile=2 * jj + 2)
        return carry

    lax.fori_loop(0, i // 2, body, 0)

    @pl.when(i % 2 == 1)
    def _():
        run_tiles([i - 1], diagonal=False, next_tile=i)

    lam = (jnp.exp(jnp.sum(lq1_ref[...] * lk1_ref[...], axis=1, keepdims=True))
           - jnp.exp(jnp.sum(lq2_ref[...] * lk2_ref[...], axis=1, keepdims=True)) + LAMBDA_INIT)
    for g in range(HEADS_PER_STEP):
        a0 = g * ACC_ROWS
        inv = 1.0 / acc_ref[a0 + DA_V_DIM:a0 + DA_V_DIM + 1, :]
        o = (acc_ref[a0:a0 + DA_V_DIM, 0:TQ] * inv[:, 0:TQ]
             - acc_ref[a0:a0 + DA_V_DIM, TQ:2 * TQ] * (lam * inv[:, TQ:2 * TQ]))
        o_ref[:, g * DA_V_DIM:(g + 1) * DA_V_DIM] = o.T.astype(o_ref.dtype)


def _attention(qt, k, vt, lq1, lk1, lq2, lk2, bsz, seqlen):
    assert TQ == TK
    nq = seqlen // TQ
    gw = HEADS_PER_STEP * DA_V_DIM
    small = (lq1, lk1, lq2, lk2)
    return pl.pallas_call(
        _attn_kernel,
        grid=(bsz, DA_HEADS // HEADS_PER_STEP, nq),
        in_specs=[
            pl.BlockSpec((gw, TQ), lambda b, h, i: (h, b * nq + i)),
            pl.BlockSpec((seqlen, gw), lambda b, h, i: (b, h)),
            pl.BlockSpec((gw, seqlen), lambda b, h, i: (h, b)),
        ] + [_const_spec(a.shape) for a in small],
        out_specs=pl.BlockSpec((TQ, gw), lambda b, h, i: (b * nq + i, h)),
        out_shape=jax.ShapeDtypeStruct((bsz * seqlen, DA_WIDTH), BF16),
        scratch_shapes=[
            pltpu.VMEM((gw, 2 * TQ), BF16),
            pltpu.VMEM((HEADS_PER_STEP * ACC_ROWS, 2 * TQ), F32),
            pltpu.VMEM((HEADS_PER_STEP, 2 * TQ), F32),
            pltpu.VMEM((HEADS_PER_STEP, 2 * TQ), F32),
            pltpu.VMEM((HEADS_PER_STEP, TK, 2 * TQ), F32),
        ],
        compiler_params=pltpu.CompilerParams(
            dimension_semantics=("arbitrary", "arbitrary", "arbitrary"), vmem_limit_bytes=VMEM_LIMIT),
        name="diff_attn",
    )(qt, k, vt, *small)


def _mlp_kernel(x_ref, ys_ref, ya_ref, sw_ref, wo_ref, nw_ref, wg_ref, wu_ref, wd_ref, fw_ref, o_ref, h_ref):
    gain = sw_ref[...] * (1.0 - LAMBDA_INIT)
    heads = []
    for g in range(DA_HEADS):
        o = ya_ref[:, g * DA_V_DIM:(g + 1) * DA_V_DIM].astype(F32)
        heads.append((o * _rms_scale(o) * gain).astype(BF16))
    ya = jnp.concatenate(heads, axis=1)
    h_ref[...] = x_ref[...] + _dot(ys_ref[...], wo_ref[0:SSD_INNER, :]) + _dot(ya, wo_ref[SSD_INNER:, :])
    h = h_ref[...]
    n2 = (h * _rms_scale(h) * nw_ref[...]).astype(BF16)
    ffn = None
    for f0, f1 in zip(FF_SPLITS[:-1], FF_SPLITS[1:]):
        gate = _dot(n2, wg_ref[:, f0:f1])
        up = _dot(n2, wu_ref[:, f0:f1])
        act = (_silu(gate) * up).astype(BF16)
        down = _dot(act, wd_ref[f0:f1, :])
        ffn = down if ffn is None else ffn + down
    out = h_ref[...] + ffn
    o_ref[...] = out * _rms_scale(out) * fw_ref[...]


def _mlp(x2, ys, ya, sw, wo, nw, wg, wu, wd, fw):
    t = x2.shape[0]
    tm = TM_PROJ
    row = pl.BlockSpec((tm, D_MODEL), lambda i: (i, 0))
    consts = (sw, wo, nw, wg, wu, wd, fw)
    return pl.pallas_call(
        _mlp_kernel,
        grid=(t // tm,),
        in_specs=[row, row, row] + [_const_spec(a.shape) for a in consts],
        out_specs=row,
        out_shape=jax.ShapeDtypeStruct((t, D_MODEL), F32),
        scratch_shapes=[pltpu.VMEM((tm, D_MODEL), F32)],
        compiler_params=pltpu.CompilerParams(
            dimension_semantics=("arbitrary",), vmem_limit_bytes=VMEM_LIMIT),
        name="mlp",
    )(x2, ys, ya, *consts)


def kernel(x, mix_norm_w, w_in, conv_w, conv_b, dt_bias, a_log, d_skip, ssd_norm_w, lam_q1, lam_k1, lam_q2,
           lam_k2, subln_w, w_out, ffn_norm_w, w_gate, w_up, w_down, final_norm_w):
    bsz, seqlen, _ = x.shape
    x2 = x.reshape(bsz * seqlen, D_MODEL)

    (z, xbc, dt_t, k, qt, vt), (wo16, wg16, wu16, wd16) = _in_proj(
        x2, mix_norm_w[0][None, :], w_in[0].T.astype(BF16), (w_out[0], w_gate[0], w_up[0], w_down[0]))

    idx = jnp.arange(SSD_CHUNK)
    tri = (idx[:, None] >= idx[None, :]).astype(BF16)
    tri3 = jnp.concatenate([tri.T, tri.T, tri.T], axis=0)
    sel = (jnp.arange(LANES)[:, None] == (jnp.arange(SSD_INNER)[None, :] // SSD_HEAD_DIM)).astype(BF16)
    exp2 = jnp.concatenate([sel, sel], axis=0)
    src = CONV_HALO + idx[None, :, None] - jnp.arange(1, SSD_CONV)[:, None, None]
    shift = (jnp.arange(CONV_HALO + SSD_CHUNK)[None, None, :] == src).astype(BF16)
    shift = shift.reshape((SSD_CONV - 1) * SSD_CHUNK, CONV_HALO + SSD_CHUNK)
    dskip_x = jnp.repeat(d_skip[0].astype(F32), SSD_HEAD_DIM)[None, :]

    per_head = lambda v: jnp.broadcast_to(v.astype(F32)[:, None], (SSD_HEADS, SSD_CHUNK))
    y_ssd = _ssd(z, xbc, dt_t, conv_w[0], conv_b[0][None, :], per_head(dt_bias[0]), per_head(a_log[0]),
                 dskip_x, ssd_norm_w[0][None, :], tri3, exp2, shift, bsz, seqlen)
    y_da = _attention(qt, k, vt, lam_q1[0][None, :], lam_k1[0][None, :], lam_q2[0][None, :],
                      lam_k2[0][None, :], bsz, seqlen)

    out = _mlp(x2, y_ssd, y_da, subln_w[0][None, :], wo16, ffn_norm_w[0][None, :], wg16, wu16, wd16,
               final_norm_w[None, :])
    return out.reshape(bsz, seqlen, D_MODEL)
```

```python
import functools
import math

import jax
import jax.numpy as jnp
from jax import lax
from jax.experimental import pallas as pl
from jax.experimental.pallas import tpu as pltpu

F32 = jnp.float32
BF16 = jnp.bfloat16

EPS = 1e-5
D_MODEL = 1024
SSD_HEADS = 16
SSD_HEAD_DIM = 64
SSD_INNER = SSD_HEADS * SSD_HEAD_DIM
SSD_GROUPS = 2
SSD_GROUP_WIDTH = SSD_INNER // SSD_GROUPS
SSD_STATE = 128
SSD_CONV = 4
SSD_CHUNK = 128
SSD_BC = SSD_GROUPS * SSD_STATE
SSD_CONV_DIM = SSD_INNER + 2 * SSD_BC
DA_HEADS = 8
DA_HEAD_DIM = 64
DA_V_DIM = 2 * DA_HEAD_DIM
DA_WIDTH = DA_HEADS * DA_V_DIM
D_FF = 2816
LAMBDA_INIT = 0.8 - 0.6 * math.exp(-0.3 * 0)

LANES = 128
CONV_HALO = 16
CONV_LANES = 512
VMEM_LIMIT = 56 * 1024 * 1024

TM_PROJ = 512
TM_IN_PROJ = 1024
TQ = 256
TK = 256
MXU_WIDTH = 256
FF_SPLITS = (0, 6 * MXU_WIDTH, D_FF)
HEADS_PER_STEP = 8
MLP_PIECE_ROWS = 256
SCORE_LOOKAHEAD = 2
ONES_ROWS = 16
ACC_ROWS = DA_V_DIM + ONES_ROWS
LOG2E = math.log2(math.e)


def _const_spec(shape):
    nd = len(shape)
    return pl.BlockSpec(shape, lambda *_: (0,) * nd, pipeline_mode=pl.Buffered(1))


def _rms_scale(xf):
    return lax.rsqrt(jnp.mean(xf * xf, axis=-1, keepdims=True) + EPS)


def _dot(a, b):
    return jnp.dot(a, b, preferred_element_type=F32)


def _dot_nt(a, b):
    return lax.dot_general(a, b, (((1,), (1,)), ((), ())), preferred_element_type=F32)


def _split3(a):
    hi = a.astype(BF16)
    r1 = a - hi.astype(F32)
    mid = r1.astype(BF16)
    lo = (r1 - mid.astype(F32)).astype(BF16)
    return hi, mid, lo


def _silu_of_twice(h):
    return h + h * jnp.tanh(h)


def _silu(x):
    return _silu_of_twice(0.5 * x)


def _in_proj_kernel(x_ref, nw_ref, wt_ref, *refs):
    n_cast = (len(refs) - 6) // 2
    cast_in, (z_ref, xbc_ref, dt_ref, k_ref, qt_ref, vt_ref), cast_out = (
        refs[:n_cast], refs[n_cast:n_cast + 6], refs[n_cast + 6:])
    for src, dst in zip(cast_in, cast_out):
        dst[...] = src[...].astype(BF16)
    xf = x_ref[...]
    xn = (xf * _rms_scale(xf) * nw_ref[...]).astype(BF16)
    r0 = 0
    for ref, width, scale, transposed in (
            (z_ref, SSD_INNER, 0.5, False), (xbc_ref, SSD_CONV_DIM, None, False), (dt_ref, SSD_HEADS, None, True),
            (qt_ref, DA_WIDTH, DA_HEAD_DIM ** -0.5 * LOG2E, True), (k_ref, DA_WIDTH, None, False),
            (vt_ref, DA_WIDTH, None, True)):
        w = wt_ref[r0:r0 + width, :]
        r = _dot_nt(w, xn) if transposed else _dot_nt(xn, w)
        ref[...] = (r if scale is None else r * scale).astype(ref.dtype)
        r0 += width


def _in_proj(x2, nw, wt, to_cast):
    t = x2.shape[0]
    tm = TM_IN_PROJ
    steps = t // tm
    row = lambda w: pl.BlockSpec((tm, w), lambda i: (i, 0))
    col = pl.BlockSpec((DA_WIDTH, tm), lambda i: (0, i))
    cast_specs = [pl.BlockSpec((a.shape[0] // steps, a.shape[1]), lambda i: (i, 0)) for a in to_cast]
    outs = pl.pallas_call(
        _in_proj_kernel,
        grid=(steps,),
        in_specs=[row(D_MODEL), _const_spec(nw.shape), _const_spec(wt.shape)] + cast_specs,
        out_specs=[row(SSD_INNER), row(SSD_CONV_DIM), pl.BlockSpec((SSD_HEADS, tm), lambda i: (0, i)),
                   row(DA_WIDTH), col, col] + cast_specs,
        out_shape=[
            jax.ShapeDtypeStruct((t, SSD_INNER), BF16),
            jax.ShapeDtypeStruct((t, SSD_CONV_DIM), BF16),
            jax.ShapeDtypeStruct((SSD_HEADS, t), F32),
            jax.ShapeDtypeStruct((t, DA_WIDTH), BF16),
            jax.ShapeDtypeStruct((DA_WIDTH, t), BF16),
            jax.ShapeDtypeStruct((DA_WIDTH, t), BF16),
        ] + [jax.ShapeDtypeStruct(a.shape, BF16) for a in to_cast],
        compiler_params=pltpu.CompilerParams(
            dimension_semantics=("arbitrary",), vmem_limit_bytes=VMEM_LIMIT),
        name="in_proj",
    )(x2, nw, wt, *to_cast)
    return outs[:6], outs[6:]


def _ssd_chunk(z_ref, xbc_ref, dt_ref, cw_ref, cb_ref, dtb_ref, alog_ref, dskip_ref, nw_ref,
               tri3_ref, exp2_ref, shift_ref, y_ref, ext_ref, conv_ref, state_ref, yacc_ref):
    L = SSD_CHUNK
    ext_ref[CONV_HALO:CONV_HALO + L, :] = xbc_ref[...]

    for c0 in range(0, SSD_CONV_DIM, CONV_LANES):
        cols = slice(c0, c0 + CONV_LANES)
        e = ext_ref[:, cols]
        cwh = 0.5 * cw_ref[:, cols]
        acc = 0.5 * cb_ref[:, cols] + cwh[SSD_CONV - 1:SSD_CONV, :] * e[CONV_HALO:, :].astype(F32)
        shifted = _dot(shift_ref[...], e)
        for back in range(1, SSD_CONV):
            j = SSD_CONV - 1 - back
            acc = acc + cwh[j:j + 1, :] * shifted[(back - 1) * L:back * L, :]
        conv_ref[:, cols] = _silu_of_twice(acc)

    dtr = dt_ref[...] + dtb_ref[...]
    dtv = jnp.maximum(dtr, 0.0) + jnp.log(1.0 + jnp.exp(-jnp.abs(dtr)))
    adt = dtv * (-jnp.exp(alog_ref[...]))
    cs_t = _dot(jnp.concatenate(_split3(adt), axis=1), tri3_ref[...]) * LOG2E
    yield
    csd_t = cs_t - jnp.log2(dtv)
    w_state_t = dtv * jnp.exp2(cs_t[:, L - 1:L] - cs_t)

    def by_time(a_t):
        return jnp.concatenate([a_t, jnp.zeros((LANES - SSD_HEADS, L), F32)], axis=0).T

    cs = by_time(cs_t)
    ecs = by_time(jnp.exp2(cs_t))
    w_state = by_time(w_state_t)

    def expand(a):
        hi = a.astype(BF16)
        lo = (a - hi.astype(F32)).astype(BF16)
        return _dot(jnp.concatenate([hi, lo], axis=1), exp2_ref[...])

    ecs_x = expand(ecs)
    wst_x = expand(w_state)
    yield

    row = lax.broadcasted_iota(jnp.int32, (L, L), 0)
    colm = lax.broadcasted_iota(jnp.int32, (L, L), 1)
    tril = row >= colm
    lane = lax.broadcasted_iota(jnp.int32, (L, LANES), 1)
    lo_half = lane < SSD_HEAD_DIM

    for g in range(SSD_GROUPS):
        gx = g * SSD_GROUP_WIDTH
        bm = conv_ref[:, SSD_INNER + g * SSD_STATE:SSD_INNER + (g + 1) * SSD_STATE]
        cm = conv_ref[:, SSD_INNER + SSD_BC + g * SSD_STATE:SSD_INNER + SSD_BC + (g + 1) * SSD_STATE]
        cm16 = cm.astype(BF16)
        cb = _dot_nt(cm16, bm.astype(BF16))
        xs_g = conv_ref[:, gx:gx + SSD_GROUP_WIDTH]

        y_off = _dot(cm16, state_ref[:, gx:gx + SSD_GROUP_WIDTH].astype(BF16))
        yield
        yacc_ref[:, gx:gx + SSD_GROUP_WIDTH] = (
            y_off * ecs_x[:, gx:gx + SSD_GROUP_WIDTH] + xs_g * dskip_ref[:, gx:gx + SSD_GROUP_WIDTH])

        for pair in range(SSD_HEADS // SSD_GROUPS // 2):
            ms = []
            h0 = g * (SSD_HEADS // SSD_GROUPS) + 2 * pair
            for h in (h0, h0 + 1):
                seg = cs[:, h:h + 1] - csd_t[h:h + 1, :]
                ms.append((cb * jnp.exp2(jnp.where(tril, seg, -jnp.inf))).astype(BF16))
            x_pair = conv_ref[:, gx + pair * LANES:gx + (pair + 1) * LANES]
            x_blk = jnp.concatenate(
                [jnp.where(lo_half, x_pair, 0.0), jnp.where(lo_half, 0.0, x_pair)], axis=0).astype(BF16)
            sl = slice(gx + pair * LANES, gx + (pair + 1) * LANES)
            y_diag = _dot(jnp.concatenate(ms, axis=1), x_blk)
            yield
            yacc_ref[:, sl] = yacc_ref[:, sl] + y_diag

        xd = (xs_g * wst_x[:, gx:gx + SSD_GROUP_WIDTH]).astype(BF16)
        contrib = _dot(bm.T.astype(BF16), xd)
        yield
        state_ref[:, gx:gx + SSD_GROUP_WIDTH] = (
            state_ref[:, gx:gx + SSD_GROUP_WIDTH] * ecs_x[L - 1:L, gx:gx + SSD_GROUP_WIDTH] + contrib)

        z_half = z_ref[:, gx:gx + SSD_GROUP_WIDTH].astype(F32)
        gy = yacc_ref[:, gx:gx + SSD_GROUP_WIDTH] * _silu_of_twice(z_half)
        y_ref[:, gx:gx + SSD_GROUP_WIDTH] = (
            gy * _rms_scale(gy) * nw_ref[:, gx:gx + SSD_GROUP_WIDTH]).astype(y_ref.dtype)


def _attn_kernel(qt_ref, k_ref, vt_ref, lq1_ref, lk1_ref, lq2_ref, lk2_ref, o_ref,
                 qs_ref, acc_ref, m_ref, mx_ref, s_ref):
    i = pl.program_id(2)
    d_idx = lax.broadcasted_iota(jnp.int32, (DA_V_DIM, TQ), 0)
    ones_rows = jnp.ones((ONES_ROWS, TK), BF16)

    for g in range(HEADS_PER_STEP):
        rows = slice(g * DA_V_DIM, (g + 1) * DA_V_DIM)
        q = qt_ref[rows, :]
        zero = jnp.zeros_like(q)
        qs_ref[rows, 0:TQ] = jnp.where(d_idx < DA_HEAD_DIM, q, zero)
        qs_ref[rows, TQ:2 * TQ] = jnp.where(d_idx < DA_HEAD_DIM, zero, q)

    def scores(g, j, diagonal):
        rows = slice(g * DA_V_DIM, (g + 1) * DA_V_DIM)
        off = pl.multiple_of(j * TK, TK)
        s = _dot(k_ref[pl.ds(off, TK), rows], qs_ref[rows, :])
        if diagonal:
            key = lax.broadcasted_iota(jnp.int32, s.shape, 0)
            qry = lax.broadcasted_iota(jnp.int32, s.shape, 1) & (TQ - 1)
            s = jnp.where(key <= qry, s, -jnp.inf)
        s_ref[g] = s
        mx_ref[g:g + 1, :] = jnp.max(s, axis=0, keepdims=True)

    def softmax_pv(g, j, first):
        rows = slice(g * DA_V_DIM, (g + 1) * DA_V_DIM)
        arow = slice(g * ACC_ROWS, (g + 1) * ACC_ROWS)
        off = pl.multiple_of(j * TK, TK)
        if first:
            m_new = mx_ref[g:g + 1, :]
        else:
            m_prev = m_ref[g:g + 1, :]
            m_new = jnp.maximum(m_prev, mx_ref[g:g + 1, :])
        p = jnp.exp2(s_ref[g] - m_new).astype(BF16)
        v_ext = jnp.concatenate([vt_ref[rows, pl.ds(off, TK)], ones_rows], axis=0)
        pv = _dot(v_ext, p)
        if first:
            acc_ref[arow, :] = pv
        else:
            acc_ref[arow, :] = jnp.exp2(m_prev - m_new) * acc_ref[arow, :] + pv
        m_ref[g:g + 1, :] = m_new

    def run_tiles(tiles, diagonal, next_tile):
        chains = [(t, g) for t in tiles for g in range(HEADS_PER_STEP)]
        for n, (t, g) in enumerate(chains):
            ahead = n + SCORE_LOOKAHEAD
            if ahead < len(chains):
                scores(chains[ahead][1], chains[ahead][0], diagonal)
            else:
                scores(ahead - len(chains), next_tile, False)
            softmax_pv(g, t, first=diagonal)

    for g in range(SCORE_LOOKAHEAD):
        scores(g, i, True)
    run_tiles([i], diagonal=True, next_tile=0)

    def body(jj, carry):
        run_tiles([2 * jj, 2 * jj + 1], diagonal=False, next_tile=2 * jj + 2)
        return carry

    lax.fori_loop(0, i // 2, body, 0)

    @pl.when(i % 2 == 1)
    def _():
        run_tiles([i - 1], diagonal=False, next_tile=i)

    lam = (jnp.exp(jnp.sum(lq1_ref[...] * lk1_ref[...], axis=1, keepdims=True))
           - jnp.exp(jnp.sum(lq2_ref[...] * lk2_ref[...], axis=1, keepdims=True)) + LAMBDA_INIT)
    for g in range(HEADS_PER_STEP):
        a0 = g * ACC_ROWS
        inv = 1.0 / acc_ref[a0 + DA_V_DIM:a0 + DA_V_DIM + 1, :]
        o = (acc_ref[a0:a0 + DA_V_DIM, 0:TQ] * inv[:, 0:TQ]
             - acc_ref[a0:a0 + DA_V_DIM, TQ:2 * TQ] * (lam * inv[:, TQ:2 * TQ]))
        o_ref[:, g * DA_V_DIM:(g + 1) * DA_V_DIM] = o.T.astype(o_ref.dtype)


def _attention(qt, k, vt, lq1, lk1, lq2, lk2, bsz, seqlen):
    assert TQ == TK
    nq = seqlen // TQ
    gw = HEADS_PER_STEP * DA_V_DIM
    small = (lq1, lk1, lq2, lk2)
    return pl.pallas_call(
        _attn_kernel,
        grid=(bsz, DA_HEADS // HEADS_PER_STEP, nq),
        in_specs=[
            pl.BlockSpec((gw, TQ), lambda b, h, i: (h, b * nq + i)),
            pl.BlockSpec((seqlen, gw), lambda b, h, i: (b, h)),
            pl.BlockSpec((gw, seqlen), lambda b, h, i: (h, b)),
        ] + [_const_spec(a.shape) for a in small],
        out_specs=pl.BlockSpec((TQ, gw), lambda b, h, i: (b * nq + i, h)),
        out_shape=jax.ShapeDtypeStruct((bsz * seqlen, DA_WIDTH), BF16),
        scratch_shapes=[
            pltpu.VMEM((gw, 2 * TQ), BF16),
            pltpu.VMEM((HEADS_PER_STEP * ACC_ROWS, 2 * TQ), F32),
            pltpu.VMEM((HEADS_PER_STEP, 2 * TQ), F32),
            pltpu.VMEM((HEADS_PER_STEP, 2 * TQ), F32),
            pltpu.VMEM((HEADS_PER_STEP, TK, 2 * TQ), F32),
        ],
        compiler_params=pltpu.CompilerParams(
            dimension_semantics=("arbitrary", "arbitrary", "arbitrary"), vmem_limit_bytes=VMEM_LIMIT),
        name="diff_attn",
    )(qt, k, vt, *small)


def _mlp_stages(x_ref, ys_ref, ya_ref, sw_ref, wo_ref, nw_ref, wg_ref, wu_ref, wd_ref, fw_ref, o_ref,
                h_ref, act_ref, ffn_ref):
    gain = sw_ref[...] * (1.0 - LAMBDA_INIT)
    heads = []
    for g in range(DA_HEADS):
        o = ya_ref[:, g * DA_V_DIM:(g + 1) * DA_V_DIM].astype(F32)
        heads.append((o * _rms_scale(o) * gain).astype(BF16))
    ya = jnp.concatenate(heads, axis=1)
    ys = ys_ref[...]
    row_blocks = [slice(r0, r0 + MLP_PIECE_ROWS) for r0 in range(0, ys.shape[0], MLP_PIECE_ROWS)]
    for n0 in range(0, D_MODEL, MXU_WIDTH):
        cols = slice(n0, n0 + MXU_WIDTH)
        for rs in row_blocks:
            part = _dot(ys[rs], wo_ref[0:SSD_INNER, cols])
            yield
            part = part + _dot(ya[rs], wo_ref[SSD_INNER:, cols])
            yield
            h_ref[rs, cols] = x_ref[rs, cols] + part
    h = h_ref[...]
    n2 = (h * _rms_scale(h) * nw_ref[...]).astype(BF16)
    for f0, f1 in zip(FF_SPLITS[:-1], FF_SPLITS[1:]):
        for n0 in range(f0, f1, MXU_WIDTH):
            cols = slice(n0, n0 + MXU_WIDTH)
            for rs in row_blocks:
                gate = _dot(n2[rs], wg_ref[:, cols])
                yield
                up = _dot(n2[rs], wu_ref[:, cols])
                yield
                act_ref[rs, n0 - f0:n0 - f0 + MXU_WIDTH] = (_silu(gate) * up).astype(BF16)
        for n0 in range(0, D_MODEL, MXU_WIDTH):
            cols = slice(n0, n0 + MXU_WIDTH)
            for rs in row_blocks:
                down = _dot(act_ref[rs, 0:f1 - f0], wd_ref[f0:f1, cols])
                yield
                ffn_ref[rs, cols] = down if f0 == 0 else ffn_ref[rs, cols] + down
    out = h_ref[...] + ffn_ref[...]
    o_ref[...] = out * _rms_scale(out) * fw_ref[...]


def _interleave(main, side):
    done = object()
    a = b = None
    while a is not done or b is not done:
        b = next(side, done)
        a = next(main, done)


def _ssd_mlp_kernel(tiles_per_row, x_ref, ya_ref, z_ref, xbc_ref, dt_ref, cw_ref, cb_ref, dtb_ref, alog_ref,
                    dskip_ref, snw_ref, tri3_ref, exp2_ref, shift_ref, sw_ref, wo_ref, nw_ref, wg_ref, wu_ref,
                    wd_ref, fw_ref, o_ref, ys_ref, h_ref, act_ref, ffn_ref, ext_ref, conv_ref, state_ref, yacc_ref):
    L = SSD_CHUNK
    s = pl.program_id(0)
    n_tiles = pl.num_programs(0) - 1
    tile = jnp.minimum(s, n_tiles - 1)
    row_start = tile % tiles_per_row == 0

    @pl.when(s == 0)
    def _():
        ys_ref[1] = jnp.zeros(ys_ref.shape[1:], BF16)

    @pl.when(row_start)
    def _():
        state_ref[...] = jnp.zeros_like(state_ref)
        ext_ref[0:CONV_HALO, :] = jnp.zeros((CONV_HALO, SSD_CONV_DIM), BF16)

    @pl.when(jnp.logical_not(row_start))
    def _():
        ext_ref[0:CONV_HALO, :] = ext_ref[TM_PROJ:TM_PROJ + CONV_HALO, :]

    slot = s % 2
    consts = (cw_ref, cb_ref, dtb_ref, alog_ref, dskip_ref, snw_ref, tri3_ref, exp2_ref, shift_ref)

    def ssd_tile():
        for n in range(TM_PROJ // L):
            t = pl.ds(n * L, L)
            yield from _ssd_chunk(z_ref.at[t], xbc_ref.at[t], dt_ref.at[:, t], *consts, ys_ref.at[slot, t],
                                  ext_ref.at[pl.ds(n * L, CONV_HALO + L)], conv_ref.at[n % 2], state_ref,
                                  yacc_ref.at[n % 2])

    mlp = _mlp_stages(x_ref, ys_ref.at[1 - slot], ya_ref, sw_ref, wo_ref, nw_ref, wg_ref, wu_ref, wd_ref, fw_ref,
                      o_ref, h_ref, act_ref, ffn_ref)
    _interleave(mlp, ssd_tile())


def _ssd_mlp(x2, ya, z, xbc, dt_t, ssd_consts, mlp_consts, seqlen):
    t = x2.shape[0]
    tm = TM_PROJ
    n_tiles = t // tm
    assert seqlen % tm == 0
    prev = lambda s: (jnp.maximum(s - 1, 0), 0)
    cur = lambda s: (jnp.minimum(s, n_tiles - 1), 0)
    consts = tuple(ssd_consts) + tuple(mlp_consts)
    return pl.pallas_call(
        functools.partial(_ssd_mlp_kernel, seqlen // tm),
        grid=(n_tiles + 1,),
        in_specs=[pl.BlockSpec((tm, D_MODEL), prev), pl.BlockSpec((tm, DA_WIDTH), prev),
                  pl.BlockSpec((tm, SSD_INNER), cur), pl.BlockSpec((tm, SSD_CONV_DIM), cur),
                  pl.BlockSpec((SSD_HEADS, tm), lambda s: (0, jnp.minimum(s, n_tiles - 1)))]
                 + [_const_spec(a.shape) for a in consts],
        out_specs=pl.BlockSpec((tm, D_MODEL), prev),
        out_shape=jax.ShapeDtypeStruct((t, D_MODEL), F32),
        scratch_shapes=[
            pltpu.VMEM((2, tm, SSD_INNER), BF16),
            pltpu.VMEM((tm, D_MODEL), F32),
            pltpu.VMEM((tm, FF_SPLITS[1] - FF_SPLITS[0]), BF16),
            pltpu.VMEM((tm, D_MODEL), F32),
            pltpu.VMEM((CONV_HALO + tm, SSD_CONV_DIM), BF16),
            pltpu.VMEM((2, SSD_CHUNK, SSD_CONV_DIM), F32),
            pltpu.VMEM((SSD_STATE, SSD_INNER), F32),
            pltpu.VMEM((2, SSD_CHUNK, SSD_INNER), F32),
        ],
        compiler_params=pltpu.CompilerParams(
            dimension_semantics=("arbitrary",), vmem_limit_bytes=VMEM_LIMIT),
        name="ssd_mlp",
    )(x2, ya, z, xbc, dt_t, *consts)


def kernel(x, mix_norm_w, w_in, conv_w, conv_b, dt_bias, a_log, d_skip, ssd_norm_w, lam_q1, lam_k1, lam_q2,
           lam_k2, subln_w, w_out, ffn_norm_w, w_gate, w_up, w_down, final_norm_w):
    bsz, seqlen, _ = x.shape
    x2 = x.reshape(bsz * seqlen, D_MODEL)

    (z, xbc, dt_t, k, qt, vt), (wo16, wg16, wu16, wd16) = _in_proj(
        x2, mix_norm_w[0][None, :], w_in[0].T.astype(BF16), (w_out[0], w_gate[0], w_up[0], w_down[0]))

    idx = jnp.arange(SSD_CHUNK)
    tri = (idx[:, None] >= idx[None, :]).astype(BF16)
    tri3 = jnp.concatenate([tri.T, tri.T, tri.T], axis=0)
    sel = (jnp.arange(LANES)[:, None] == (jnp.arange(SSD_INNER)[None, :] // SSD_HEAD_DIM)).astype(BF16)
    exp2 = jnp.concatenate([sel, sel], axis=0)
    src = CONV_HALO + idx[None, :, None] - jnp.arange(1, SSD_CONV)[:, None, None]
    shift = (jnp.arange(CONV_HALO + SSD_CHUNK)[None, None, :] == src).astype(BF16)
    shift = shift.reshape((SSD_CONV - 1) * SSD_CHUNK, CONV_HALO + SSD_CHUNK)
    dskip_x = jnp.repeat(d_skip[0].astype(F32), SSD_HEAD_DIM)[None, :]

    per_head = lambda v: jnp.broadcast_to(v.astype(F32)[:, None], (SSD_HEADS, SSD_CHUNK))
    ssd_consts = (conv_w[0], conv_b[0][None, :], per_head(dt_bias[0]), per_head(a_log[0]), dskip_x,
                  ssd_norm_w[0][None, :], tri3, exp2, shift)
    y_da = _attention(qt, k, vt, lam_q1[0][None, :], lam_k1[0][None, :], lam_q2[0][None, :],
                      lam_k2[0][None, :], bsz, seqlen)
    mlp_consts = (subln_w[0][None, :], wo16, ffn_norm_w[0][None, :], wg16, wu16, wd16, final_norm_w[None, :])
    out = _ssd_mlp(x2, y_da, z, xbc, dt_t, ssd_consts, mlp_consts, seqlen)
    return out.reshape(bsz, seqlen, D_MODEL)
```

```python
import math

import jax
import jax.numpy as jnp
from jax import lax
from jax.experimental import pallas as pl
from jax.experimental.pallas import tpu as pltpu

F32 = jnp.float32
BF16 = jnp.bfloat16

EPS = 1e-5
D_MODEL = 1024
SSD_HEADS = 16
SSD_HEAD_DIM = 64
SSD_INNER = SSD_HEADS * SSD_HEAD_DIM
SSD_GROUPS = 2
SSD_GROUP_WIDTH = SSD_INNER // SSD_GROUPS
SSD_STATE = 128
SSD_CONV = 4
SSD_CHUNK = 128
SSD_BC = SSD_GROUPS * SSD_STATE
SSD_CONV_DIM = SSD_INNER + 2 * SSD_BC
DA_HEADS = 8
DA_HEAD_DIM = 64
DA_V_DIM = 2 * DA_HEAD_DIM
DA_WIDTH = DA_HEADS * DA_V_DIM
D_FF = 2816
LAMBDA_INIT = 0.8 - 0.6 * math.exp(-0.3 * 0)

LANES = 128
CONV_HALO = 16
CONV_LANES = 512
VMEM_LIMIT = 56 * 1024 * 1024

TM_PROJ = 512
TM_IN_PROJ = 1024
TQ = 256
TK = 256
MXU_WIDTH = 256
FF_SPLITS = (0, 6 * MXU_WIDTH, D_FF)
HEADS_PER_STEP = 8
Q_TILES_PER_STEP = 2
SSD_ROWS_PER_STEP = 4
SSD_CHUNKS_PER_STEP = 2
SCORE_LOOKAHEAD = 2
ONES_ROWS = 16
ACC_ROWS = DA_V_DIM + ONES_ROWS
LOG2E = math.log2(math.e)


def _const_spec(shape):
    nd = len(shape)
    return pl.BlockSpec(shape, lambda *_: (0,) * nd, pipeline_mode=pl.Buffered(1))


def _rms_scale(xf):
    return lax.rsqrt(jnp.mean(xf * xf, axis=-1, keepdims=True) + EPS)


def _dot(a, b):
    return jnp.dot(a, b, preferred_element_type=F32)


def _dot_nt(a, b):
    return lax.dot_general(a, b, (((1,), (1,)), ((), ())), preferred_element_type=F32)


def _split3(a):
    hi = a.astype(BF16)
    r1 = a - hi.astype(F32)
    mid = r1.astype(BF16)
    lo = (r1 - mid.astype(F32)).astype(BF16)
    return hi, mid, lo


def _silu_of_twice(h):
    return h + h * jnp.tanh(h)


def _silu(x):
    return _silu_of_twice(0.5 * x)


def _in_proj_kernel(x_ref, nw_ref, wt_ref, *refs):
    n_cast = (len(refs) - 6) // 2
    cast_in, (z_ref, xbc_ref, dt_ref, k_ref, qt_ref, vt_ref), cast_out = (
        refs[:n_cast], refs[n_cast:n_cast + 6], refs[n_cast + 6:])
    for src, dst in zip(cast_in, cast_out):
        dst[...] = src[...].astype(BF16)
    xf = x_ref[...]
    xn = (xf * _rms_scale(xf) * nw_ref[...]).astype(BF16)
    r0 = 0
    for ref, width, scale, transposed in (
            (z_ref, SSD_INNER, 0.5, False), (xbc_ref, SSD_CONV_DIM, None, False), (dt_ref, SSD_HEADS, None, True),
            (qt_ref, DA_WIDTH, DA_HEAD_DIM ** -0.5 * LOG2E, True), (k_ref, DA_WIDTH, None, False),
            (vt_ref, DA_WIDTH, None, True)):
        w = wt_ref[r0:r0 + width, :]
        r = _dot_nt(w, xn) if transposed else _dot_nt(xn, w)
        ref[...] = (r if scale is None else r * scale).astype(ref.dtype)
        r0 += width


def _in_proj(x2, nw, wt, to_cast):
    t = x2.shape[0]
    tm = TM_IN_PROJ
    steps = t // tm
    row = lambda w: pl.BlockSpec((tm, w), lambda i: (i, 0))
    col = pl.BlockSpec((DA_WIDTH, tm), lambda i: (0, i))
    cast_specs = [pl.BlockSpec((a.shape[0] // steps, a.shape[1]), lambda i: (i, 0)) for a in to_cast]
    outs = pl.pallas_call(
        _in_proj_kernel,
        grid=(steps,),
        in_specs=[row(D_MODEL), _const_spec(nw.shape), _const_spec(wt.shape)] + cast_specs,
        out_specs=[row(SSD_INNER), row(SSD_CONV_DIM), pl.BlockSpec((SSD_HEADS, tm), lambda i: (0, i)),
                   row(DA_WIDTH), col, col] + cast_specs,
        out_shape=[
            jax.ShapeDtypeStruct((t, SSD_INNER), BF16),
            jax.ShapeDtypeStruct((t, SSD_CONV_DIM), BF16),
            jax.ShapeDtypeStruct((SSD_HEADS, t), F32),
            jax.ShapeDtypeStruct((t, DA_WIDTH), BF16),
            jax.ShapeDtypeStruct((DA_WIDTH, t), BF16),
            jax.ShapeDtypeStruct((DA_WIDTH, t), BF16),
        ] + [jax.ShapeDtypeStruct(a.shape, BF16) for a in to_cast],
        compiler_params=pltpu.CompilerParams(
            dimension_semantics=("arbitrary",), vmem_limit_bytes=VMEM_LIMIT),
        name="in_proj",
    )(x2, nw, wt, *to_cast)
    return outs[:6], outs[6:]


def _ssd_kernel(z_ref, xbc_ref, dt_ref, cw_ref, cb_ref, dtb_ref, alog_ref, dskip_ref, nw_ref,
                tri3_ref, exp2_ref, shift_ref, y_ref, ext_ref, conv_ref, state_ref, yacc_ref):
    L = SSD_CHUNK
    c = pl.program_id(1)

    @pl.when(c == 0)
    def _():
        state_ref[...] = jnp.zeros_like(state_ref)
        ext_ref[:, 0:CONV_HALO, :] = jnp.zeros((ext_ref.shape[0], CONV_HALO, SSD_CONV_DIM), BF16)

    @pl.when(c > 0)
    def _():
        ext_ref[:, 0:CONV_HALO, :] = ext_ref[:, SSD_CHUNKS_PER_STEP * L:SSD_CHUNKS_PER_STEP * L + CONV_HALO, :]

    consts = (cw_ref, cb_ref, dtb_ref, alog_ref, dskip_ref, nw_ref, tri3_ref, exp2_ref, shift_ref)

    def row_chunks(b):
        for n in range(SSD_CHUNKS_PER_STEP):
            t = pl.ds(n * L, L)
            yield from _ssd_chunk(z_ref.at[b, t], xbc_ref.at[b, t], dt_ref.at[b, :, t], *consts, y_ref.at[b, t],
                                  ext_ref.at[b, pl.ds(n * L, CONV_HALO + L)], conv_ref.at[b], state_ref.at[b],
                                  yacc_ref.at[b])

    _round_robin([row_chunks(b) for b in range(z_ref.shape[0])])


def _round_robin(stages):
    stages = list(stages)
    while stages:
        for gen in list(stages):
            if next(gen, StopIteration) is StopIteration:
                stages.remove(gen)


def _ssd_chunk(z_ref, xbc_ref, dt_ref, cw_ref, cb_ref, dtb_ref, alog_ref, dskip_ref, nw_ref,
               tri3_ref, exp2_ref, shift_ref, y_ref, ext_ref, conv_ref, state_ref, yacc_ref):
    L = SSD_CHUNK
    ext_ref[CONV_HALO:CONV_HALO + L, :] = xbc_ref[...]

    for c0 in range(0, SSD_CONV_DIM, CONV_LANES):
        cols = slice(c0, c0 + CONV_LANES)
        e = ext_ref[:, cols]
        cwh = 0.5 * cw_ref[:, cols]
        acc = 0.5 * cb_ref[:, cols] + cwh[SSD_CONV - 1:SSD_CONV, :] * e[CONV_HALO:, :].astype(F32)
        shifted = _dot(shift_ref[...], e)
        for back in range(1, SSD_CONV):
            j = SSD_CONV - 1 - back
            acc = acc + cwh[j:j + 1, :] * shifted[(back - 1) * L:back * L, :]
        conv_ref[:, cols] = _silu_of_twice(acc)

    dtr = dt_ref[...] + dtb_ref[...]
    dtv = jnp.maximum(dtr, 0.0) + jnp.log(1.0 + jnp.exp(-jnp.abs(dtr)))
    adt = dtv * (-jnp.exp(alog_ref[...]))
    cs_t = _dot(jnp.concatenate(_split3(adt), axis=1), tri3_ref[...]) * LOG2E
    yield
    csd_t = cs_t - jnp.log2(dtv)
    w_state_t = dtv * jnp.exp2(cs_t[:, L - 1:L] - cs_t)

    def by_time(a_t):
        return jnp.concatenate([a_t, jnp.zeros((LANES - SSD_HEADS, L), F32)], axis=0).T

    cs = by_time(cs_t)
    ecs = by_time(jnp.exp2(cs_t))
    w_state = by_time(w_state_t)

    def expand(a):
        hi = a.astype(BF16)
        lo = (a - hi.astype(F32)).astype(BF16)
        return _dot(jnp.concatenate([hi, lo], axis=1), exp2_ref[...])

    ecs_x = expand(ecs)
    wst_x = expand(w_state)
    yield

    row = lax.broadcasted_iota(jnp.int32, (L, L), 0)
    colm = lax.broadcasted_iota(jnp.int32, (L, L), 1)
    tril = row >= colm
    lane = lax.broadcasted_iota(jnp.int32, (L, LANES), 1)
    lo_half = lane < SSD_HEAD_DIM

    for g in range(SSD_GROUPS):
        gx = g * SSD_GROUP_WIDTH
        bm = conv_ref[:, SSD_INNER + g * SSD_STATE:SSD_INNER + (g + 1) * SSD_STATE]
        cm = conv_ref[:, SSD_INNER + SSD_BC + g * SSD_STATE:SSD_INNER + SSD_BC + (g + 1) * SSD_STATE]
        cm16 = cm.astype(BF16)
        cb = _dot_nt(cm16, bm.astype(BF16))
        xs_g = conv_ref[:, gx:gx + SSD_GROUP_WIDTH]

        y_off = _dot(cm16, state_ref[:, gx:gx + SSD_GROUP_WIDTH].astype(BF16))
        yield
        yacc_ref[:, gx:gx + SSD_GROUP_WIDTH] = (
            y_off * ecs_x[:, gx:gx + SSD_GROUP_WIDTH] + xs_g * dskip_ref[:, gx:gx + SSD_GROUP_WIDTH])

        for pair in range(SSD_HEADS // SSD_GROUPS // 2):
            ms = []
            h0 = g * (SSD_HEADS // SSD_GROUPS) + 2 * pair
            for h in (h0, h0 + 1):
                seg = cs[:, h:h + 1] - csd_t[h:h + 1, :]
                ms.append((cb * jnp.exp2(jnp.where(tril, seg, -jnp.inf))).astype(BF16))
            x_pair = conv_ref[:, gx + pair * LANES:gx + (pair + 1) * LANES]
            x_blk = jnp.concatenate(
                [jnp.where(lo_half, x_pair, 0.0), jnp.where(lo_half, 0.0, x_pair)], axis=0).astype(BF16)
            sl = slice(gx + pair * LANES, gx + (pair + 1) * LANES)
            y_diag = _dot(jnp.concatenate(ms, axis=1), x_blk)
            yield
            yacc_ref[:, sl] = yacc_ref[:, sl] + y_diag

        xd = (xs_g * wst_x[:, gx:gx + SSD_GROUP_WIDTH]).astype(BF16)
        contrib = _dot(bm.T.astype(BF16), xd)
        yield
        state_ref[:, gx:gx + SSD_GROUP_WIDTH] = (
            state_ref[:, gx:gx + SSD_GROUP_WIDTH] * ecs_x[L - 1:L, gx:gx + SSD_GROUP_WIDTH] + contrib)

        z_half = z_ref[:, gx:gx + SSD_GROUP_WIDTH].astype(F32)
        gy = yacc_ref[:, gx:gx + SSD_GROUP_WIDTH] * _silu_of_twice(z_half)
        y_ref[:, gx:gx + SSD_GROUP_WIDTH] = (
            gy * _rms_scale(gy) * nw_ref[:, gx:gx + SSD_GROUP_WIDTH]).astype(y_ref.dtype)


def _ssd(z, xbc, dt_t, cw, cb, dtb, alog, dskip_x, nw, tri3, exp2, shift, bsz, seqlen):
    L = SSD_CHUNK * SSD_CHUNKS_PER_STEP
    nc = seqlen // L
    rb = SSD_ROWS_PER_STEP
    row = lambda w: pl.BlockSpec((rb, L, w), lambda r, c: (r, c, 0))
    dt_spec = pl.BlockSpec((rb, SSD_HEADS, L), lambda r, c: (r, 0, c))
    per_batch = lambda a: a.reshape(bsz, seqlen, a.shape[-1])
    consts = (cw, cb, dtb, alog, dskip_x, nw, tri3, exp2, shift)
    y = pl.pallas_call(
        _ssd_kernel,
        grid=(bsz // rb, nc),
        in_specs=[row(SSD_INNER), row(SSD_CONV_DIM), dt_spec] + [_const_spec(a.shape) for a in consts],
        out_specs=row(SSD_INNER),
        out_shape=jax.ShapeDtypeStruct((bsz, seqlen, SSD_INNER), BF16),
        scratch_shapes=[
            pltpu.VMEM((rb, CONV_HALO + L, SSD_CONV_DIM), BF16),
            pltpu.VMEM((rb, SSD_CHUNK, SSD_CONV_DIM), F32),
            pltpu.VMEM((rb, SSD_STATE, SSD_INNER), F32),
            pltpu.VMEM((rb, SSD_CHUNK, SSD_INNER), F32),
        ],
        compiler_params=pltpu.CompilerParams(
            dimension_semantics=("arbitrary", "arbitrary"), vmem_limit_bytes=VMEM_LIMIT),
        name="ssd",
    )(per_batch(z), per_batch(xbc), dt_t.reshape(SSD_HEADS, bsz, seqlen).transpose(1, 0, 2), *consts)
    return y.reshape(bsz * seqlen, SSD_INNER)


def _attn_kernel(qt_ref, k_ref, vt_ref, lq1_ref, lk1_ref, lq2_ref, lk2_ref, o_ref,
                 qs_ref, acc_ref, m_ref, mx_ref, s_ref):
    for t in range(Q_TILES_PER_STEP):
        _attn_query_tile(pl.program_id(2) * Q_TILES_PER_STEP + t, slice(t * TQ, (t + 1) * TQ),
                         qt_ref, k_ref, vt_ref, lq1_ref, lk1_ref, lq2_ref, lk2_ref, o_ref,
                         qs_ref, acc_ref, m_ref, mx_ref, s_ref)


def _attn_query_tile(i, cols, qt_ref, k_ref, vt_ref, lq1_ref, lk1_ref, lq2_ref, lk2_ref, o_ref,
                     qs_ref, acc_ref, m_ref, mx_ref, s_ref):
    d_idx = lax.broadcasted_iota(jnp.int32, (DA_V_DIM, TQ), 0)
    ones_rows = jnp.ones((ONES_ROWS, TK), BF16)

    for g in range(HEADS_PER_STEP):
        rows = slice(g * DA_V_DIM, (g + 1) * DA_V_DIM)
        q = qt_ref[rows, cols]
        zero = jnp.zeros_like(q)
        qs_ref[rows, 0:TQ] = jnp.where(d_idx < DA_HEAD_DIM, q, zero)
        qs_ref[rows, TQ:2 * TQ] = jnp.where(d_idx < DA_HEAD_DIM, zero, q)

    def scores(g, j, diagonal):
        rows = slice(g * DA_V_DIM, (g + 1) * DA_V_DIM)
        off = pl.multiple_of(j * TK, TK)
        s = _dot(k_ref[pl.ds(off, TK), rows], qs_ref[rows, :])
        if diagonal:
            key = lax.broadcasted_iota(jnp.int32, s.shape, 0)
            qry = lax.broadcasted_iota(jnp.int32, s.shape, 1) & (TQ - 1)
            s = jnp.where(key <= qry, s, -jnp.inf)
        s_ref[g] = s
        mx_ref[g:g + 1, :] = jnp.max(s, axis=0, keepdims=True)

    def softmax_pv(g, j, first):
        rows = slice(g * DA_V_DIM, (g + 1) * DA_V_DIM)
        arow = slice(g * ACC_ROWS, (g + 1) * ACC_ROWS)
        off = pl.multiple_of(j * TK, TK)
        if first:
            m_new = mx_ref[g:g + 1, :]
        else:
            m_prev = m_ref[g:g + 1, :]
            m_new = jnp.maximum(m_prev, mx_ref[g:g + 1, :])
        p = jnp.exp2(s_ref[g] - m_new).astype(BF16)
        v_ext = jnp.concatenate([vt_ref[rows, pl.ds(off, TK)], ones_rows], axis=0)
        pv = _dot(v_ext, p)
        if first:
            acc_ref[arow, :] = pv
        else:
            acc_ref[arow, :] = jnp.exp2(m_prev - m_new) * acc_ref[arow, :] + pv
        m_ref[g:g + 1, :] = m_new

    def run_tiles(tiles, diagonal, next_tile):
        chains = [(t, g) for t in tiles for g in range(HEADS_PER_STEP)]
        for n, (t, g) in enumerate(chains):
            ahead = n + SCORE_LOOKAHEAD
            if ahead < len(chains):
                scores(chains[ahead][1], chains[ahead][0], diagonal)
            else:
                scores(ahead - len(chains), next_tile, False)
            softmax_pv(g, t, first=diagonal)

    for g in range(SCORE_LOOKAHEAD):
        scores(g, i, True)
    run_tiles([i], diagonal=True, next_tile=0)

    def body(jj, carry):
        run_tiles([2 * jj, 2 * jj + 1], diagonal=False, next_tile=2 * jj + 2)
        return carry

    lax.fori_loop(0, i // 2, body, 0)

    @pl.when(i % 2 == 1)
    def _():
        run_tiles([i - 1], diagonal=False, next_tile=i)

    lam = (jnp.exp(jnp.sum(lq1_ref[...] * lk1_ref[...], axis=1, keepdims=True))
           - jnp.exp(jnp.sum(lq2_ref[...] * lk2_ref[...], axis=1, keepdims=True)) + LAMBDA_INIT)
    for g in range(HEADS_PER_STEP):
        a0 = g * ACC_ROWS
        inv = 1.0 / acc_ref[a0 + DA_V_DIM:a0 + DA_V_DIM + 1, :]
        o = (acc_ref[a0:a0 + DA_V_DIM, 0:TQ] * inv[:, 0:TQ]
             - acc_ref[a0:a0 + DA_V_DIM, TQ:2 * TQ] * (lam * inv[:, TQ:2 * TQ]))
        o_ref[cols, g * DA_V_DIM:(g + 1) * DA_V_DIM] = o.T.astype(o_ref.dtype)


def _attention(qt, k, vt, lq1, lk1, lq2, lk2, bsz, seqlen):
    assert TQ == TK
    tq = Q_TILES_PER_STEP * TQ
    nq = seqlen // tq
    gw = HEADS_PER_STEP * DA_V_DIM
    small = (lq1, lk1, lq2, lk2)
    return pl.pallas_call(
        _attn_kernel,
        grid=(bsz, DA_HEADS // HEADS_PER_STEP, nq),
        in_specs=[
            pl.BlockSpec((gw, tq), lambda b, h, i: (h, b * nq + i)),
            pl.BlockSpec((seqlen, gw), lambda b, h, i: (b, h)),
            pl.BlockSpec((gw, seqlen), lambda b, h, i: (h, b)),
        ] + [_const_spec(a.shape) for a in small],
        out_specs=pl.BlockSpec((tq, gw), lambda b, h, i: (b * nq + i, h)),
        out_shape=jax.ShapeDtypeStruct((bsz * seqlen, DA_WIDTH), BF16),
        scratch_shapes=[
            pltpu.VMEM((gw, 2 * TQ), BF16),
            pltpu.VMEM((HEADS_PER_STEP * ACC_ROWS, 2 * TQ), F32),
            pltpu.VMEM((HEADS_PER_STEP, 2 * TQ), F32),
            pltpu.VMEM((HEADS_PER_STEP, 2 * TQ), F32),
            pltpu.VMEM((HEADS_PER_STEP, TK, 2 * TQ), F32),
        ],
        compiler_params=pltpu.CompilerParams(
            dimension_semantics=("arbitrary", "arbitrary", "arbitrary"), vmem_limit_bytes=VMEM_LIMIT),
        name="diff_attn",
    )(qt, k, vt, *small)


def _mlp_kernel(x_ref, ys_ref, ya_ref, sw_ref, wo_ref, nw_ref, wg_ref, wu_ref, wd_ref, fw_ref, o_ref, h_ref):
    gain = sw_ref[...] * (1.0 - LAMBDA_INIT)
    heads = []
    for g in range(DA_HEADS):
        o = ya_ref[:, g * DA_V_DIM:(g + 1) * DA_V_DIM].astype(F32)
        heads.append((o * _rms_scale(o) * gain).astype(BF16))
    ya = jnp.concatenate(heads, axis=1)
    h_ref[...] = x_ref[...] + _dot(ys_ref[...], wo_ref[0:SSD_INNER, :]) + _dot(ya, wo_ref[SSD_INNER:, :])
    h = h_ref[...]
    n2 = (h * _rms_scale(h) * nw_ref[...]).astype(BF16)
    ffn = None
    for f0, f1 in zip(FF_SPLITS[:-1], FF_SPLITS[1:]):
        gate = _dot(n2, wg_ref[:, f0:f1])
        up = _dot(n2, wu_ref[:, f0:f1])
        act = (_silu(gate) * up).astype(BF16)
        down = _dot(act, wd_ref[f0:f1, :])
        ffn = down if ffn is None else ffn + down
    out = h_ref[...] + ffn
    o_ref[...] = out * _rms_scale(out) * fw_ref[...]


def _mlp(x2, ys, ya, sw, wo, nw, wg, wu, wd, fw):
    t = x2.shape[0]
    tm = TM_PROJ
    row = pl.BlockSpec((tm, D_MODEL), lambda i: (i, 0))
    consts = (sw, wo, nw, wg, wu, wd, fw)
    return pl.pallas_call(
        _mlp_kernel,
        grid=(t // tm,),
        in_specs=[row, row, row] + [_const_spec(a.shape) for a in consts],
        out_specs=row,
        out_shape=jax.ShapeDtypeStruct((t, D_MODEL), F32),
        scratch_shapes=[pltpu.VMEM((tm, D_MODEL), F32)],
        compiler_params=pltpu.CompilerParams(
            dimension_semantics=("arbitrary",), vmem_limit_bytes=VMEM_LIMIT),
        name="mlp",
    )(x2, ys, ya, *consts)


def kernel(x, mix_norm_w, w_in, conv_w, conv_b, dt_bias, a_log, d_skip, ssd_norm_w, lam_q1, lam_k1, lam_q2,
           lam_k2, subln_w, w_out, ffn_norm_w, w_gate, w_up, w_down, final_norm_w):
    bsz, seqlen, _ = x.shape
    x2 = x.reshape(bsz * seqlen, D_MODEL)

    (z, xbc, dt_t, k, qt, vt), (wo16, wg16, wu16, wd16) = _in_proj(
        x2, mix_norm_w[0][None, :], w_in[0].T.astype(BF16), (w_out[0], w_gate[0], w_up[0], w_down[0]))

    idx = jnp.arange(SSD_CHUNK)
    tri = (idx[:, None] >= idx[None, :]).astype(BF16)
    tri3 = jnp.concatenate([tri.T, tri.T, tri.T], axis=0)
    sel = (jnp.arange(LANES)[:, None] == (jnp.arange(SSD_INNER)[None, :] // SSD_HEAD_DIM)).astype(BF16)
    exp2 = jnp.concatenate([sel, sel], axis=0)
    src = CONV_HALO + idx[None, :, None] - jnp.arange(1, SSD_CONV)[:, None, None]
    shift = (jnp.arange(CONV_HALO + SSD_CHUNK)[None, None, :] == src).astype(BF16)
    shift = shift.reshape((SSD_CONV - 1) * SSD_CHUNK, CONV_HALO + SSD_CHUNK)
    dskip_x = jnp.repeat(d_skip[0].astype(F32), SSD_HEAD_DIM)[None, :]

    per_head = lambda v: jnp.broadcast_to(v.astype(F32)[:, None], (SSD_HEADS, SSD_CHUNK))
    y_ssd = _ssd(z, xbc, dt_t, conv_w[0], conv_b[0][None, :], per_head(dt_bias[0]), per_head(a_log[0]),
                 dskip_x, ssd_norm_w[0][None, :], tri3, exp2, shift, bsz, seqlen)
    y_da = _attention(qt, k, vt, lam_q1[0][None, :], lam_k1[0][None, :], lam_q2[0][None, :],
                      lam_k2[0][None, :], bsz, seqlen)

    out = _mlp(x2, y_ssd, y_da, subln_w[0][None, :], wo16, ffn_norm_w[0][None, :], wg16, wu16, wd16,
               final_norm_w[None, :])
    return out.reshape(bsz, seqlen, D_MODEL)
```

```python
import math

import jax
import jax.numpy as jnp
from jax import lax
from jax.experimental import pallas as pl
from jax.experimental.pallas import tpu as pltpu

F32 = jnp.float32
BF16 = jnp.bfloat16

EPS = 1e-5
D_MODEL = 1024
SSD_HEADS = 16
SSD_HEAD_DIM = 64
SSD_INNER = SSD_HEADS * SSD_HEAD_DIM
SSD_GROUPS = 2
SSD_GROUP_WIDTH = SSD_INNER // SSD_GROUPS
SSD_STATE = 128
SSD_CONV = 4
SSD_CHUNK = 128
SSD_BC = SSD_GROUPS * SSD_STATE
SSD_CONV_DIM = SSD_INNER + 2 * SSD_BC
DA_HEADS = 8
DA_HEAD_DIM = 64
DA_V_DIM = 2 * DA_HEAD_DIM
DA_WIDTH = DA_HEADS * DA_V_DIM
D_FF = 2816
LAMBDA_INIT = 0.8 - 0.6 * math.exp(-0.3 * 0)

LANES = 128
CONV_HALO = 16
CONV_LANES = 512
VMEM_LIMIT = 56 * 1024 * 1024

TM_PROJ = 512
TM_IN_PROJ = 1024
TQ = 256
TK = 256
MXU_WIDTH = 256
FF_SPLITS = (0, 6 * MXU_WIDTH, D_FF)
HEADS_PER_STEP = 8
Q_TILES_PER_STEP = 4
SSD_ROWS_PER_STEP = 4
SSD_CHUNKS_PER_STEP = 2
SCORE_LOOKAHEAD = 2
ONES_ROWS = 16
ACC_ROWS = DA_V_DIM + ONES_ROWS
LOG2E = math.log2(math.e)


def _const_spec(shape):
    nd = len(shape)
    return pl.BlockSpec(shape, lambda *_: (0,) * nd, pipeline_mode=pl.Buffered(1))


def _rms_scale(xf):
    return lax.rsqrt(jnp.mean(xf * xf, axis=-1, keepdims=True) + EPS)


def _dot(a, b):
    return jnp.dot(a, b, preferred_element_type=F32)


def _dot_nt(a, b):
    return lax.dot_general(a, b, (((1,), (1,)), ((), ())), preferred_element_type=F32)


def _split3(a):
    hi = a.astype(BF16)
    r1 = a - hi.astype(F32)
    mid = r1.astype(BF16)
    lo = (r1 - mid.astype(F32)).astype(BF16)
    return hi, mid, lo


def _silu_of_twice(h):
    return h + h * jnp.tanh(h)


def _silu(x):
    return _silu_of_twice(0.5 * x)


def _in_proj_kernel(x_ref, nw_ref, wt_ref, *refs):
    n_cast = (len(refs) - 6) // 2
    cast_in, (z_ref, xbc_ref, dt_ref, k_ref, qt_ref, vt_ref), cast_out = (
        refs[:n_cast], refs[n_cast:n_cast + 6], refs[n_cast + 6:])
    for src, dst in zip(cast_in, cast_out):
        dst[...] = src[...].astype(BF16)
    xf = x_ref[...]
    xn = (xf * _rms_scale(xf) * nw_ref[...]).astype(BF16)
    r0 = 0
    for ref, width, scale, transposed in (
            (z_ref, SSD_INNER, 0.5, False), (xbc_ref, SSD_CONV_DIM, None, False), (dt_ref, SSD_HEADS, None, True),
            (qt_ref, DA_WIDTH, DA_HEAD_DIM ** -0.5 * LOG2E, True), (k_ref, DA_WIDTH, None, False),
            (vt_ref, DA_WIDTH, None, True)):
        w = wt_ref[r0:r0 + width, :]
        r = _dot_nt(w, xn) if transposed else _dot_nt(xn, w)
        ref[...] = (r if scale is None else r * scale).astype(ref.dtype)
        r0 += width


def _in_proj(x2, nw, wt, to_cast):
    t = x2.shape[0]
    tm = TM_IN_PROJ
    steps = t // tm
    row = lambda w: pl.BlockSpec((tm, w), lambda i: (i, 0))
    col = pl.BlockSpec((DA_WIDTH, tm), lambda i: (0, i))
    cast_specs = [pl.BlockSpec((a.shape[0] // steps, a.shape[1]), lambda i: (i, 0)) for a in to_cast]
    outs = pl.pallas_call(
        _in_proj_kernel,
        grid=(steps,),
        in_specs=[row(D_MODEL), _const_spec(nw.shape), _const_spec(wt.shape)] + cast_specs,
        out_specs=[row(SSD_INNER), row(SSD_CONV_DIM), pl.BlockSpec((SSD_HEADS, tm), lambda i: (0, i)),
                   row(DA_WIDTH), col, col] + cast_specs,
        out_shape=[
            jax.ShapeDtypeStruct((t, SSD_INNER), BF16),
            jax.ShapeDtypeStruct((t, SSD_CONV_DIM), BF16),
            jax.ShapeDtypeStruct((SSD_HEADS, t), F32),
            jax.ShapeDtypeStruct((t, DA_WIDTH), BF16),
            jax.ShapeDtypeStruct((DA_WIDTH, t), BF16),
            jax.ShapeDtypeStruct((DA_WIDTH, t), BF16),
        ] + [jax.ShapeDtypeStruct(a.shape, BF16) for a in to_cast],
        compiler_params=pltpu.CompilerParams(
            dimension_semantics=("arbitrary",), vmem_limit_bytes=VMEM_LIMIT),
        name="in_proj",
    )(x2, nw, wt, *to_cast)
    return outs[:6], outs[6:]


def _ssd_kernel(z_ref, xbc_ref, dt_ref, cw_ref, cb_ref, dtb_ref, alog_ref, dskip_ref, nw_ref,
                tri3_ref, exp2_ref, shift_ref, y_ref, ext_ref, conv_ref, state_ref, yacc_ref):
    L = SSD_CHUNK
    c = pl.program_id(1)

    @pl.when(c == 0)
    def _():
        state_ref[...] = jnp.zeros_like(state_ref)
        ext_ref[:, 0:CONV_HALO, :] = jnp.zeros((ext_ref.shape[0], CONV_HALO, SSD_CONV_DIM), BF16)

    @pl.when(c > 0)
    def _():
        ext_ref[:, 0:CONV_HALO, :] = ext_ref[:, SSD_CHUNKS_PER_STEP * L:SSD_CHUNKS_PER_STEP * L + CONV_HALO, :]

    consts = (cw_ref, cb_ref, dtb_ref, alog_ref, dskip_ref, nw_ref, tri3_ref, exp2_ref, shift_ref)

    def row_chunks(b):
        for n in range(SSD_CHUNKS_PER_STEP):
            t = pl.ds(n * L, L)
            yield from _ssd_chunk(z_ref.at[b, t], xbc_ref.at[b, t], dt_ref.at[b, :, t], *consts, y_ref.at[b, t],
                                  ext_ref.at[b, pl.ds(n * L, CONV_HALO + L)], conv_ref.at[b], state_ref.at[b],
                                  yacc_ref.at[b])

    _round_robin([row_chunks(b) for b in range(z_ref.shape[0])])


def _round_robin(stages):
    stages = list(stages)
    while stages:
        for gen in list(stages):
            if next(gen, StopIteration) is StopIteration:
                stages.remove(gen)


def _ssd_chunk(z_ref, xbc_ref, dt_ref, cw_ref, cb_ref, dtb_ref, alog_ref, dskip_ref, nw_ref,
               tri3_ref, exp2_ref, shift_ref, y_ref, ext_ref, conv_ref, state_ref, yacc_ref):
    L = SSD_CHUNK
    ext_ref[CONV_HALO:CONV_HALO + L, :] = xbc_ref[...]

    for c0 in range(0, SSD_CONV_DIM, CONV_LANES):
        cols = slice(c0, c0 + CONV_LANES)
        e = ext_ref[:, cols]
        cwh = 0.5 * cw_ref[:, cols]
        acc = 0.5 * cb_ref[:, cols] + cwh[SSD_CONV - 1:SSD_CONV, :] * e[CONV_HALO:, :].astype(F32)
        shifted = _dot(shift_ref[...], e)
        for back in range(1, SSD_CONV):
            j = SSD_CONV - 1 - back
            acc = acc + cwh[j:j + 1, :] * shifted[(back - 1) * L:back * L, :]
        conv_ref[:, cols] = _silu_of_twice(acc)

    dtr = dt_ref[...] + dtb_ref[...]
    dtv = jnp.maximum(dtr, 0.0) + jnp.log(1.0 + jnp.exp(-jnp.abs(dtr)))
    adt = dtv * (-jnp.exp(alog_ref[...]))
    cs_t = _dot(jnp.concatenate(_split3(adt), axis=1), tri3_ref[...]) * LOG2E
    yield
    csd_t = cs_t - jnp.log2(dtv)
    w_state_t = dtv * jnp.exp2(cs_t[:, L - 1:L] - cs_t)

    def by_time(a_t):
        return jnp.concatenate([a_t, jnp.zeros((LANES - SSD_HEADS, L), F32)], axis=0).T

    cs = by_time(cs_t)
    ecs = by_time(jnp.exp2(cs_t))
    w_state = by_time(w_state_t)

    def expand(a):
        hi = a.astype(BF16)
        lo = (a - hi.astype(F32)).astype(BF16)
        return _dot(jnp.concatenate([hi, lo], axis=1), exp2_ref[...])

    ecs_x = expand(ecs)
    wst_x = expand(w_state)
    yield

    row = lax.broadcasted_iota(jnp.int32, (L, L), 0)
    colm = lax.broadcasted_iota(jnp.int32, (L, L), 1)
    tril = row >= colm
    lane = lax.broadcasted_iota(jnp.int32, (L, LANES), 1)
    lo_half = lane < SSD_HEAD_DIM

    for g in range(SSD_GROUPS):
        gx = g * SSD_GROUP_WIDTH
        bm = conv_ref[:, SSD_INNER + g * SSD_STATE:SSD_INNER + (g + 1) * SSD_STATE]
        cm = conv_ref[:, SSD_INNER + SSD_BC + g * SSD_STATE:SSD_INNER + SSD_BC + (g + 1) * SSD_STATE]
        cm16 = cm.astype(BF16)
        cb = _dot_nt(cm16, bm.astype(BF16))
        xs_g = conv_ref[:, gx:gx + SSD_GROUP_WIDTH]

        y_off = _dot(cm16, state_ref[:, gx:gx + SSD_GROUP_WIDTH].astype(BF16))
        yield
        yacc_ref[:, gx:gx + SSD_GROUP_WIDTH] = (
            y_off * ecs_x[:, gx:gx + SSD_GROUP_WIDTH] + xs_g * dskip_ref[:, gx:gx + SSD_GROUP_WIDTH])

        for pair in range(SSD_HEADS // SSD_GROUPS // 2):
            ms = []
            h0 = g * (SSD_HEADS // SSD_GROUPS) + 2 * pair
            for h in (h0, h0 + 1):
                seg = cs[:, h:h + 1] - csd_t[h:h + 1, :]
                ms.append((cb * jnp.exp2(jnp.where(tril, seg, -jnp.inf))).astype(BF16))
            x_pair = conv_ref[:, gx + pair * LANES:gx + (pair + 1) * LANES]
            x_blk = jnp.concatenate(
                [jnp.where(lo_half, x_pair, 0.0), jnp.where(lo_half, 0.0, x_pair)], axis=0).astype(BF16)
            sl = slice(gx + pair * LANES, gx + (pair + 1) * LANES)
            y_diag = _dot(jnp.concatenate(ms, axis=1), x_blk)
            yield
            yacc_ref[:, sl] = yacc_ref[:, sl] + y_diag

        xd = (xs_g * wst_x[:, gx:gx + SSD_GROUP_WIDTH]).astype(BF16)
        contrib = _dot(bm.T.astype(BF16), xd)
        yield
        state_ref[:, gx:gx + SSD_GROUP_WIDTH] = (
            state_ref[:, gx:gx + SSD_GROUP_WIDTH] * ecs_x[L - 1:L, gx:gx + SSD_GROUP_WIDTH] + contrib)

        z_half = z_ref[:, gx:gx + SSD_GROUP_WIDTH].astype(F32)
        gy = yacc_ref[:, gx:gx + SSD_GROUP_WIDTH] * _silu_of_twice(z_half)
        y_ref[:, gx:gx + SSD_GROUP_WIDTH] = (
            gy * _rms_scale(gy) * nw_ref[:, gx:gx + SSD_GROUP_WIDTH]).astype(y_ref.dtype)


def _ssd(z, xbc, dt_t, cw, cb, dtb, alog, dskip_x, nw, tri3, exp2, shift, bsz, seqlen):
    L = SSD_CHUNK * SSD_CHUNKS_PER_STEP
    nc = seqlen // L
    rb = SSD_ROWS_PER_STEP
    row = lambda w: pl.BlockSpec((rb, L, w), lambda r, c: (r, c, 0))
    dt_spec = pl.BlockSpec((rb, SSD_HEADS, L), lambda r, c: (r, 0, c))
    per_batch = lambda a: a.reshape(bsz, seqlen, a.shape[-1])
    consts = (cw, cb, dtb, alog, dskip_x, nw, tri3, exp2, shift)
    y = pl.pallas_call(
        _ssd_kernel,
        grid=(bsz // rb, nc),
        in_specs=[row(SSD_INNER), row(SSD_CONV_DIM), dt_spec] + [_const_spec(a.shape) for a in consts],
        out_specs=row(SSD_INNER),
        out_shape=jax.ShapeDtypeStruct((bsz, seqlen, SSD_INNER), BF16),
        scratch_shapes=[
            pltpu.VMEM((rb, CONV_HALO + L, SSD_CONV_DIM), BF16),
            pltpu.VMEM((rb, SSD_CHUNK, SSD_CONV_DIM), F32),
            pltpu.VMEM((rb, SSD_STATE, SSD_INNER), F32),
            pltpu.VMEM((rb, SSD_CHUNK, SSD_INNER), F32),
        ],
        compiler_params=pltpu.CompilerParams(
            dimension_semantics=("arbitrary", "arbitrary"), vmem_limit_bytes=VMEM_LIMIT),
        name="ssd",
    )(per_batch(z), per_batch(xbc), dt_t.reshape(SSD_HEADS, bsz, seqlen).transpose(1, 0, 2), *consts)
    return y.reshape(bsz * seqlen, SSD_INNER)


def _attn_kernel(qt_ref, k_ref, vt_ref, lq1_ref, lk1_ref, lq2_ref, lk2_ref, o_ref,
                 qs_ref, acc_ref, m_ref, mx_ref, s_ref):
    for t in range(Q_TILES_PER_STEP):
        _attn_query_tile(pl.program_id(2) * Q_TILES_PER_STEP + t, slice(t * TQ, (t + 1) * TQ),
                         qt_ref, k_ref, vt_ref, lq1_ref, lk1_ref, lq2_ref, lk2_ref, o_ref,
                         qs_ref, acc_ref, m_ref, mx_ref, s_ref)


def _attn_query_tile(i, cols, qt_ref, k_ref, vt_ref, lq1_ref, lk1_ref, lq2_ref, lk2_ref, o_ref,
                     qs_ref, acc_ref, m_ref, mx_ref, s_ref):
    d_idx = lax.broadcasted_iota(jnp.int32, (DA_V_DIM, TQ), 0)
    ones_rows = jnp.ones((ONES_ROWS, TK), BF16)

    for g in range(HEADS_PER_STEP):
        rows = slice(g * DA_V_DIM, (g + 1) * DA_V_DIM)
        q = qt_ref[rows, cols]
        zero = jnp.zeros_like(q)
        qs_ref[rows, 0:TQ] = jnp.where(d_idx < DA_HEAD_DIM, q, zero)
        qs_ref[rows, TQ:2 * TQ] = jnp.where(d_idx < DA_HEAD_DIM, zero, q)

    def scores(g, j, diagonal):
        rows = slice(g * DA_V_DIM, (g + 1) * DA_V_DIM)
        off = pl.multiple_of(j * TK, TK)
        s = _dot(k_ref[pl.ds(off, TK), rows], qs_ref[rows, :])
        if diagonal:
            key = lax.broadcasted_iota(jnp.int32, s.shape, 0)
            qry = lax.broadcasted_iota(jnp.int32, s.shape, 1) & (TQ - 1)
            s = jnp.where(key <= qry, s, -jnp.inf)
        s_ref[g] = s
        mx_ref[g:g + 1, :] = jnp.max(s, axis=0, keepdims=True)

    def softmax_pv(g, j, first):
        rows = slice(g * DA_V_DIM, (g + 1) * DA_V_DIM)
        arow = slice(g * ACC_ROWS, (g + 1) * ACC_ROWS)
        off = pl.multiple_of(j * TK, TK)
        if first:
            m_new = mx_ref[g:g + 1, :]
        else:
            m_prev = m_ref[g:g + 1, :]
            m_new = jnp.maximum(m_prev, mx_ref[g:g + 1, :])
        p = jnp.exp2(s_ref[g] - m_new).astype(BF16)
        v_ext = jnp.concatenate([vt_ref[rows, pl.ds(off, TK)], ones_rows], axis=0)
        pv = _dot(v_ext, p)
        if first:
            acc_ref[arow, :] = pv
        else:
            acc_ref[arow, :] = jnp.exp2(m_prev - m_new) * acc_ref[arow, :] + pv
        m_ref[g:g + 1, :] = m_new

    def run_tiles(tiles, diagonal, next_tile):
        chains = [(t, g) for t in tiles for g in range(HEADS_PER_STEP)]
        for n, (t, g) in enumerate(chains):
            ahead = n + SCORE_LOOKAHEAD
            if ahead < len(chains):
                scores(chains[ahead][1], chains[ahead][0], diagonal)
            else:
                scores(ahead - len(chains), next_tile, False)
            softmax_pv(g, t, first=diagonal)

    for g in range(SCORE_LOOKAHEAD):
        scores(g, i, True)
    run_tiles([i], diagonal=True, next_tile=0)

    def body(jj, carry):
        run_tiles([2 * jj, 2 * jj + 1], diagonal=False, next_tile=2 * jj + 2)
        return carry

    lax.fori_loop(0, i // 2, body, 0)

    @pl.when(i % 2 == 1)
    def _():
        run_tiles([i - 1], diagonal=False, next_tile=i)

    lam = (jnp.exp(jnp.sum(lq1_ref[...] * lk1_ref[...], axis=1, keepdims=True))
           - jnp.exp(jnp.sum(lq2_ref[...] * lk2_ref[...], axis=1, keepdims=True)) + LAMBDA_INIT)
    for g in range(HEADS_PER_STEP):
        a0 = g * ACC_ROWS
        inv = 1.0 / acc_ref[a0 + DA_V_DIM:a0 + DA_V_DIM + 1, :]
        o = (acc_ref[a0:a0 + DA_V_DIM, 0:TQ] * inv[:, 0:TQ]
             - acc_ref[a0:a0 + DA_V_DIM, TQ:2 * TQ] * (lam * inv[:, TQ:2 * TQ]))
        o_ref[cols, g * DA_V_DIM:(g + 1) * DA_V_DIM] = o.T.astype(o_ref.dtype)


def _attention(qt, k, vt, lq1, lk1, lq2, lk2, bsz, seqlen):
    assert TQ == TK
    tq = Q_TILES_PER_STEP * TQ
    nq = seqlen // tq
    gw = HEADS_PER_STEP * DA_V_DIM
    small = (lq1, lk1, lq2, lk2)
    return pl.pallas_call(
        _attn_kernel,
        grid=(bsz, DA_HEADS // HEADS_PER_STEP, nq),
        in_specs=[
            pl.BlockSpec((gw, tq), lambda b, h, i: (h, b * nq + i)),
            pl.BlockSpec((seqlen, gw), lambda b, h, i: (b, h)),
            pl.BlockSpec((gw, seqlen), lambda b, h, i: (h, b)),
        ] + [_const_spec(a.shape) for a in small],
        out_specs=pl.BlockSpec((tq, gw), lambda b, h, i: (b * nq + i, h)),
        out_shape=jax.ShapeDtypeStruct((bsz * seqlen, DA_WIDTH), BF16),
        scratch_shapes=[
            pltpu.VMEM((gw, 2 * TQ), BF16),
            pltpu.VMEM((HEADS_PER_STEP * ACC_ROWS, 2 * TQ), F32),
            pltpu.VMEM((HEADS_PER_STEP, 2 * TQ), F32),
            pltpu.VMEM((HEADS_PER_STEP, 2 * TQ), F32),
            pltpu.VMEM((HEADS_PER_STEP, TK, 2 * TQ), F32),
        ],
        compiler_params=pltpu.CompilerParams(
            dimension_semantics=("arbitrary", "arbitrary", "arbitrary"), vmem_limit_bytes=VMEM_LIMIT),
        name="diff_attn",
    )(qt, k, vt, *small)


def _mlp_kernel(x_ref, ys_ref, ya_ref, sw_ref, wo_ref, nw_ref, wg_ref, wu_ref, wd_ref, fw_ref, o_ref, h_ref):
    gain = sw_ref[...] * (1.0 - LAMBDA_INIT)
    heads = []
    for g in range(DA_HEADS):
        o = ya_ref[:, g * DA_V_DIM:(g + 1) * DA_V_DIM].astype(F32)
        heads.append((o * _rms_scale(o) * gain).astype(BF16))
    ya = jnp.concatenate(heads, axis=1)
    h_ref[...] = x_ref[...] + _dot(ys_ref[...], wo_ref[0:SSD_INNER, :]) + _dot(ya, wo_ref[SSD_INNER:, :])
    h = h_ref[...]
    n2 = (h * _rms_scale(h) * nw_ref[...]).astype(BF16)
    ffn = None
    for f0, f1 in zip(FF_SPLITS[:-1], FF_SPLITS[1:]):
        gate = _dot(n2, wg_ref[:, f0:f1])
        up = _dot(n2, wu_ref[:, f0:f1])
        act = (_silu(gate) * up).astype(BF16)
        down = _dot(act, wd_ref[f0:f1, :])
        ffn = down if ffn is None else ffn + down
    out = h_ref[...] + ffn
    o_ref[...] = out * _rms_scale(out) * fw_ref[...]


def _mlp(x2, ys, ya, sw, wo, nw, wg, wu, wd, fw):
    t = x2.shape[0]
    tm = TM_PROJ
    row = pl.BlockSpec((tm, D_MODEL), lambda i: (i, 0))
    consts = (sw, wo, nw, wg, wu, wd, fw)
    return pl.pallas_call(
        _mlp_kernel,
        grid=(t // tm,),
        in_specs=[row, row, row] + [_const_spec(a.shape) for a in consts],
        out_specs=row,
        out_shape=jax.ShapeDtypeStruct((t, D_MODEL), F32),
        scratch_shapes=[pltpu.VMEM((tm, D_MODEL), F32)],
        compiler_params=pltpu.CompilerParams(
            dimension_semantics=("arbitrary",), vmem_limit_bytes=VMEM_LIMIT),
        name="mlp",
    )(x2, ys, ya, *consts)


def kernel(x, mix_norm_w, w_in, conv_w, conv_b, dt_bias, a_log, d_skip, ssd_norm_w, lam_q1, lam_k1, lam_q2,
           lam_k2, subln_w, w_out, ffn_norm_w, w_gate, w_up, w_down, final_norm_w):
    bsz, seqlen, _ = x.shape
    x2 = x.reshape(bsz * seqlen, D_MODEL)

    (z, xbc, dt_t, k, qt, vt), (wo16, wg16, wu16, wd16) = _in_proj(
        x2, mix_norm_w[0][None, :], w_in[0].T.astype(BF16), (w_out[0], w_gate[0], w_up[0], w_down[0]))

    idx = jnp.arange(SSD_CHUNK)
    tri = (idx[:, None] >= idx[None, :]).astype(BF16)
    tri3 = jnp.concatenate([tri.T, tri.T, tri.T], axis=0)
    sel = (jnp.arange(LANES)[:, None] == (jnp.arange(SSD_INNER)[None, :] // SSD_HEAD_DIM)).astype(BF16)
    exp2 = jnp.concatenate([sel, sel], axis=0)
    src = CONV_HALO + idx[None, :, None] - jnp.arange(1, SSD_CONV)[:, None, None]
    shift = (jnp.arange(CONV_HALO + SSD_CHUNK)[None, None, :] == src).astype(BF16)
    shift = shift.reshape((SSD_CONV - 1) * SSD_CHUNK, CONV_HALO + SSD_CHUNK)
    dskip_x = jnp.repeat(d_skip[0].astype(F32), SSD_HEAD_DIM)[None, :]

    per_head = lambda v: jnp.broadcast_to(v.astype(F32)[:, None], (SSD_HEADS, SSD_CHUNK))
    y_ssd = _ssd(z, xbc, dt_t, conv_w[0], conv_b[0][None, :], per_head(dt_bias[0]), per_head(a_log[0]),
                 dskip_x, ssd_norm_w[0][None, :], tri3, exp2, shift, bsz, seqlen)
    y_da = _attention(qt, k, vt, lam_q1[0][None, :], lam_k1[0][None, :], lam_q2[0][None, :],
                      lam_k2[0][None, :], bsz, seqlen)

    out = _mlp(x2, y_ssd, y_da, subln_w[0][None, :], wo16, ffn_norm_w[0][None, :], wg16, wu16, wd16,
               final_norm_w[None, :])
    return out.reshape(bsz, seqlen, D_MODEL)
```

```python
import math

import jax
import jax.numpy as jnp
from jax import lax
from jax.experimental import pallas as pl
from jax.experimental.pallas import tpu as pltpu

F32 = jnp.float32
BF16 = jnp.bfloat16

EPS = 1e-5
D_MODEL = 1024
SSD_HEADS = 16
SSD_HEAD_DIM = 64
SSD_INNER = SSD_HEADS * SSD_HEAD_DIM
SSD_GROUPS = 2
SSD_GROUP_WIDTH = SSD_INNER // SSD_GROUPS
SSD_STATE = 128
SSD_CONV = 4
SSD_CHUNK = 128
SSD_BC = SSD_GROUPS * SSD_STATE
SSD_CONV_DIM = SSD_INNER + 2 * SSD_BC
DA_HEADS = 8
DA_HEAD_DIM = 64
DA_V_DIM = 2 * DA_HEAD_DIM
DA_WIDTH = DA_HEADS * DA_V_DIM
D_FF = 2816
LAMBDA_INIT = 0.8 - 0.6 * math.exp(-0.3 * 0)

LANES = 128
CONV_HALO = 16
CONV_LANES = 512
VMEM_LIMIT = 56 * 1024 * 1024

TM_PROJ = 512
TM_IN_PROJ = 1024
TQ = 256
TK = 256
MXU_WIDTH = 256
FF_SPLITS = (0, 6 * MXU_WIDTH, D_FF)
HEADS_PER_STEP = 8
Q_TILES_PER_STEP = 2
KEY_TILES_PER_TRIP = 3
SSD_ROWS_PER_STEP = 4
SSD_CHUNKS_PER_STEP = 2
SCORE_LOOKAHEAD = 2
ONES_ROWS = 16
ACC_ROWS = DA_V_DIM + ONES_ROWS
LOG2E = math.log2(math.e)


def _const_spec(shape):
    nd = len(shape)
    return pl.BlockSpec(shape, lambda *_: (0,) * nd, pipeline_mode=pl.Buffered(1))


def _rms_scale(xf):
    return lax.rsqrt(jnp.mean(xf * xf, axis=-1, keepdims=True) + EPS)


def _dot(a, b):
    return jnp.dot(a, b, preferred_element_type=F32)


def _dot_nt(a, b):
    return lax.dot_general(a, b, (((1,), (1,)), ((), ())), preferred_element_type=F32)


def _split3(a):
    hi = a.astype(BF16)
    r1 = a - hi.astype(F32)
    mid = r1.astype(BF16)
    lo = (r1 - mid.astype(F32)).astype(BF16)
    return hi, mid, lo


def _silu_of_twice(h):
    return h + h * jnp.tanh(h)


def _silu(x):
    return _silu_of_twice(0.5 * x)


def _in_proj_kernel(x_ref, nw_ref, wt_ref, *refs):
    n_cast = (len(refs) - 6) // 2
    cast_in, (z_ref, xbc_ref, dt_ref, k_ref, qt_ref, vt_ref), cast_out = (
        refs[:n_cast], refs[n_cast:n_cast + 6], refs[n_cast + 6:])
    for src, dst in zip(cast_in, cast_out):
        dst[...] = src[...].astype(BF16)
    xf = x_ref[...]
    xn = (xf * _rms_scale(xf) * nw_ref[...]).astype(BF16)
    r0 = 0
    for ref, width, scale, transposed in (
            (z_ref, SSD_INNER, 0.5, False), (xbc_ref, SSD_CONV_DIM, None, False), (dt_ref, SSD_HEADS, None, True),
            (qt_ref, DA_WIDTH, DA_HEAD_DIM ** -0.5 * LOG2E, True), (k_ref, DA_WIDTH, None, False),
            (vt_ref, DA_WIDTH, None, True)):
        w = wt_ref[r0:r0 + width, :]
        r = _dot_nt(w, xn) if transposed else _dot_nt(xn, w)
        ref[...] = (r if scale is None else r * scale).astype(ref.dtype)
        r0 += width


def _in_proj(x2, nw, wt, to_cast):
    t = x2.shape[0]
    tm = TM_IN_PROJ
    steps = t // tm
    row = lambda w: pl.BlockSpec((tm, w), lambda i: (i, 0))
    col = pl.BlockSpec((DA_WIDTH, tm), lambda i: (0, i))
    cast_specs = [pl.BlockSpec((a.shape[0] // steps, a.shape[1]), lambda i: (i, 0)) for a in to_cast]
    outs = pl.pallas_call(
        _in_proj_kernel,
        grid=(steps,),
        in_specs=[row(D_MODEL), _const_spec(nw.shape), _const_spec(wt.shape)] + cast_specs,
        out_specs=[row(SSD_INNER), row(SSD_CONV_DIM), pl.BlockSpec((SSD_HEADS, tm), lambda i: (0, i)),
                   row(DA_WIDTH), col, col] + cast_specs,
        out_shape=[
            jax.ShapeDtypeStruct((t, SSD_INNER), BF16),
            jax.ShapeDtypeStruct((t, SSD_CONV_DIM), BF16),
            jax.ShapeDtypeStruct((SSD_HEADS, t), F32),
            jax.ShapeDtypeStruct((t, DA_WIDTH), BF16),
            jax.ShapeDtypeStruct((DA_WIDTH, t), BF16),
            jax.ShapeDtypeStruct((DA_WIDTH, t), BF16),
        ] + [jax.ShapeDtypeStruct(a.shape, BF16) for a in to_cast],
        compiler_params=pltpu.CompilerParams(
            dimension_semantics=("arbitrary",), vmem_limit_bytes=VMEM_LIMIT),
        name="in_proj",
    )(x2, nw, wt, *to_cast)
    return outs[:6], outs[6:]


def _ssd_kernel(z_ref, xbc_ref, dt_ref, cw_ref, cb_ref, dtb_ref, alog_ref, dskip_ref, nw_ref,
                tri3_ref, exp2_ref, shift_ref, y_ref, ext_ref, conv_ref, state_ref, yacc_ref):
    L = SSD_CHUNK
    c = pl.program_id(1)

    @pl.when(c == 0)
    def _():
        state_ref[...] = jnp.zeros_like(state_ref)
        ext_ref[:, 0:CONV_HALO, :] = jnp.zeros((ext_ref.shape[0], CONV_HALO, SSD_CONV_DIM), BF16)

    @pl.when(c > 0)
    def _():
        ext_ref[:, 0:CONV_HALO, :] = ext_ref[:, SSD_CHUNKS_PER_STEP * L:SSD_CHUNKS_PER_STEP * L + CONV_HALO, :]

    consts = (cw_ref, cb_ref, dtb_ref, alog_ref, dskip_ref, nw_ref, tri3_ref, exp2_ref, shift_ref)

    def row_chunks(b):
        for n in range(SSD_CHUNKS_PER_STEP):
            t = pl.ds(n * L, L)
            yield from _ssd_chunk(z_ref.at[b, t], xbc_ref.at[b, t], dt_ref.at[b, :, t], *consts, y_ref.at[b, t],
                                  ext_ref.at[b, pl.ds(n * L, CONV_HALO + L)], conv_ref.at[b], state_ref.at[b],
                                  yacc_ref.at[b])

    _round_robin([row_chunks(b) for b in range(z_ref.shape[0])])


def _round_robin(stages):
    stages = list(stages)
    while stages:
        for gen in list(stages):
            if next(gen, StopIteration) is StopIteration:
                stages.remove(gen)


def _ssd_chunk(z_ref, xbc_ref, dt_ref, cw_ref, cb_ref, dtb_ref, alog_ref, dskip_ref, nw_ref,
               tri3_ref, exp2_ref, shift_ref, y_ref, ext_ref, conv_ref, state_ref, yacc_ref):
    L = SSD_CHUNK
    ext_ref[CONV_HALO:CONV_HALO + L, :] = xbc_ref[...]

    for c0 in range(0, SSD_CONV_DIM, CONV_LANES):
        cols = slice(c0, c0 + CONV_LANES)
        e = ext_ref[:, cols]
        cwh = 0.5 * cw_ref[:, cols]
        acc = 0.5 * cb_ref[:, cols] + cwh[SSD_CONV - 1:SSD_CONV, :] * e[CONV_HALO:, :].astype(F32)
        shifted = _dot(shift_ref[...], e)
        for back in range(1, SSD_CONV):
            j = SSD_CONV - 1 - back
            acc = acc + cwh[j:j + 1, :] * shifted[(back - 1) * L:back * L, :]
        conv_ref[:, cols] = _silu_of_twice(acc)

    dtr = dt_ref[...] + dtb_ref[...]
    dtv = jnp.maximum(dtr, 0.0) + jnp.log(1.0 + jnp.exp(-jnp.abs(dtr)))
    adt = dtv * (-jnp.exp(alog_ref[...]))
    cs_t = _dot(jnp.concatenate(_split3(adt), axis=1), tri3_ref[...]) * LOG2E
    yield
    csd_t = cs_t - jnp.log2(dtv)
    w_state_t = dtv * jnp.exp2(cs_t[:, L - 1:L] - cs_t)

    def by_time(a_t):
        return jnp.concatenate([a_t, jnp.zeros((LANES - SSD_HEADS, L), F32)], axis=0).T

    cs = by_time(cs_t)
    ecs = by_time(jnp.exp2(cs_t))
    w_state = by_time(w_state_t)

    def expand(a):
        hi = a.astype(BF16)
        lo = (a - hi.astype(F32)).astype(BF16)
        return _dot(jnp.concatenate([hi, lo], axis=1), exp2_ref[...])

    ecs_x = expand(ecs)
    wst_x = expand(w_state)
    yield

    row = lax.broadcasted_iota(jnp.int32, (L, L), 0)
    colm = lax.broadcasted_iota(jnp.int32, (L, L), 1)
    tril = row >= colm
    lane = lax.broadcasted_iota(jnp.int32, (L, LANES), 1)
    lo_half = lane < SSD_HEAD_DIM

    for g in range(SSD_GROUPS):
        gx = g * SSD_GROUP_WIDTH
        bm = conv_ref[:, SSD_INNER + g * SSD_STATE:SSD_INNER + (g + 1) * SSD_STATE]
        cm = conv_ref[:, SSD_INNER + SSD_BC + g * SSD_STATE:SSD_INNER + SSD_BC + (g + 1) * SSD_STATE]
        cm16 = cm.astype(BF16)
        cb = _dot_nt(cm16, bm.astype(BF16))
        xs_g = conv_ref[:, gx:gx + SSD_GROUP_WIDTH]

        y_off = _dot(cm16, state_ref[:, gx:gx + SSD_GROUP_WIDTH].astype(BF16))
        yield
        yacc_ref[:, gx:gx + SSD_GROUP_WIDTH] = (
            y_off * ecs_x[:, gx:gx + SSD_GROUP_WIDTH] + xs_g * dskip_ref[:, gx:gx + SSD_GROUP_WIDTH])

        for pair in range(SSD_HEADS // SSD_GROUPS // 2):
            ms = []
            h0 = g * (SSD_HEADS // SSD_GROUPS) + 2 * pair
            for h in (h0, h0 + 1):
                seg = cs[:, h:h + 1] - csd_t[h:h + 1, :]
                ms.append((cb * jnp.exp2(jnp.where(tril, seg, -jnp.inf))).astype(BF16))
            x_pair = conv_ref[:, gx + pair * LANES:gx + (pair + 1) * LANES]
            x_blk = jnp.concatenate(
                [jnp.where(lo_half, x_pair, 0.0), jnp.where(lo_half, 0.0, x_pair)], axis=0).astype(BF16)
            sl = slice(gx + pair * LANES, gx + (pair + 1) * LANES)
            y_diag = _dot(jnp.concatenate(ms, axis=1), x_blk)
            yield
            yacc_ref[:, sl] = yacc_ref[:, sl] + y_diag

        xd = (xs_g * wst_x[:, gx:gx + SSD_GROUP_WIDTH]).astype(BF16)
        contrib = _dot(bm.T.astype(BF16), xd)
        yield
        state_ref[:, gx:gx + SSD_GROUP_WIDTH] = (
            state_ref[:, gx:gx + SSD_GROUP_WIDTH] * ecs_x[L - 1:L, gx:gx + SSD_GROUP_WIDTH] + contrib)

        z_half = z_ref[:, gx:gx + SSD_GROUP_WIDTH].astype(F32)
        gy = yacc_ref[:, gx:gx + SSD_GROUP_WIDTH] * _silu_of_twice(z_half)
        y_ref[:, gx:gx + SSD_GROUP_WIDTH] = (
            gy * _rms_scale(gy) * nw_ref[:, gx:gx + SSD_GROUP_WIDTH]).astype(y_ref.dtype)


def _ssd(z, xbc, dt_t, cw, cb, dtb, alog, dskip_x, nw, tri3, exp2, shift, bsz, seqlen):
    L = SSD_CHUNK * SSD_CHUNKS_PER_STEP
    nc = seqlen // L
    rb = SSD_ROWS_PER_STEP
    row = lambda w: pl.BlockSpec((rb, L, w), lambda r, c: (r, c, 0))
    dt_spec = pl.BlockSpec((rb, SSD_HEADS, L), lambda r, c: (r, 0, c))
    per_batch = lambda a: a.reshape(bsz, seqlen, a.shape[-1])
    consts = (cw, cb, dtb, alog, dskip_x, nw, tri3, exp2, shift)
    y = pl.pallas_call(
        _ssd_kernel,
        grid=(bsz // rb, nc),
        in_specs=[row(SSD_INNER), row(SSD_CONV_DIM), dt_spec] + [_const_spec(a.shape) for a in consts],
        out_specs=row(SSD_INNER),
        out_shape=jax.ShapeDtypeStruct((bsz, seqlen, SSD_INNER), BF16),
        scratch_shapes=[
            pltpu.VMEM((rb, CONV_HALO + L, SSD_CONV_DIM), BF16),
            pltpu.VMEM((rb, SSD_CHUNK, SSD_CONV_DIM), F32),
            pltpu.VMEM((rb, SSD_STATE, SSD_INNER), F32),
            pltpu.VMEM((rb, SSD_CHUNK, SSD_INNER), F32),
        ],
        compiler_params=pltpu.CompilerParams(
            dimension_semantics=("arbitrary", "arbitrary"), vmem_limit_bytes=VMEM_LIMIT),
        name="ssd",
    )(per_batch(z), per_batch(xbc), dt_t.reshape(SSD_HEADS, bsz, seqlen).transpose(1, 0, 2), *consts)
    return y.reshape(bsz * seqlen, SSD_INNER)


def _attn_kernel(qt_ref, k_ref, vt_ref, lq1_ref, lk1_ref, lq2_ref, lk2_ref, o_ref,
                 qs_ref, acc_ref, m_ref, mx_ref, s_ref):
    for t in range(Q_TILES_PER_STEP):
        _attn_query_tile(pl.program_id(2) * Q_TILES_PER_STEP + t, slice(t * TQ, (t + 1) * TQ),
                         qt_ref, k_ref, vt_ref, lq1_ref, lk1_ref, lq2_ref, lk2_ref, o_ref,
                         qs_ref, acc_ref, m_ref, mx_ref, s_ref)


def _attn_query_tile(i, cols, qt_ref, k_ref, vt_ref, lq1_ref, lk1_ref, lq2_ref, lk2_ref, o_ref,
                     qs_ref, acc_ref, m_ref, mx_ref, s_ref):
    d_idx = lax.broadcasted_iota(jnp.int32, (DA_V_DIM, TQ), 0)
    ones_rows = jnp.ones((ONES_ROWS, TK), BF16)

    for g in range(HEADS_PER_STEP):
        rows = slice(g * DA_V_DIM, (g + 1) * DA_V_DIM)
        q = qt_ref[rows, cols]
        zero = jnp.zeros_like(q)
        qs_ref[rows, 0:TQ] = jnp.where(d_idx < DA_HEAD_DIM, q, zero)
        qs_ref[rows, TQ:2 * TQ] = jnp.where(d_idx < DA_HEAD_DIM, zero, q)

    def scores(g, j, diagonal):
        rows = slice(g * DA_V_DIM, (g + 1) * DA_V_DIM)
        off = pl.multiple_of(j * TK, TK)
        s = _dot(k_ref[pl.ds(off, TK), rows], qs_ref[rows, :])
        if diagonal:
            key = lax.broadcasted_iota(jnp.int32, s.shape, 0)
            qry = lax.broadcasted_iota(jnp.int32, s.shape, 1) & (TQ - 1)
            s = jnp.where(key <= qry, s, -jnp.inf)
        s_ref[g] = s
        mx_ref[g:g + 1, :] = jnp.max(s, axis=0, keepdims=True)

    def softmax_pv(g, j, first):
        rows = slice(g * DA_V_DIM, (g + 1) * DA_V_DIM)
        arow = slice(g * ACC_ROWS, (g + 1) * ACC_ROWS)
        off = pl.multiple_of(j * TK, TK)
        if first:
            m_new = mx_ref[g:g + 1, :]
        else:
            m_prev = m_ref[g:g + 1, :]
            m_new = jnp.maximum(m_prev, mx_ref[g:g + 1, :])
        p = jnp.exp2(s_ref[g] - m_new).astype(BF16)
        v_ext = jnp.concatenate([vt_ref[rows, pl.ds(off, TK)], ones_rows], axis=0)
        pv = _dot(v_ext, p)
        if first:
            acc_ref[arow, :] = pv
        else:
            acc_ref[arow, :] = jnp.exp2(m_prev - m_new) * acc_ref[arow, :] + pv
        m_ref[g:g + 1, :] = m_new

    def run_tiles(tiles, diagonal, next_tile):
        chains = [(t, g) for t in tiles for g in range(HEADS_PER_STEP)]
        for n, (t, g) in enumerate(chains):
            ahead = n + SCORE_LOOKAHEAD
            if ahead < len(chains):
                scores(chains[ahead][1], chains[ahead][0], diagonal)
            else:
                scores(ahead - len(chains), next_tile, False)
            softmax_pv(g, t, first=diagonal)

    for g in range(SCORE_LOOKAHEAD):
        scores(g, i, True)
    run_tiles([i], diagonal=True, next_tile=0)

    def body(jj, carry):
        t0 = KEY_TILES_PER_TRIP * jj
        run_tiles([t0 + n for n in range(KEY_TILES_PER_TRIP)], diagonal=False, next_tile=t0 + KEY_TILES_PER_TRIP)
        return carry

    lax.fori_loop(0, i // KEY_TILES_PER_TRIP, body, 0)

    for rest in range(1, KEY_TILES_PER_TRIP):
        @pl.when(i % KEY_TILES_PER_TRIP == rest)
        def _(rest=rest):
            run_tiles([i - rest + n for n in range(rest)], diagonal=False, next_tile=i)

    lam = (jnp.exp(jnp.sum(lq1_ref[...] * lk1_ref[...], axis=1, keepdims=True))
           - jnp.exp(jnp.sum(lq2_ref[...] * lk2_ref[...], axis=1, keepdims=True)) + LAMBDA_INIT)
    for g in range(HEADS_PER_STEP):
        a0 = g * ACC_ROWS
        inv = 1.0 / acc_ref[a0 + DA_V_DIM:a0 + DA_V_DIM + 1, :]
        o = (acc_ref[a0:a0 + DA_V_DIM, 0:TQ] * inv[:, 0:TQ]
             - acc_ref[a0:a0 + DA_V_DIM, TQ:2 * TQ] * (lam * inv[:, TQ:2 * TQ]))
        o_ref[cols, g * DA_V_DIM:(g + 1) * DA_V_DIM] = o.T.astype(o_ref.dtype)


def _attention(qt, k, vt, lq1, lk1, lq2, lk2, bsz, seqlen):
    assert TQ == TK
    tq = Q_TILES_PER_STEP * TQ
    nq = seqlen // tq
    gw = HEADS_PER_STEP * DA_V_DIM
    small = (lq1, lk1, lq2, lk2)
    return pl.pallas_call(
        _attn_kernel,
        grid=(bsz, DA_HEADS // HEADS_PER_STEP, nq),
        in_specs=[
            pl.BlockSpec((gw, tq), lambda b, h, i: (h, b * nq + i)),
            pl.BlockSpec((seqlen, gw), lambda b, h, i: (b, h)),
            pl.BlockSpec((gw, seqlen), lambda b, h, i: (h, b)),
        ] + [_const_spec(a.shape) for a in small],
        out_specs=pl.BlockSpec((tq, gw), lambda b, h, i: (b * nq + i, h)),
        out_shape=jax.ShapeDtypeStruct((bsz * seqlen, DA_WIDTH), BF16),
        scratch_shapes=[
            pltpu.VMEM((gw, 2 * TQ), BF16),
            pltpu.VMEM((HEADS_PER_STEP * ACC_ROWS, 2 * TQ), F32),
            pltpu.VMEM((HEADS_PER_STEP, 2 * TQ), F32),
            pltpu.VMEM((HEADS_PER_STEP, 2 * TQ), F32),
            pltpu.VMEM((HEADS_PER_STEP, TK, 2 * TQ), F32),
        ],
        compiler_params=pltpu.CompilerParams(
            dimension_semantics=("arbitrary", "arbitrary", "arbitrary"), vmem_limit_bytes=VMEM_LIMIT),
        name="diff_attn",
    )(qt, k, vt, *small)


def _mlp_kernel(x_ref, ys_ref, ya_ref, sw_ref, wo_ref, nw_ref, wg_ref, wu_ref, wd_ref, fw_ref, o_ref, h_ref):
    gain = sw_ref[...] * (1.0 - LAMBDA_INIT)
    heads = []
    for g in range(DA_HEADS):
        o = ya_ref[:, g * DA_V_DIM:(g + 1) * DA_V_DIM].astype(F32)
        heads.append((o * _rms_scale(o) * gain).astype(BF16))
    ya = jnp.concatenate(heads, axis=1)
    h_ref[...] = x_ref[...] + _dot(ys_ref[...], wo_ref[0:SSD_INNER, :]) + _dot(ya, wo_ref[SSD_INNER:, :])
    h = h_ref[...]
    n2 = (h * _rms_scale(h) * nw_ref[...]).astype(BF16)
    ffn = None
    for f0, f1 in zip(FF_SPLITS[:-1], FF_SPLITS[1:]):
        gate = _dot(n2, wg_ref[:, f0:f1])
        up = _dot(n2, wu_ref[:, f0:f1])
        act = (_silu(gate) * up).astype(BF16)
        down = _dot(act, wd_ref[f0:f1, :])
        ffn = down if ffn is None else ffn + down
    out = h_ref[...] + ffn
    o_ref[...] = out * _rms_scale(out) * fw_ref[...]


def _mlp(x2, ys, ya, sw, wo, nw, wg, wu, wd, fw):
    t = x2.shape[0]
    tm = TM_PROJ
    row = pl.BlockSpec((tm, D_MODEL), lambda i: (i, 0))
    consts = (sw, wo, nw, wg, wu, wd, fw)
    return pl.pallas_call(
        _mlp_kernel,
        grid=(t // tm,),
        in_specs=[row, row, row] + [_const_spec(a.shape) for a in consts],
        out_specs=row,
        out_shape=jax.ShapeDtypeStruct((t, D_MODEL), F32),
        scratch_shapes=[pltpu.VMEM((tm, D_MODEL), F32)],
        compiler_params=pltpu.CompilerParams(
            dimension_semantics=("arbitrary",), vmem_limit_bytes=VMEM_LIMIT),
        name="mlp",
    )(x2, ys, ya, *consts)


def kernel(x, mix_norm_w, w_in, conv_w, conv_b, dt_bias, a_log, d_skip, ssd_norm_w, lam_q1, lam_k1, lam_q2,
           lam_k2, subln_w, w_out, ffn_norm_w, w_gate, w_up, w_down, final_norm_w):
    bsz, seqlen, _ = x.shape
    x2 = x.reshape(bsz * seqlen, D_MODEL)

    (z, xbc, dt_t, k, qt, vt), (wo16, wg16, wu16, wd16) = _in_proj(
        x2, mix_norm_w[0][None, :], w_in[0].T.astype(BF16), (w_out[0], w_gate[0], w_up[0], w_down[0]))

    idx = jnp.arange(SSD_CHUNK)
    tri = (idx[:, None] >= idx[None, :]).astype(BF16)
    tri3 = jnp.concatenate([tri.T, tri.T, tri.T], axis=0)
    sel = (jnp.arange(LANES)[:, None] == (jnp.arange(SSD_INNER)[None, :] // SSD_HEAD_DIM)).astype(BF16)
    exp2 = jnp.concatenate([sel, sel], axis=0)
    src = CONV_HALO + idx[None, :, None] - jnp.arange(1, SSD_CONV)[:, None, None]
    shift = (jnp.arange(CONV_HALO + SSD_CHUNK)[None, None, :] == src).astype(BF16)
    shift = shift.reshape((SSD_CONV - 1) * SSD_CHUNK, CONV_HALO + SSD_CHUNK)
    dskip_x = jnp.repeat(d_skip[0].astype(F32), SSD_HEAD_DIM)[None, :]

    per_head = lambda v: jnp.broadcast_to(v.astype(F32)[:, None], (SSD_HEADS, SSD_CHUNK))
    y_ssd = _ssd(z, xbc, dt_t, conv_w[0], conv_b[0][None, :], per_head(dt_bias[0]), per_head(a_log[0]),
                 dskip_x, ssd_norm_w[0][None, :], tri3, exp2, shift, bsz, seqlen)
    y_da = _attention(qt, k, vt, lam_q1[0][None, :], lam_k1[0][None, :], lam_q2[0][None, :],
                      lam_k2[0][None, :], bsz, seqlen)

    out = _mlp(x2, y_ssd, y_da, subln_w[0][None, :], wo16, ffn_norm_w[0][None, :], wg16, wu16, wd16,
               final_norm_w[None, :])
    return out.reshape(bsz, seqlen, D_MODEL)
```

```python
import math

import jax
import jax.numpy as jnp
from jax import lax
from jax.experimental import pallas as pl
from jax.experimental.pallas import tpu as pltpu

F32 = jnp.float32
BF16 = jnp.bfloat16

EPS = 1e-5
D_MODEL = 1024
SSD_HEADS = 16
SSD_HEAD_DIM = 64
SSD_INNER = SSD_HEADS * SSD_HEAD_DIM
SSD_GROUPS = 2
SSD_GROUP_WIDTH = SSD_INNER // SSD_GROUPS
SSD_STATE = 128
SSD_CONV = 4
SSD_CHUNK = 128
SSD_BC = SSD_GROUPS * SSD_STATE
SSD_CONV_DIM = SSD_INNER + 2 * SSD_BC
DA_HEADS = 8
DA_HEAD_DIM = 64
DA_V_DIM = 2 * DA_HEAD_DIM
DA_WIDTH = DA_HEADS * DA_V_DIM
D_FF = 2816
LAMBDA_INIT = 0.8 - 0.6 * math.exp(-0.3 * 0)

LANES = 128
CONV_HALO = 16
CONV_LANES = 512
VMEM_LIMIT = 56 * 1024 * 1024

TM_PROJ = 512
TM_IN_PROJ = 1024
TQ = 256
TK = 256
MXU_WIDTH = 256
FF_SPLITS = (0, 6 * MXU_WIDTH, D_FF)
HEADS_PER_STEP = 8
Q_TILES_PER_STEP = 2
KEY_TILES_PER_TRIP = 4
SSD_ROWS_PER_STEP = 4
SSD_CHUNKS_PER_STEP = 2
SCORE_LOOKAHEAD = 2
ONES_ROWS = 16
ACC_ROWS = DA_V_DIM + ONES_ROWS
LOG2E = math.log2(math.e)


def _const_spec(shape):
    nd = len(shape)
    return pl.BlockSpec(shape, lambda *_: (0,) * nd, pipeline_mode=pl.Buffered(1))


def _rms_scale(xf):
    return lax.rsqrt(jnp.mean(xf * xf, axis=-1, keepdims=True) + EPS)


def _dot(a, b):
    return jnp.dot(a, b, preferred_element_type=F32)


def _dot_nt(a, b):
    return lax.dot_general(a, b, (((1,), (1,)), ((), ())), preferred_element_type=F32)


def _split3(a):
    hi = a.astype(BF16)
    r1 = a - hi.astype(F32)
    mid = r1.astype(BF16)
    lo = (r1 - mid.astype(F32)).astype(BF16)
    return hi, mid, lo


def _silu_of_twice(h):
    return h + h * jnp.tanh(h)


def _silu(x):
    return _silu_of_twice(0.5 * x)


def _in_proj_kernel(x_ref, nw_ref, wt_ref, *refs):
    n_cast = (len(refs) - 6) // 2
    cast_in, (z_ref, xbc_ref, dt_ref, k_ref, qt_ref, vt_ref), cast_out = (
        refs[:n_cast], refs[n_cast:n_cast + 6], refs[n_cast + 6:])
    for src, dst in zip(cast_in, cast_out):
        dst[...] = src[...].astype(BF16)
    xf = x_ref[...]
    xn = (xf * _rms_scale(xf) * nw_ref[...]).astype(BF16)
    r0 = 0
    for ref, width, scale, transposed in (
            (z_ref, SSD_INNER, 0.5, False), (xbc_ref, SSD_CONV_DIM, None, False), (dt_ref, SSD_HEADS, None, True),
            (qt_ref, DA_WIDTH, DA_HEAD_DIM ** -0.5 * LOG2E, True), (k_ref, DA_WIDTH, None, False),
            (vt_ref, DA_WIDTH, None, True)):
        w = wt_ref[r0:r0 + width, :]
        r = _dot_nt(w, xn) if transposed else _dot_nt(xn, w)
        ref[...] = (r if scale is None else r * scale).astype(ref.dtype)
        r0 += width


def _in_proj(x2, nw, wt, to_cast):
    t = x2.shape[0]
    tm = TM_IN_PROJ
    steps = t // tm
    row = lambda w: pl.BlockSpec((tm, w), lambda i: (i, 0))
    col = pl.BlockSpec((DA_WIDTH, tm), lambda i: (0, i))
    cast_specs = [pl.BlockSpec((a.shape[0] // steps, a.shape[1]), lambda i: (i, 0)) for a in to_cast]
    outs = pl.pallas_call(
        _in_proj_kernel,
        grid=(steps,),
        in_specs=[row(D_MODEL), _const_spec(nw.shape), _const_spec(wt.shape)] + cast_specs,
        out_specs=[row(SSD_INNER), row(SSD_CONV_DIM), pl.BlockSpec((SSD_HEADS, tm), lambda i: (0, i)),
                   row(DA_WIDTH), col, col] + cast_specs,
        out_shape=[
            jax.ShapeDtypeStruct((t, SSD_INNER), BF16),
            jax.ShapeDtypeStruct((t, SSD_CONV_DIM), BF16),
            jax.ShapeDtypeStruct((SSD_HEADS, t), F32),
            jax.ShapeDtypeStruct((t, DA_WIDTH), BF16),
            jax.ShapeDtypeStruct((DA_WIDTH, t), BF16),
            jax.ShapeDtypeStruct((DA_WIDTH, t), BF16),
        ] + [jax.ShapeDtypeStruct(a.shape, BF16) for a in to_cast],
        compiler_params=pltpu.CompilerParams(
            dimension_semantics=("arbitrary",), vmem_limit_bytes=VMEM_LIMIT),
        name="in_proj",
    )(x2, nw, wt, *to_cast)
    return outs[:6], outs[6:]


def _ssd_kernel(z_ref, xbc_ref, dt_ref, cw_ref, cb_ref, dtb_ref, alog_ref, dskip_ref, nw_ref,
                tri3_ref, exp2_ref, shift_ref, y_ref, ext_ref, conv_ref, state_ref, yacc_ref):
    L = SSD_CHUNK
    c = pl.program_id(1)

    @pl.when(c == 0)
    def _():
        state_ref[...] = jnp.zeros_like(state_ref)
        ext_ref[:, 0:CONV_HALO, :] = jnp.zeros((ext_ref.shape[0], CONV_HALO, SSD_CONV_DIM), BF16)

    @pl.when(c > 0)
    def _():
        ext_ref[:, 0:CONV_HALO, :] = ext_ref[:, SSD_CHUNKS_PER_STEP * L:SSD_CHUNKS_PER_STEP * L + CONV_HALO, :]

    consts = (cw_ref, cb_ref, dtb_ref, alog_ref, dskip_ref, nw_ref, tri3_ref, exp2_ref, shift_ref)

    def row_chunks(b):
        for n in range(SSD_CHUNKS_PER_STEP):
            t = pl.ds(n * L, L)
            yield from _ssd_chunk(z_ref.at[b, t], xbc_ref.at[b, t], dt_ref.at[b, :, t], *consts, y_ref.at[b, t],
                                  ext_ref.at[b, pl.ds(n * L, CONV_HALO + L)], conv_ref.at[b], state_ref.at[b],
                                  yacc_ref.at[b])

    _round_robin([row_chunks(b) for b in range(z_ref.shape[0])])


def _round_robin(stages):
    stages = list(stages)
    while stages:
        for gen in list(stages):
            if next(gen, StopIteration) is StopIteration:
                stages.remove(gen)


def _ssd_chunk(z_ref, xbc_ref, dt_ref, cw_ref, cb_ref, dtb_ref, alog_ref, dskip_ref, nw_ref,
               tri3_ref, exp2_ref, shift_ref, y_ref, ext_ref, conv_ref, state_ref, yacc_ref):
    L = SSD_CHUNK
    ext_ref[CONV_HALO:CONV_HALO + L, :] = xbc_ref[...]

    for c0 in range(0, SSD_CONV_DIM, CONV_LANES):
        cols = slice(c0, c0 + CONV_LANES)
        e = ext_ref[:, cols]
        cwh = 0.5 * cw_ref[:, cols]
        acc = 0.5 * cb_ref[:, cols] + cwh[SSD_CONV - 1:SSD_CONV, :] * e[CONV_HALO:, :].astype(F32)
        shifted = _dot(shift_ref[...], e)
        for back in range(1, SSD_CONV):
            j = SSD_CONV - 1 - back
            acc = acc + cwh[j:j + 1, :] * shifted[(back - 1) * L:back * L, :]
        conv_ref[:, cols] = _silu_of_twice(acc)

    dtr = dt_ref[...] + dtb_ref[...]
    dtv = jnp.maximum(dtr, 0.0) + jnp.log(1.0 + jnp.exp(-jnp.abs(dtr)))
    adt = dtv * (-jnp.exp(alog_ref[...]))
    cs_t = _dot(jnp.concatenate(_split3(adt), axis=1), tri3_ref[...]) * LOG2E
    yield
    csd_t = cs_t - jnp.log2(dtv)
    w_state_t = dtv * jnp.exp2(cs_t[:, L - 1:L] - cs_t)

    def by_time(a_t):
        return jnp.concatenate([a_t, jnp.zeros((LANES - SSD_HEADS, L), F32)], axis=0).T

    cs = by_time(cs_t)
    ecs = by_time(jnp.exp2(cs_t))
    w_state = by_time(w_state_t)

    def expand(a):
        hi = a.astype(BF16)
        lo = (a - hi.astype(F32)).astype(BF16)
        return _dot(jnp.concatenate([hi, lo], axis=1), exp2_ref[...])

    ecs_x = expand(ecs)
    wst_x = expand(w_state)
    yield

    row = lax.broadcasted_iota(jnp.int32, (L, L), 0)
    colm = lax.broadcasted_iota(jnp.int32, (L, L), 1)
    tril = row >= colm
    lane = lax.broadcasted_iota(jnp.int32, (L, LANES), 1)
    lo_half = lane < SSD_HEAD_DIM

    for g in range(SSD_GROUPS):
        gx = g * SSD_GROUP_WIDTH
        bm = conv_ref[:, SSD_INNER + g * SSD_STATE:SSD_INNER + (g + 1) * SSD_STATE]
        cm = conv_ref[:, SSD_INNER + SSD_BC + g * SSD_STATE:SSD_INNER + SSD_BC + (g + 1) * SSD_STATE]
        cm16 = cm.astype(BF16)
        cb = _dot_nt(cm16, bm.astype(BF16))
        xs_g = conv_ref[:, gx:gx + SSD_GROUP_WIDTH]

        y_off = _dot(cm16, state_ref[:, gx:gx + SSD_GROUP_WIDTH].astype(BF16))
        yield
        yacc_ref[:, gx:gx + SSD_GROUP_WIDTH] = (
            y_off * ecs_x[:, gx:gx + SSD_GROUP_WIDTH] + xs_g * dskip_ref[:, gx:gx + SSD_GROUP_WIDTH])

        for pair in range(SSD_HEADS // SSD_GROUPS // 2):
            ms = []
            h0 = g * (SSD_HEADS // SSD_GROUPS) + 2 * pair
            for h in (h0, h0 + 1):
                seg = cs[:, h:h + 1] - csd_t[h:h + 1, :]
                ms.append((cb * jnp.exp2(jnp.where(tril, seg, -jnp.inf))).astype(BF16))
            x_pair = conv_ref[:, gx + pair * LANES:gx + (pair + 1) * LANES]
            x_blk = jnp.concatenate(
                [jnp.where(lo_half, x_pair, 0.0), jnp.where(lo_half, 0.0, x_pair)], axis=0).astype(BF16)
            sl = slice(gx + pair * LANES, gx + (pair + 1) * LANES)
            y_diag = _dot(jnp.concatenate(ms, axis=1), x_blk)
            yield
            yacc_ref[:, sl] = yacc_ref[:, sl] + y_diag

        xd = (xs_g * wst_x[:, gx:gx + SSD_GROUP_WIDTH]).astype(BF16)
        contrib = _dot(bm.T.astype(BF16), xd)
        yield
        state_ref[:, gx:gx + SSD_GROUP_WIDTH] = (
            state_ref[:, gx:gx + SSD_GROUP_WIDTH] * ecs_x[L - 1:L, gx:gx + SSD_GROUP_WIDTH] + contrib)

        z_half = z_ref[:, gx:gx + SSD_GROUP_WIDTH].astype(F32)
        gy = yacc_ref[:, gx:gx + SSD_GROUP_WIDTH] * _silu_of_twice(z_half)
        y_ref[:, gx:gx + SSD_GROUP_WIDTH] = (
            gy * _rms_scale(gy) * nw_ref[:, gx:gx + SSD_GROUP_WIDTH]).astype(y_ref.dtype)


def _ssd(z, xbc, dt_t, cw, cb, dtb, alog, dskip_x, nw, tri3, exp2, shift, bsz, seqlen):
    L = SSD_CHUNK * SSD_CHUNKS_PER_STEP
    nc = seqlen // L
    rb = SSD_ROWS_PER_STEP
    row = lambda w: pl.BlockSpec((rb, L, w), lambda r, c: (r, c, 0))
    dt_spec = pl.BlockSpec((rb, SSD_HEADS, L), lambda r, c: (r, 0, c))
    per_batch = lambda a: a.reshape(bsz, seqlen, a.shape[-1])
    consts = (cw, cb, dtb, alog, dskip_x, nw, tri3, exp2, shift)
    y = pl.pallas_call(
        _ssd_kernel,
        grid=(bsz // rb, nc),
        in_specs=[row(SSD_INNER), row(SSD_CONV_DIM), dt_spec] + [_const_spec(a.shape) for a in consts],
        out_specs=row(SSD_INNER),
        out_shape=jax.ShapeDtypeStruct((bsz, seqlen, SSD_INNER), BF16),
        scratch_shapes=[
            pltpu.VMEM((rb, CONV_HALO + L, SSD_CONV_DIM), BF16),
            pltpu.VMEM((rb, SSD_CHUNK, SSD_CONV_DIM), F32),
            pltpu.VMEM((rb, SSD_STATE, SSD_INNER), F32),
            pltpu.VMEM((rb, SSD_CHUNK, SSD_INNER), F32),
        ],
        compiler_params=pltpu.CompilerParams(
            dimension_semantics=("arbitrary", "arbitrary"), vmem_limit_bytes=VMEM_LIMIT),
        name="ssd",
    )(per_batch(z), per_batch(xbc), dt_t.reshape(SSD_HEADS, bsz, seqlen).transpose(1, 0, 2), *consts)
    return y.reshape(bsz * seqlen, SSD_INNER)


def _attn_kernel(qt_ref, k_ref, vt_ref, lq1_ref, lk1_ref, lq2_ref, lk2_ref, o_ref,
                 qs_ref, acc_ref, m_ref, mx_ref, s_ref):
    for t in range(Q_TILES_PER_STEP):
        _attn_query_tile(pl.program_id(2) * Q_TILES_PER_STEP + t, slice(t * TQ, (t + 1) * TQ),
                         qt_ref, k_ref, vt_ref, lq1_ref, lk1_ref, lq2_ref, lk2_ref, o_ref,
                         qs_ref, acc_ref, m_ref, mx_ref, s_ref)


def _attn_query_tile(i, cols, qt_ref, k_ref, vt_ref, lq1_ref, lk1_ref, lq2_ref, lk2_ref, o_ref,
                     qs_ref, acc_ref, m_ref, mx_ref, s_ref):
    d_idx = lax.broadcasted_iota(jnp.int32, (DA_V_DIM, TQ), 0)
    ones_rows = jnp.ones((ONES_ROWS, TK), BF16)

    for g in range(HEADS_PER_STEP):
        rows = slice(g * DA_V_DIM, (g + 1) * DA_V_DIM)
        q = qt_ref[rows, cols]
        zero = jnp.zeros_like(q)
        qs_ref[rows, 0:TQ] = jnp.where(d_idx < DA_HEAD_DIM, q, zero)
        qs_ref[rows, TQ:2 * TQ] = jnp.where(d_idx < DA_HEAD_DIM, zero, q)

    def scores(g, j, diagonal):
        rows = slice(g * DA_V_DIM, (g + 1) * DA_V_DIM)
        off = pl.multiple_of(j * TK, TK)
        s = _dot(k_ref[pl.ds(off, TK), rows], qs_ref[rows, :])
        if diagonal:
            key = lax.broadcasted_iota(jnp.int32, s.shape, 0)
            qry = lax.broadcasted_iota(jnp.int32, s.shape, 1) & (TQ - 1)
            s = jnp.where(key <= qry, s, -jnp.inf)
        s_ref[g] = s
        mx_ref[g:g + 1, :] = jnp.max(s, axis=0, keepdims=True)

    def softmax_pv(g, j, first):
        rows = slice(g * DA_V_DIM, (g + 1) * DA_V_DIM)
        arow = slice(g * ACC_ROWS, (g + 1) * ACC_ROWS)
        off = pl.multiple_of(j * TK, TK)
        if first:
            m_new = mx_ref[g:g + 1, :]
        else:
            m_prev = m_ref[g:g + 1, :]
            m_new = jnp.maximum(m_prev, mx_ref[g:g + 1, :])
        p = jnp.exp2(s_ref[g] - m_new).astype(BF16)
        v_ext = jnp.concatenate([vt_ref[rows, pl.ds(off, TK)], ones_rows], axis=0)
        pv = _dot(v_ext, p)
        if first:
            acc_ref[arow, :] = pv
        else:
            acc_ref[arow, :] = jnp.exp2(m_prev - m_new) * acc_ref[arow, :] + pv
        m_ref[g:g + 1, :] = m_new

    def run_tiles(tiles, diagonal, next_tile):
        chains = [(t, g) for t in tiles for g in range(HEADS_PER_STEP)]
        for n, (t, g) in enumerate(chains):
            ahead = n + SCORE_LOOKAHEAD
            if ahead < len(chains):
                scores(chains[ahead][1], chains[ahead][0], diagonal)
            else:
                scores(ahead - len(chains), next_tile, False)
            softmax_pv(g, t, first=diagonal)

    for g in range(SCORE_LOOKAHEAD):
        scores(g, i, True)
    run_tiles([i], diagonal=True, next_tile=0)

    def body(jj, carry):
        t0 = KEY_TILES_PER_TRIP * jj
        run_tiles([t0 + n for n in range(KEY_TILES_PER_TRIP)], diagonal=False, next_tile=t0 + KEY_TILES_PER_TRIP)
        return carry

    lax.fori_loop(0, i // KEY_TILES_PER_TRIP, body, 0)

    for rest in range(1, KEY_TILES_PER_TRIP):
        @pl.when(i % KEY_TILES_PER_TRIP == rest)
        def _(rest=rest):
            run_tiles([i - rest + n for n in range(rest)], diagonal=False, next_tile=i)

    lam = (jnp.exp(jnp.sum(lq1_ref[...] * lk1_ref[...], axis=1, keepdims=True))
           - jnp.exp(jnp.sum(lq2_ref[...] * lk2_ref[...], axis=1, keepdims=True)) + LAMBDA_INIT)
    for g in range(HEADS_PER_STEP):
        a0 = g * ACC_ROWS
        inv = 1.0 / acc_ref[a0 + DA_V_DIM:a0 + DA_V_DIM + 1, :]
        o = (acc_ref[a0:a0 + DA_V_DIM, 0:TQ] * inv[:, 0:TQ]
             - acc_ref[a0:a0 + DA_V_DIM, TQ:2 * TQ] * (lam * inv[:, TQ:2 * TQ]))
        o_ref[cols, g * DA_V_DIM:(g + 1) * DA_V_DIM] = o.T.astype(o_ref.dtype)


def _attention(qt, k, vt, lq1, lk1, lq2, lk2, bsz, seqlen):
    assert TQ == TK
    tq = Q_TILES_PER_STEP * TQ
    nq = seqlen // tq
    gw = HEADS_PER_STEP * DA_V_DIM
    small = (lq1, lk1, lq2, lk2)
    return pl.pallas_call(
        _attn_kernel,
        grid=(bsz, DA_HEADS // HEADS_PER_STEP, nq),
        in_specs=[
            pl.BlockSpec((gw, tq), lambda b, h, i: (h, b * nq + i)),
            pl.BlockSpec((seqlen, gw), lambda b, h, i: (b, h)),
            pl.BlockSpec((gw, seqlen), lambda b, h, i: (h, b)),
        ] + [_const_spec(a.shape) for a in small],
        out_specs=pl.BlockSpec((tq, gw), lambda b, h, i: (b * nq + i, h)),
        out_shape=jax.ShapeDtypeStruct((bsz * seqlen, DA_WIDTH), BF16),
        scratch_shapes=[
            pltpu.VMEM((gw, 2 * TQ), BF16),
            pltpu.VMEM((HEADS_PER_STEP * ACC_ROWS, 2 * TQ), F32),
            pltpu.VMEM((HEADS_PER_STEP, 2 * TQ), F32),
            pltpu.VMEM((HEADS_PER_STEP, 2 * TQ), F32),
            pltpu.VMEM((HEADS_PER_STEP, TK, 2 * TQ), F32),
        ],
        compiler_params=pltpu.CompilerParams(
            dimension_semantics=("arbitrary", "arbitrary", "arbitrary"), vmem_limit_bytes=VMEM_LIMIT),
        name="diff_attn",
    )(qt, k, vt, *small)


def _mlp_kernel(x_ref, ys_ref, ya_ref, sw_ref, wo_ref, nw_ref, wg_ref, wu_ref, wd_ref, fw_ref, o_ref, h_ref):
    gain = sw_ref[...] * (1.0 - LAMBDA_INIT)
    heads = []
    for g in range(DA_HEADS):
        o = ya_ref[:, g * DA_V_DIM:(g + 1) * DA_V_DIM].astype(F32)
        heads.append((o * _rms_scale(o) * gain).astype(BF16))
    ya = jnp.concatenate(heads, axis=1)
    h_ref[...] = x_ref[...] + _dot(ys_ref[...], wo_ref[0:SSD_INNER, :]) + _dot(ya, wo_ref[SSD_INNER:, :])
    h = h_ref[...]
    n2 = (h * _rms_scale(h) * nw_ref[...]).astype(BF16)
    ffn = None
    for f0, f1 in zip(FF_SPLITS[:-1], FF_SPLITS[1:]):
        gate = _dot(n2, wg_ref[:, f0:f1])
        up = _dot(n2, wu_ref[:, f0:f1])
        act = (_silu(gate) * up).astype(BF16)
        down = _dot(act, wd_ref[f0:f1, :])
        ffn = down if ffn is None else ffn + down
    out = h_ref[...] + ffn
    o_ref[...] = out * _rms_scale(out) * fw_ref[...]


def _mlp(x2, ys, ya, sw, wo, nw, wg, wu, wd, fw):
    t = x2.shape[0]
    tm = TM_PROJ
    row = pl.BlockSpec((tm, D_MODEL), lambda i: (i, 0))
    consts = (sw, wo, nw, wg, wu, wd, fw)
    return pl.pallas_call(
        _mlp_kernel,
        grid=(t // tm,),
        in_specs=[row, row, row] + [_const_spec(a.shape) for a in consts],
        out_specs=row,
        out_shape=jax.ShapeDtypeStruct((t, D_MODEL), F32),
        scratch_shapes=[pltpu.VMEM((tm, D_MODEL), F32)],
        compiler_params=pltpu.CompilerParams(
            dimension_semantics=("arbitrary",), vmem_limit_bytes=VMEM_LIMIT),
        name="mlp",
    )(x2, ys, ya, *consts)


def kernel(x, mix_norm_w, w_in, conv_w, conv_b, dt_bias, a_log, d_skip, ssd_norm_w, lam_q1, lam_k1, lam_q2,
           lam_k2, subln_w, w_out, ffn_norm_w, w_gate, w_up, w_down, final_norm_w):
    bsz, seqlen, _ = x.shape
    x2 = x.reshape(bsz * seqlen, D_MODEL)

    (z, xbc, dt_t, k, qt, vt), (wo16, wg16, wu16, wd16) = _in_proj(
        x2, mix_norm_w[0][None, :], w_in[0].T.astype(BF16), (w_out[0], w_gate[0], w_up[0], w_down[0]))

    idx = jnp.arange(SSD_CHUNK)
    tri = (idx[:, None] >= idx[None, :]).astype(BF16)
    tri3 = jnp.concatenate([tri.T, tri.T, tri.T], axis=0)
    sel = (jnp.arange(LANES)[:, None] == (jnp.arange(SSD_INNER)[None, :] // SSD_HEAD_DIM)).astype(BF16)
    exp2 = jnp.concatenate([sel, sel], axis=0)
    src = CONV_HALO + idx[None, :, None] - jnp.arange(1, SSD_CONV)[:, None, None]
    shift = (jnp.arange(CONV_HALO + SSD_CHUNK)[None, None, :] == src).astype(BF16)
    shift = shift.reshape((SSD_CONV - 1) * SSD_CHUNK, CONV_HALO + SSD_CHUNK)
    dskip_x = jnp.repeat(d_skip[0].astype(F32), SSD_HEAD_DIM)[None, :]

    per_head = lambda v: jnp.broadcast_to(v.astype(F32)[:, None], (SSD_HEADS, SSD_CHUNK))
    y_ssd = _ssd(z, xbc, dt_t, conv_w[0], conv_b[0][None, :], per_head(dt_bias[0]), per_head(a_log[0]),
                 dskip_x, ssd_norm_w[0][None, :], tri3, exp2, shift, bsz, seqlen)
    y_da = _attention(qt, k, vt, lam_q1[0][None, :], lam_k1[0][None, :], lam_q2[0][None, :],
                      lam_k2[0][None, :], bsz, seqlen)

    out = _mlp(x2, y_ssd, y_da, subln_w[0][None, :], wo16, ffn_norm_w[0][None, :], wg16, wu16, wd16,
               final_norm_w[None, :])
    return out.reshape(bsz, seqlen, D_MODEL)
```

```python
import math

import jax
import jax.numpy as jnp
from jax import lax
from jax.experimental import pallas as pl
from jax.experimental.pallas import tpu as pltpu

F32 = jnp.float32
BF16 = jnp.bfloat16

EPS = 1e-5
D_MODEL = 1024
SSD_HEADS = 16
SSD_HEAD_DIM = 64
SSD_INNER = SSD_HEADS * SSD_HEAD_DIM
SSD_GROUPS = 2
SSD_GROUP_WIDTH = SSD_INNER // SSD_GROUPS
SSD_STATE = 128
SSD_CONV = 4
SSD_CHUNK = 128
SSD_BC = SSD_GROUPS * SSD_STATE
SSD_CONV_DIM = SSD_INNER + 2 * SSD_BC
DA_HEADS = 8
DA_HEAD_DIM = 64
DA_V_DIM = 2 * DA_HEAD_DIM
DA_WIDTH = DA_HEADS * DA_V_DIM
D_FF = 2816
LAMBDA_INIT = 0.8 - 0.6 * math.exp(-0.3 * 0)

LANES = 128
CONV_HALO = 16
CONV_LANES = 512
VMEM_LIMIT = 56 * 1024 * 1024

TM_PROJ = 512
TM_IN_PROJ = 1024
TQ = 256
TK = 256
MXU_WIDTH = 256
FF_SPLITS = (0, 6 * MXU_WIDTH, D_FF)
HEADS_PER_STEP = 8
Q_TILES_PER_STEP = 2
KEY_TILES_PER_TRIP = 3
SSD_ROWS_PER_STEP = 4
SSD_CHUNKS_PER_STEP = 2
SCORE_LOOKAHEAD = 2
ONES_ROWS = 16
ACC_ROWS = DA_V_DIM + ONES_ROWS
LOG2E = math.log2(math.e)


def _const_spec(shape):
    nd = len(shape)
    return pl.BlockSpec(shape, lambda *_: (0,) * nd, pipeline_mode=pl.Buffered(1))


def _rms_scale(xf):
    return lax.rsqrt(jnp.mean(xf * xf, axis=-1, keepdims=True) + EPS)


def _dot(a, b):
    return jnp.dot(a, b, preferred_element_type=F32)


def _dot_nt(a, b):
    return lax.dot_general(a, b, (((1,), (1,)), ((), ())), preferred_element_type=F32)


def _split3(a):
    hi = a.astype(BF16)
    r1 = a - hi.astype(F32)
    mid = r1.astype(BF16)
    lo = (r1 - mid.astype(F32)).astype(BF16)
    return hi, mid, lo


def _silu_of_twice(h):
    return h + h * jnp.tanh(h)


def _silu(x):
    return _silu_of_twice(0.5 * x)


def _in_proj_kernel(x_ref, nw_ref, wt_ref, *refs):
    n_cast = (len(refs) - 6) // 2
    cast_in, (z_ref, xbc_ref, dt_ref, k_ref, qt_ref, vt_ref), cast_out = (
        refs[:n_cast], refs[n_cast:n_cast + 6], refs[n_cast + 6:])
    for src, dst in zip(cast_in, cast_out):
        dst[...] = src[...].astype(BF16)
    xf = x_ref[...]
    xn = (xf * _rms_scale(xf) * nw_ref[...]).astype(BF16)
    r0 = 0
    for ref, width, scale, transposed in (
            (z_ref, SSD_INNER, 0.5, False), (xbc_ref, SSD_CONV_DIM, None, False), (dt_ref, SSD_HEADS, None, True),
            (qt_ref, DA_WIDTH, DA_HEAD_DIM ** -0.5 * LOG2E, True), (k_ref, DA_WIDTH, None, False),
            (vt_ref, DA_WIDTH, None, True)):
        w = wt_ref[r0:r0 + width, :]
        r = _dot_nt(w, xn) if transposed else _dot_nt(xn, w)
        ref[...] = (r if scale is None else r * scale).astype(ref.dtype)
        r0 += width


def _in_proj(x2, nw, wt, to_cast):
    t = x2.shape[0]
    tm = TM_IN_PROJ
    steps = t // tm
    row = lambda w: pl.BlockSpec((tm, w), lambda i: (i, 0))
    col = pl.BlockSpec((DA_WIDTH, tm), lambda i: (0, i))
    cast_specs = [pl.BlockSpec((a.shape[0] // steps, a.shape[1]), lambda i: (i, 0)) for a in to_cast]
    outs = pl.pallas_call(
        _in_proj_kernel,
        grid=(steps,),
        in_specs=[row(D_MODEL), _const_spec(nw.shape), _const_spec(wt.shape)] + cast_specs,
        out_specs=[row(SSD_INNER), row(SSD_CONV_DIM), pl.BlockSpec((SSD_HEADS, tm), lambda i: (0, i)),
                   row(DA_WIDTH), col, col] + cast_specs,
        out_shape=[
            jax.ShapeDtypeStruct((t, SSD_INNER), BF16),
            jax.ShapeDtypeStruct((t, SSD_CONV_DIM), BF16),
            jax.ShapeDtypeStruct((SSD_HEADS, t), F32),
            jax.ShapeDtypeStruct((t, DA_WIDTH), BF16),
            jax.ShapeDtypeStruct((DA_WIDTH, t), BF16),
            jax.ShapeDtypeStruct((DA_WIDTH, t), BF16),
        ] + [jax.ShapeDtypeStruct(a.shape, BF16) for a in to_cast],
        compiler_params=pltpu.CompilerParams(
            dimension_semantics=("arbitrary",), vmem_limit_bytes=VMEM_LIMIT),
        name="in_proj",
    )(x2, nw, wt, *to_cast)
    return outs[:6], outs[6:]


def _ssd_kernel(z_ref, xbc_ref, dt_ref, cw_ref, cb_ref, dtb_ref, alog_ref, dskip_ref, nw_ref,
                tri3_ref, exp2_ref, shift_ref, y_ref, ext_ref, conv_ref, state_ref, yacc_ref):
    L = SSD_CHUNK
    c = pl.program_id(1)

    @pl.when(c == 0)
    def _():
        state_ref[...] = jnp.zeros_like(state_ref)
        ext_ref[:, 0:CONV_HALO, :] = jnp.zeros((ext_ref.shape[0], CONV_HALO, SSD_CONV_DIM), BF16)

    @pl.when(c > 0)
    def _():
        ext_ref[:, 0:CONV_HALO, :] = ext_ref[:, SSD_CHUNKS_PER_STEP * L:SSD_CHUNKS_PER_STEP * L + CONV_HALO, :]

    consts = (cw_ref, cb_ref, dtb_ref, alog_ref, dskip_ref, nw_ref, tri3_ref, exp2_ref, shift_ref)

    def row_chunks(b):
        for n in range(SSD_CHUNKS_PER_STEP):
            t = pl.ds(n * L, L)
            yield from _ssd_chunk(z_ref.at[b, t], xbc_ref.at[b, t], dt_ref.at[b, :, t], *consts, y_ref.at[b, t],
                                  ext_ref.at[b, pl.ds(n * L, CONV_HALO + L)], conv_ref.at[b], state_ref.at[b],
                                  yacc_ref.at[b])

    _round_robin([row_chunks(b) for b in range(z_ref.shape[0])])


def _round_robin(stages):
    stages = list(stages)
    while stages:
        for gen in list(stages):
            if next(gen, StopIteration) is StopIteration:
                stages.remove(gen)


def _ssd_chunk(z_ref, xbc_ref, dt_ref, cw_ref, cb_ref, dtb_ref, alog_ref, dskip_ref, nw_ref,
               tri3_ref, exp2_ref, shift_ref, y_ref, ext_ref, conv_ref, state_ref, yacc_ref):
    L = SSD_CHUNK
    ext_ref[CONV_HALO:CONV_HALO + L, :] = xbc_ref[...]

    for c0 in range(0, SSD_CONV_DIM, CONV_LANES):
        cols = slice(c0, c0 + CONV_LANES)
        e = ext_ref[:, cols]
        cwh = 0.5 * cw_ref[:, cols]
        acc = 0.5 * cb_ref[:, cols] + cwh[SSD_CONV - 1:SSD_CONV, :] * e[CONV_HALO:, :].astype(F32)
        shifted = _dot(shift_ref[...], e)
        for back in range(1, SSD_CONV):
            j = SSD_CONV - 1 - back
            acc = acc + cwh[j:j + 1, :] * shifted[(back - 1) * L:back * L, :]
        conv_ref[:, cols] = _silu_of_twice(acc)

    dtr = dt_ref[...] + dtb_ref[...]
    dtv = jnp.maximum(dtr, 0.0) + jnp.log(1.0 + jnp.exp(-jnp.abs(dtr)))
    adt = dtv * (-jnp.exp(alog_ref[...]))
    cs_t = _dot(jnp.concatenate(_split3(adt), axis=1), tri3_ref[...]) * LOG2E
    yield
    csd_t = cs_t - jnp.log2(dtv)
    w_state_t = dtv * jnp.exp2(cs_t[:, L - 1:L] - cs_t)

    def by_time(a_t):
        return jnp.concatenate([a_t, jnp.zeros((LANES - SSD_HEADS, L), F32)], axis=0).T

    cs = by_time(cs_t)
    ecs = by_time(jnp.exp2(cs_t))
    w_state = by_time(w_state_t)

    def expand(a):
        hi = a.astype(BF16)
        lo = (a - hi.astype(F32)).astype(BF16)
        return _dot(jnp.concatenate([hi, lo], axis=1), exp2_ref[...])

    ecs_x = expand(ecs)
    wst_x = expand(w_state)
    yield

    row = lax.broadcasted_iota(jnp.int32, (L, L), 0)
    colm = lax.broadcasted_iota(jnp.int32, (L, L), 1)
    tril = row >= colm
    lane = lax.broadcasted_iota(jnp.int32, (L, LANES), 1)
    lo_half = lane < SSD_HEAD_DIM

    for g in range(SSD_GROUPS):
        gx = g * SSD_GROUP_WIDTH
        bm = conv_ref[:, SSD_INNER + g * SSD_STATE:SSD_INNER + (g + 1) * SSD_STATE]
        cm = conv_ref[:, SSD_INNER + SSD_BC + g * SSD_STATE:SSD_INNER + SSD_BC + (g + 1) * SSD_STATE]
        cm16 = cm.astype(BF16)
        cb = _dot_nt(cm16, bm.astype(BF16))
        xs_g = conv_ref[:, gx:gx + SSD_GROUP_WIDTH]

        y_off = _dot(cm16, state_ref[:, gx:gx + SSD_GROUP_WIDTH].astype(BF16))
        yield
        yacc_ref[:, gx:gx + SSD_GROUP_WIDTH] = (
            y_off * ecs_x[:, gx:gx + SSD_GROUP_WIDTH] + xs_g * dskip_ref[:, gx:gx + SSD_GROUP_WIDTH])

        for pair in range(SSD_HEADS // SSD_GROUPS // 2):
            ms = []
            h0 = g * (SSD_HEADS // SSD_GROUPS) + 2 * pair
            for h in (h0, h0 + 1):
                seg = cs[:, h:h + 1] - csd_t[h:h + 1, :]
                ms.append((cb * jnp.exp2(jnp.where(tril, seg, -jnp.inf))).astype(BF16))
            x_pair = conv_ref[:, gx + pair * LANES:gx + (pair + 1) * LANES]
            x_blk = jnp.concatenate(
                [jnp.where(lo_half, x_pair, 0.0), jnp.where(lo_half, 0.0, x_pair)], axis=0).astype(BF16)
            sl = slice(gx + pair * LANES, gx + (pair + 1) * LANES)
            y_diag = _dot(jnp.concatenate(ms, axis=1), x_blk)
            yield
            yacc_ref[:, sl] = yacc_ref[:, sl] + y_diag

        xd = (xs_g * wst_x[:, gx:gx + SSD_GROUP_WIDTH]).astype(BF16)
        contrib = _dot(bm.T.astype(BF16), xd)
        yield
        state_ref[:, gx:gx + SSD_GROUP_WIDTH] = (
            state_ref[:, gx:gx + SSD_GROUP_WIDTH] * ecs_x[L - 1:L, gx:gx + SSD_GROUP_WIDTH] + contrib)

        z_half = z_ref[:, gx:gx + SSD_GROUP_WIDTH].astype(F32)
        gy = yacc_ref[:, gx:gx + SSD_GROUP_WIDTH] * _silu_of_twice(z_half)
        y_ref[:, gx:gx + SSD_GROUP_WIDTH] = (
            gy * _rms_scale(gy) * nw_ref[:, gx:gx + SSD_GROUP_WIDTH]).astype(y_ref.dtype)


def _ssd(z, xbc, dt_t, cw, cb, dtb, alog, dskip_x, nw, tri3, exp2, shift, bsz, seqlen):
    L = SSD_CHUNK * SSD_CHUNKS_PER_STEP
    nc = seqlen // L
    rb = SSD_ROWS_PER_STEP
    row = lambda w: pl.BlockSpec((rb, L, w), lambda r, c: (r, c, 0))
    dt_spec = pl.BlockSpec((rb, SSD_HEADS, L), lambda r, c: (r, 0, c))
    per_batch = lambda a: a.reshape(bsz, seqlen, a.shape[-1])
    consts = (cw, cb, dtb, alog, dskip_x, nw, tri3, exp2, shift)
    y = pl.pallas_call(
        _ssd_kernel,
        grid=(bsz // rb, nc),
        in_specs=[row(SSD_INNER), row(SSD_CONV_DIM), dt_spec] + [_const_spec(a.shape) for a in consts],
        out_specs=row(SSD_INNER),
        out_shape=jax.ShapeDtypeStruct((bsz, seqlen, SSD_INNER), BF16),
        scratch_shapes=[
            pltpu.VMEM((rb, CONV_HALO + L, SSD_CONV_DIM), BF16),
            pltpu.VMEM((rb, SSD_CHUNK, SSD_CONV_DIM), F32),
            pltpu.VMEM((rb, SSD_STATE, SSD_INNER), F32),
            pltpu.VMEM((rb, SSD_CHUNK, SSD_INNER), F32),
        ],
        compiler_params=pltpu.CompilerParams(
            dimension_semantics=("arbitrary", "arbitrary"), vmem_limit_bytes=VMEM_LIMIT),
        name="ssd",
    )(per_batch(z), per_batch(xbc), dt_t.reshape(SSD_HEADS, bsz, seqlen).transpose(1, 0, 2), *consts)
    return y.reshape(bsz * seqlen, SSD_INNER)


def _attn_kernel(qt_ref, k_ref, vt_ref, lq1_ref, lk1_ref, lq2_ref, lk2_ref, o_ref,
                 qs_ref, acc_ref, m_ref, mx_ref, s_ref):
    for t in range(Q_TILES_PER_STEP):
        _attn_query_tile(pl.program_id(2) * Q_TILES_PER_STEP + t, slice(t * TQ, (t + 1) * TQ),
                         qt_ref, k_ref, vt_ref, lq1_ref, lk1_ref, lq2_ref, lk2_ref, o_ref,
                         qs_ref, acc_ref, m_ref, mx_ref, s_ref)


def _attn_query_tile(i, cols, qt_ref, k_ref, vt_ref, lq1_ref, lk1_ref, lq2_ref, lk2_ref, o_ref,
                     qs_ref, acc_ref, m_ref, mx_ref, s_ref):
    d_idx = lax.broadcasted_iota(jnp.int32, (DA_V_DIM, TQ), 0)
    ones_rows = jnp.ones((ONES_ROWS, TK), BF16)

    for g in range(HEADS_PER_STEP):
        rows = slice(g * DA_V_DIM, (g + 1) * DA_V_DIM)
        q = qt_ref[rows, cols]
        zero = jnp.zeros_like(q)
        qs_ref[rows, 0:TQ] = jnp.where(d_idx < DA_HEAD_DIM, q, zero)
        qs_ref[rows, TQ:2 * TQ] = jnp.where(d_idx < DA_HEAD_DIM, zero, q)

    def scores(g, j, diagonal):
        rows = slice(g * DA_V_DIM, (g + 1) * DA_V_DIM)
        off = pl.multiple_of(j * TK, TK)
        s = _dot(k_ref[pl.ds(off, TK), rows], qs_ref[rows, :])
        if diagonal:
            key = lax.broadcasted_iota(jnp.int32, s.shape, 0)
            qry = lax.broadcasted_iota(jnp.int32, s.shape, 1) & (TQ - 1)
            s = jnp.where(key <= qry, s, -jnp.inf)
        s_ref[g] = s
        mx_ref[g:g + 1, :] = jnp.max(s, axis=0, keepdims=True)

    def softmax_pv(g, j, first):
        rows = slice(g * DA_V_DIM, (g + 1) * DA_V_DIM)
        arow = slice(g * ACC_ROWS, (g + 1) * ACC_ROWS)
        off = pl.multiple_of(j * TK, TK)
        if first:
            m_new = mx_ref[g:g + 1, :]
        else:
            m_prev = m_ref[g:g + 1, :]
            m_new = jnp.maximum(m_prev, mx_ref[g:g + 1, :])
        p = jnp.exp2(s_ref[g] - m_new).astype(BF16)
        v_ext = jnp.concatenate([vt_ref[rows, pl.ds(off, TK)], ones_rows], axis=0)
        pv = _dot(v_ext, p)
        if first:
            acc_ref[arow, :] = pv
        else:
            acc_ref[arow, :] = jnp.exp2(m_prev - m_new) * acc_ref[arow, :] + pv
        m_ref[g:g + 1, :] = m_new

    def run_tiles(tiles, next_tile, first_is_diagonal=False):
        chains = [(t, g, first_is_diagonal and n == 0) for n, t in enumerate(tiles) for g in range(HEADS_PER_STEP)]
        for n, (t, g, diagonal) in enumerate(chains):
            ahead = n + SCORE_LOOKAHEAD
            if ahead < len(chains):
                scores(chains[ahead][1], chains[ahead][0], chains[ahead][2])
            else:
                scores(ahead - len(chains), next_tile, False)
            softmax_pv(g, t, first=diagonal)

    for rest in range(KEY_TILES_PER_TRIP):
        @pl.when(i % KEY_TILES_PER_TRIP == rest)
        def _(rest=rest):
            for g in range(SCORE_LOOKAHEAD):
                scores(g, i, True)
            run_tiles([i] + [i - rest + n for n in range(rest)], next_tile=0, first_is_diagonal=True)

    def body(jj, carry):
        t0 = KEY_TILES_PER_TRIP * jj
        run_tiles([t0 + n for n in range(KEY_TILES_PER_TRIP)], next_tile=t0 + KEY_TILES_PER_TRIP)
        return carry

    lax.fori_loop(0, i // KEY_TILES_PER_TRIP, body, 0)

    lam = (jnp.exp(jnp.sum(lq1_ref[...] * lk1_ref[...], axis=1, keepdims=True))
           - jnp.exp(jnp.sum(lq2_ref[...] * lk2_ref[...], axis=1, keepdims=True)) + LAMBDA_INIT)
    for g in range(HEADS_PER_STEP):
        a0 = g * ACC_ROWS
        inv = 1.0 / acc_ref[a0 + DA_V_DIM:a0 + DA_V_DIM + 1, :]
        o = (acc_ref[a0:a0 + DA_V_DIM, 0:TQ] * inv[:, 0:TQ]
             - acc_ref[a0:a0 + DA_V_DIM, TQ:2 * TQ] * (lam * inv[:, TQ:2 * TQ]))
        o_ref[cols, g * DA_V_DIM:(g + 1) * DA_V_DIM] = o.T.astype(o_ref.dtype)


def _attention(qt, k, vt, lq1, lk1, lq2, lk2, bsz, seqlen):
    assert TQ == TK
    tq = Q_TILES_PER_STEP * TQ
    nq = seqlen // tq
    gw = HEADS_PER_STEP * DA_V_DIM
    small = (lq1, lk1, lq2, lk2)
    return pl.pallas_call(
        _attn_kernel,
        grid=(bsz, DA_HEADS // HEADS_PER_STEP, nq),
        in_specs=[
            pl.BlockSpec((gw, tq), lambda b, h, i: (h, b * nq + i)),
            pl.BlockSpec((seqlen, gw), lambda b, h, i: (b, h)),
            pl.BlockSpec((gw, seqlen), lambda b, h, i: (h, b)),
        ] + [_const_spec(a.shape) for a in small],
        out_specs=pl.BlockSpec((tq, gw), lambda b, h, i: (b * nq + i, h)),
        out_shape=jax.ShapeDtypeStruct((bsz * seqlen, DA_WIDTH), BF16),
        scratch_shapes=[
            pltpu.VMEM((gw, 2 * TQ), BF16),
            pltpu.VMEM((HEADS_PER_STEP * ACC_ROWS, 2 * TQ), F32),
            pltpu.VMEM((HEADS_PER_STEP, 2 * TQ), F32),
            pltpu.VMEM((HEADS_PER_STEP, 2 * TQ), F32),
            pltpu.VMEM((HEADS_PER_STEP, TK, 2 * TQ), F32),
        ],
        compiler_params=pltpu.CompilerParams(
            dimension_semantics=("arbitrary", "arbitrary", "arbitrary"), vmem_limit_bytes=VMEM_LIMIT),
        name="diff_attn",
    )(qt, k, vt, *small)


def _mlp_kernel(x_ref, ys_ref, ya_ref, sw_ref, wo_ref, nw_ref, wg_ref, wu_ref, wd_ref, fw_ref, o_ref, h_ref):
    gain = sw_ref[...] * (1.0 - LAMBDA_INIT)
    heads = []
    for g in range(DA_HEADS):
        o = ya_ref[:, g * DA_V_DIM:(g + 1) * DA_V_DIM].astype(F32)
        heads.append((o * _rms_scale(o) * gain).astype(BF16))
    ya = jnp.concatenate(heads, axis=1)
    h_ref[...] = x_ref[...] + _dot(ys_ref[...], wo_ref[0:SSD_INNER, :]) + _dot(ya, wo_ref[SSD_INNER:, :])
    h = h_ref[...]
    n2 = (h * _rms_scale(h) * nw_ref[...]).astype(BF16)
    ffn = None
    for f0, f1 in zip(FF_SPLITS[:-1], FF_SPLITS[1:]):
        gate = _dot(n2, wg_ref[:, f0:f1])
        up = _dot(n2, wu_ref[:, f0:f1])
        act = (_silu(gate) * up).astype(BF16)
        down = _dot(act, wd_ref[f0:f1, :])
        ffn = down if ffn is None else ffn + down
    out = h_ref[...] + ffn
    o_ref[...] = out * _rms_scale(out) * fw_ref[...]


def _mlp(x2, ys, ya, sw, wo, nw, wg, wu, wd, fw):
    t = x2.shape[0]
    tm = TM_PROJ
    row = pl.BlockSpec((tm, D_MODEL), lambda i: (i, 0))
    consts = (sw, wo, nw, wg, wu, wd, fw)
    return pl.pallas_call(
        _mlp_kernel,
        grid=(t // tm,),
        in_specs=[row, row, row] + [_const_spec(a.shape) for a in consts],
        out_specs=row,
        out_shape=jax.ShapeDtypeStruct((t, D_MODEL), F32),
        scratch_shapes=[pltpu.VMEM((tm, D_MODEL), F32)],
        compiler_params=pltpu.CompilerParams(
            dimension_semantics=("arbitrary",), vmem_limit_bytes=VMEM_LIMIT),
        name="mlp",
    )(x2, ys, ya, *consts)


def kernel(x, mix_norm_w, w_in, conv_w, conv_b, dt_bias, a_log, d_skip, ssd_norm_w, lam_q1, lam_k1, lam_q2,
           lam_k2, subln_w, w_out, ffn_norm_w, w_gate, w_up, w_down, final_norm_w):
    bsz, seqlen, _ = x.shape
    x2 = x.reshape(bsz * seqlen, D_MODEL)

    (z, xbc, dt_t, k, qt, vt), (wo16, wg16, wu16, wd16) = _in_proj(
        x2, mix_norm_w[0][None, :], w_in[0].T.astype(BF16), (w_out[0], w_gate[0], w_up[0], w_down[0]))

    idx = jnp.arange(SSD_CHUNK)
    tri = (idx[:, None] >= idx[None, :]).astype(BF16)
    tri3 = jnp.concatenate([tri.T, tri.T, tri.T], axis=0)
    sel = (jnp.arange(LANES)[:, None] == (jnp.arange(SSD_INNER)[None, :] // SSD_HEAD_DIM)).astype(BF16)
    exp2 = jnp.concatenate([sel, sel], axis=0)
    src = CONV_HALO + idx[None, :, None] - jnp.arange(1, SSD_CONV)[:, None, None]
    shift = (jnp.arange(CONV_HALO + SSD_CHUNK)[None, None, :] == src).astype(BF16)
    shift = shift.reshape((SSD_CONV - 1) * SSD_CHUNK, CONV_HALO + SSD_CHUNK)
    dskip_x = jnp.repeat(d_skip[0].astype(F32), SSD_HEAD_DIM)[None, :]

    per_head = lambda v: jnp.broadcast_to(v.astype(F32)[:, None], (SSD_HEADS, SSD_CHUNK))
    y_ssd = _ssd(z, xbc, dt_t, conv_w[0], conv_b[0][None, :], per_head(dt_bias[0]), per_head(a_log[0]),
                 dskip_x, ssd_norm_w[0][None, :], tri3, exp2, shift, bsz, seqlen)
    y_da = _attention(qt, k, vt, lam_q1[0][None, :], lam_k1[0][None, :], lam_q2[0][None, :],
                      lam_k2[0][None, :], bsz, seqlen)

    out = _mlp(x2, y_ssd, y_da, subln_w[0][None, :], wo16, ffn_norm_w[0][None, :], wg16, wu16, wd16,
               final_norm_w[None, :])
    return out.reshape(bsz, seqlen, D_MODEL)
```

```python
import math

import jax
import jax.numpy as jnp
from jax import lax
from jax.experimental import pallas as pl
from jax.experimental.pallas import tpu as pltpu

F32 = jnp.float32
BF16 = jnp.bfloat16

EPS = 1e-5
D_MODEL = 1024
SSD_HEADS = 16
SSD_HEAD_DIM = 64
SSD_INNER = SSD_HEADS * SSD_HEAD_DIM
SSD_GROUPS = 2
SSD_GROUP_WIDTH = SSD_INNER // SSD_GROUPS
SSD_STATE = 128
SSD_CONV = 4
SSD_CHUNK = 128
SSD_BC = SSD_GROUPS * SSD_STATE
SSD_CONV_DIM = SSD_INNER + 2 * SSD_BC
DA_HEADS = 8
DA_HEAD_DIM = 64
DA_V_DIM = 2 * DA_HEAD_DIM
DA_WIDTH = DA_HEADS * DA_V_DIM
D_FF = 2816
LAMBDA_INIT = 0.8 - 0.6 * math.exp(-0.3 * 0)

LANES = 128
CONV_HALO = 16
CONV_LANES = 512
VMEM_LIMIT = 56 * 1024 * 1024

TM_PROJ = 512
MLP_ROWS = 256
TM_IN_PROJ = 1024
TQ = 256
TK = 256
MXU_WIDTH = 256
FF_SPLITS = (0, 6 * MXU_WIDTH, D_FF)
HEADS_PER_STEP = 8
Q_TILES_PER_STEP = 2
KEY_TILES_PER_TRIP = 3
SSD_ROWS_PER_STEP = 4
SSD_CHUNKS_PER_STEP = 2
SCORE_LOOKAHEAD = 2
ONES_ROWS = 16
ACC_ROWS = DA_V_DIM + ONES_ROWS
LOG2E = math.log2(math.e)


def _const_spec(shape):
    nd = len(shape)
    return pl.BlockSpec(shape, lambda *_: (0,) * nd, pipeline_mode=pl.Buffered(1))


def _rms_scale(xf):
    return lax.rsqrt(jnp.mean(xf * xf, axis=-1, keepdims=True) + EPS)


def _dot(a, b):
    return jnp.dot(a, b, preferred_element_type=F32)


def _dot_nt(a, b):
    return lax.dot_general(a, b, (((1,), (1,)), ((), ())), preferred_element_type=F32)


def _split3(a):
    hi = a.astype(BF16)
    r1 = a - hi.astype(F32)
    mid = r1.astype(BF16)
    lo = (r1 - mid.astype(F32)).astype(BF16)
    return hi, mid, lo


def _silu_of_twice(h):
    return h + h * jnp.tanh(h)


def _silu(x):
    return _silu_of_twice(0.5 * x)


def _in_proj_kernel(x_ref, nw_ref, wt_ref, *refs):
    n_cast = (len(refs) - 6) // 2
    cast_in, (z_ref, xbc_ref, dt_ref, k_ref, qt_ref, vt_ref), cast_out = (
        refs[:n_cast], refs[n_cast:n_cast + 6], refs[n_cast + 6:])
    for src, dst in zip(cast_in, cast_out):
        dst[...] = src[...].astype(BF16)
    xf = x_ref[...]
    xn = (xf * _rms_scale(xf) * nw_ref[...]).astype(BF16)
    r0 = 0
    for ref, width, scale, transposed in (
            (z_ref, SSD_INNER, 0.5, False), (xbc_ref, SSD_CONV_DIM, None, False), (dt_ref, SSD_HEADS, None, True),
            (qt_ref, DA_WIDTH, DA_HEAD_DIM ** -0.5 * LOG2E, True), (k_ref, DA_WIDTH, None, False),
            (vt_ref, DA_WIDTH, None, True)):
        w = wt_ref[r0:r0 + width, :]
        r = _dot_nt(w, xn) if transposed else _dot_nt(xn, w)
        ref[...] = (r if scale is None else r * scale).astype(ref.dtype)
        r0 += width


def _in_proj(x2, nw, wt, to_cast):
    t = x2.shape[0]
    tm = TM_IN_PROJ
    steps = t // tm
    row = lambda w: pl.BlockSpec((tm, w), lambda i: (i, 0))
    col = pl.BlockSpec((DA_WIDTH, tm), lambda i: (0, i))
    cast_specs = [pl.BlockSpec((a.shape[0] // steps, a.shape[1]), lambda i: (i, 0)) for a in to_cast]
    outs = pl.pallas_call(
        _in_proj_kernel,
        grid=(steps,),
        in_specs=[row(D_MODEL), _const_spec(nw.shape), _const_spec(wt.shape)] + cast_specs,
        out_specs=[row(SSD_INNER), row(SSD_CONV_DIM), pl.BlockSpec((SSD_HEADS, tm), lambda i: (0, i)),
                   row(DA_WIDTH), col, col] + cast_specs,
        out_shape=[
            jax.ShapeDtypeStruct((t, SSD_INNER), BF16),
            jax.ShapeDtypeStruct((t, SSD_CONV_DIM), BF16),
            jax.ShapeDtypeStruct((SSD_HEADS, t), F32),
            jax.ShapeDtypeStruct((t, DA_WIDTH), BF16),
            jax.ShapeDtypeStruct((DA_WIDTH, t), BF16),
            jax.ShapeDtypeStruct((DA_WIDTH, t), BF16),
        ] + [jax.ShapeDtypeStruct(a.shape, BF16) for a in to_cast],
        compiler_params=pltpu.CompilerParams(
            dimension_semantics=("arbitrary",), vmem_limit_bytes=VMEM_LIMIT),
        name="in_proj",
    )(x2, nw, wt, *to_cast)
    return outs[:6], outs[6:]


def _ssd_kernel(z_ref, xbc_ref, dt_ref, cw_ref, cb_ref, dtb_ref, alog_ref, dskip_ref, nw_ref,
                tri3_ref, exp2_ref, shift_ref, y_ref, ext_ref, conv_ref, state_ref, yacc_ref):
    L = SSD_CHUNK
    c = pl.program_id(1)

    @pl.when(c == 0)
    def _():
        state_ref[...] = jnp.zeros_like(state_ref)
        ext_ref[:, 0:CONV_HALO, :] = jnp.zeros((ext_ref.shape[0], CONV_HALO, SSD_CONV_DIM), BF16)

    @pl.when(c > 0)
    def _():
        ext_ref[:, 0:CONV_HALO, :] = ext_ref[:, SSD_CHUNKS_PER_STEP * L:SSD_CHUNKS_PER_STEP * L + CONV_HALO, :]

    consts = (cw_ref, cb_ref, dtb_ref, alog_ref, dskip_ref, nw_ref, tri3_ref, exp2_ref, shift_ref)

    def row_chunks(b):
        for n in range(SSD_CHUNKS_PER_STEP):
            t = pl.ds(n * L, L)
            yield from _ssd_chunk(z_ref.at[b, t], xbc_ref.at[b, t], dt_ref.at[b, :, t], *consts, y_ref.at[b, t],
                                  ext_ref.at[b, pl.ds(n * L, CONV_HALO + L)], conv_ref.at[b], state_ref.at[b],
                                  yacc_ref.at[b])

    _round_robin([row_chunks(b) for b in range(z_ref.shape[0])])


def _round_robin(stages):
    stages = list(stages)
    while stages:
        for gen in list(stages):
            if next(gen, StopIteration) is StopIteration:
                stages.remove(gen)


def _ssd_chunk(z_ref, xbc_ref, dt_ref, cw_ref, cb_ref, dtb_ref, alog_ref, dskip_ref, nw_ref,
               tri3_ref, exp2_ref, shift_ref, y_ref, ext_ref, conv_ref, state_ref, yacc_ref):
    L = SSD_CHUNK
    ext_ref[CONV_HALO:CONV_HALO + L, :] = xbc_ref[...]

    for c0 in range(0, SSD_CONV_DIM, CONV_LANES):
        cols = slice(c0, c0 + CONV_LANES)
        e = ext_ref[:, cols]
        cwh = 0.5 * cw_ref[:, cols]
        acc = 0.5 * cb_ref[:, cols] + cwh[SSD_CONV - 1:SSD_CONV, :] * e[CONV_HALO:, :].astype(F32)
        shifted = _dot(shift_ref[...], e)
        for back in range(1, SSD_CONV):
            j = SSD_CONV - 1 - back
            acc = acc + cwh[j:j + 1, :] * shifted[(back - 1) * L:back * L, :]
        conv_ref[:, cols] = _silu_of_twice(acc)

    dtr = dt_ref[...] + dtb_ref[...]
    dtv = jnp.maximum(dtr, 0.0) + jnp.log(1.0 + jnp.exp(-jnp.abs(dtr)))
    adt = dtv * (-jnp.exp(alog_ref[...]))
    cs_t = _dot(jnp.concatenate(_split3(adt), axis=1), tri3_ref[...]) * LOG2E
    yield
    csd_t = cs_t - jnp.log2(dtv)
    w_state_t = dtv * jnp.exp2(cs_t[:, L - 1:L] - cs_t)

    def by_time(a_t):
        return jnp.concatenate([a_t, jnp.zeros((LANES - SSD_HEADS, L), F32)], axis=0).T

    cs = by_time(cs_t)
    ecs = by_time(jnp.exp2(cs_t))
    w_state = by_time(w_state_t)

    def expand(a):
        hi = a.astype(BF16)
        lo = (a - hi.astype(F32)).astype(BF16)
        return _dot(jnp.concatenate([hi, lo], axis=1), exp2_ref[...])

    ecs_x = expand(ecs)
    wst_x = expand(w_state)
    yield

    row = lax.broadcasted_iota(jnp.int32, (L, L), 0)
    colm = lax.broadcasted_iota(jnp.int32, (L, L), 1)
    tril = row >= colm
    lane = lax.broadcasted_iota(jnp.int32, (L, LANES), 1)
    lo_half = lane < SSD_HEAD_DIM

    for g in range(SSD_GROUPS):
        gx = g * SSD_GROUP_WIDTH
        bm = conv_ref[:, SSD_INNER + g * SSD_STATE:SSD_INNER + (g + 1) * SSD_STATE]
        cm = conv_ref[:, SSD_INNER + SSD_BC + g * SSD_STATE:SSD_INNER + SSD_BC + (g + 1) * SSD_STATE]
        cm16 = cm.astype(BF16)
        cb = _dot_nt(cm16, bm.astype(BF16))
        xs_g = conv_ref[:, gx:gx + SSD_GROUP_WIDTH]

        y_off = _dot(cm16, state_ref[:, gx:gx + SSD_GROUP_WIDTH].astype(BF16))
        yield
        yacc_ref[:, gx:gx + SSD_GROUP_WIDTH] = (
            y_off * ecs_x[:, gx:gx + SSD_GROUP_WIDTH] + xs_g * dskip_ref[:, gx:gx + SSD_GROUP_WIDTH])

        for pair in range(SSD_HEADS // SSD_GROUPS // 2):
            ms = []
            h0 = g * (SSD_HEADS // SSD_GROUPS) + 2 * pair
            for h in (h0, h0 + 1):
                seg = cs[:, h:h + 1] - csd_t[h:h + 1, :]
                ms.append((cb * jnp.exp2(jnp.where(tril, seg, -jnp.inf))).astype(BF16))
            x_pair = conv_ref[:, gx + pair * LANES:gx + (pair + 1) * LANES]
            x_blk = jnp.concatenate(
                [jnp.where(lo_half, x_pair, 0.0), jnp.where(lo_half, 0.0, x_pair)], axis=0).astype(BF16)
            sl = slice(gx + pair * LANES, gx + (pair + 1) * LANES)
            y_diag = _dot(jnp.concatenate(ms, axis=1), x_blk)
            yield
            yacc_ref[:, sl] = yacc_ref[:, sl] + y_diag

        xd = (xs_g * wst_x[:, gx:gx + SSD_GROUP_WIDTH]).astype(BF16)
        contrib = _dot(bm.T.astype(BF16), xd)
        yield
        state_ref[:, gx:gx + SSD_GROUP_WIDTH] = (
            state_ref[:, gx:gx + SSD_GROUP_WIDTH] * ecs_x[L - 1:L, gx:gx + SSD_GROUP_WIDTH] + contrib)

        z_half = z_ref[:, gx:gx + SSD_GROUP_WIDTH].astype(F32)
        gy = yacc_ref[:, gx:gx + SSD_GROUP_WIDTH] * _silu_of_twice(z_half)
        y_ref[:, gx:gx + SSD_GROUP_WIDTH] = (
            gy * _rms_scale(gy) * nw_ref[:, gx:gx + SSD_GROUP_WIDTH]).astype(y_ref.dtype)


def _ssd(z, xbc, dt_t, cw, cb, dtb, alog, dskip_x, nw, tri3, exp2, shift, bsz, seqlen):
    L = SSD_CHUNK * SSD_CHUNKS_PER_STEP
    nc = seqlen // L
    rb = SSD_ROWS_PER_STEP
    row = lambda w: pl.BlockSpec((rb, L, w), lambda r, c: (r, c, 0))
    dt_spec = pl.BlockSpec((rb, SSD_HEADS, L), lambda r, c: (r, 0, c))
    per_batch = lambda a: a.reshape(bsz, seqlen, a.shape[-1])
    consts = (cw, cb, dtb, alog, dskip_x, nw, tri3, exp2, shift)
    y = pl.pallas_call(
        _ssd_kernel,
        grid=(bsz // rb, nc),
        in_specs=[row(SSD_INNER), row(SSD_CONV_DIM), dt_spec] + [_const_spec(a.shape) for a in consts],
        out_specs=row(SSD_INNER),
        out_shape=jax.ShapeDtypeStruct((bsz, seqlen, SSD_INNER), BF16),
        scratch_shapes=[
            pltpu.VMEM((rb, CONV_HALO + L, SSD_CONV_DIM), BF16),
            pltpu.VMEM((rb, SSD_CHUNK, SSD_CONV_DIM), F32),
            pltpu.VMEM((rb, SSD_STATE, SSD_INNER), F32),
            pltpu.VMEM((rb, SSD_CHUNK, SSD_INNER), F32),
        ],
        compiler_params=pltpu.CompilerParams(
            dimension_semantics=("arbitrary", "arbitrary"), vmem_limit_bytes=VMEM_LIMIT),
        name="ssd",
    )(per_batch(z), per_batch(xbc), dt_t.reshape(SSD_HEADS, bsz, seqlen).transpose(1, 0, 2), *consts)
    return y.reshape(bsz * seqlen, SSD_INNER)


def _attn_kernel(qt_ref, k_ref, vt_ref, lq1_ref, lk1_ref, lq2_ref, lk2_ref, o_ref,
                 qs_ref, acc_ref, m_ref, mx_ref, s_ref):
    for t in range(Q_TILES_PER_STEP):
        _attn_query_tile(pl.program_id(2) * Q_TILES_PER_STEP + t, slice(t * TQ, (t + 1) * TQ),
                         qt_ref, k_ref, vt_ref, lq1_ref, lk1_ref, lq2_ref, lk2_ref, o_ref,
                         qs_ref, acc_ref, m_ref, mx_ref, s_ref)


def _attn_query_tile(i, cols, qt_ref, k_ref, vt_ref, lq1_ref, lk1_ref, lq2_ref, lk2_ref, o_ref,
                     qs_ref, acc_ref, m_ref, mx_ref, s_ref):
    d_idx = lax.broadcasted_iota(jnp.int32, (DA_V_DIM, TQ), 0)
    ones_rows = jnp.ones((ONES_ROWS, TK), BF16)

    for g in range(HEADS_PER_STEP):
        rows = slice(g * DA_V_DIM, (g + 1) * DA_V_DIM)
        q = qt_ref[rows, cols]
        zero = jnp.zeros_like(q)
        qs_ref[rows, 0:TQ] = jnp.where(d_idx < DA_HEAD_DIM, q, zero)
        qs_ref[rows, TQ:2 * TQ] = jnp.where(d_idx < DA_HEAD_DIM, zero, q)

    def scores(g, j, diagonal):
        rows = slice(g * DA_V_DIM, (g + 1) * DA_V_DIM)
        off = pl.multiple_of(j * TK, TK)
        s = _dot(k_ref[pl.ds(off, TK), rows], qs_ref[rows, :])
        if diagonal:
            key = lax.broadcasted_iota(jnp.int32, s.shape, 0)
            qry = lax.broadcasted_iota(jnp.int32, s.shape, 1) & (TQ - 1)
            s = jnp.where(key <= qry, s, -jnp.inf)
        s_ref[g] = s
        mx_ref[g:g + 1, :] = jnp.max(s, axis=0, keepdims=True)

    def softmax_pv(g, j, first):
        rows = slice(g * DA_V_DIM, (g + 1) * DA_V_DIM)
        arow = slice(g * ACC_ROWS, (g + 1) * ACC_ROWS)
        off = pl.multiple_of(j * TK, TK)
        if first:
            m_new = mx_ref[g:g + 1, :]
        else:
            m_prev = m_ref[g:g + 1, :]
            m_new = jnp.maximum(m_prev, mx_ref[g:g + 1, :])
        p = jnp.exp2(s_ref[g] - m_new).astype(BF16)
        v_ext = jnp.concatenate([vt_ref[rows, pl.ds(off, TK)], ones_rows], axis=0)
        pv = _dot(v_ext, p)
        if first:
            acc_ref[arow, :] = pv
        else:
            acc_ref[arow, :] = jnp.exp2(m_prev - m_new) * acc_ref[arow, :] + pv
        m_ref[g:g + 1, :] = m_new

    def run_tiles(tiles, diagonal, next_tile):
        chains = [(t, g) for t in tiles for g in range(HEADS_PER_STEP)]
        for n, (t, g) in enumerate(chains):
            ahead = n + SCORE_LOOKAHEAD
            if ahead < len(chains):
                scores(chains[ahead][1], chains[ahead][0], diagonal)
            else:
                scores(ahead - len(chains), next_tile, False)
            softmax_pv(g, t, first=diagonal)

    for g in range(SCORE_LOOKAHEAD):
        scores(g, i, True)
    run_tiles([i], diagonal=True, next_tile=0)

    def body(jj, carry):
        t0 = KEY_TILES_PER_TRIP * jj
        run_tiles([t0 + n for n in range(KEY_TILES_PER_TRIP)], diagonal=False, next_tile=t0 + KEY_TILES_PER_TRIP)
        return carry

    lax.fori_loop(0, i // KEY_TILES_PER_TRIP, body, 0)

    for rest in range(1, KEY_TILES_PER_TRIP):
        @pl.when(i % KEY_TILES_PER_TRIP == rest)
        def _(rest=rest):
            run_tiles([i - rest + n for n in range(rest)], diagonal=False, next_tile=i)

    lam = (jnp.exp(jnp.sum(lq1_ref[...] * lk1_ref[...], axis=1, keepdims=True))
           - jnp.exp(jnp.sum(lq2_ref[...] * lk2_ref[...], axis=1, keepdims=True)) + LAMBDA_INIT)
    for g in range(HEADS_PER_STEP):
        a0 = g * ACC_ROWS
        inv = 1.0 / acc_ref[a0 + DA_V_DIM:a0 + DA_V_DIM + 1, :]
        o = (acc_ref[a0:a0 + DA_V_DIM, 0:TQ] * inv[:, 0:TQ]
             - acc_ref[a0:a0 + DA_V_DIM, TQ:2 * TQ] * (lam * inv[:, TQ:2 * TQ]))
        o_ref[cols, g * DA_V_DIM:(g + 1) * DA_V_DIM] = o.T.astype(o_ref.dtype)


def _attention(qt, k, vt, lq1, lk1, lq2, lk2, bsz, seqlen):
    assert TQ == TK
    tq = Q_TILES_PER_STEP * TQ
    nq = seqlen // tq
    gw = HEADS_PER_STEP * DA_V_DIM
    small = (lq1, lk1, lq2, lk2)
    return pl.pallas_call(
        _attn_kernel,
        grid=(bsz, DA_HEADS // HEADS_PER_STEP, nq),
        in_specs=[
            pl.BlockSpec((gw, tq), lambda b, h, i: (h, b * nq + i)),
            pl.BlockSpec((seqlen, gw), lambda b, h, i: (b, h)),
            pl.BlockSpec((gw, seqlen), lambda b, h, i: (h, b)),
        ] + [_const_spec(a.shape) for a in small],
        out_specs=pl.BlockSpec((tq, gw), lambda b, h, i: (b * nq + i, h)),
        out_shape=jax.ShapeDtypeStruct((bsz * seqlen, DA_WIDTH), BF16),
        scratch_shapes=[
            pltpu.VMEM((gw, 2 * TQ), BF16),
            pltpu.VMEM((HEADS_PER_STEP * ACC_ROWS, 2 * TQ), F32),
            pltpu.VMEM((HEADS_PER_STEP, 2 * TQ), F32),
            pltpu.VMEM((HEADS_PER_STEP, 2 * TQ), F32),
            pltpu.VMEM((HEADS_PER_STEP, TK, 2 * TQ), F32),
        ],
        compiler_params=pltpu.CompilerParams(
            dimension_semantics=("arbitrary", "arbitrary", "arbitrary"), vmem_limit_bytes=VMEM_LIMIT),
        name="diff_attn",
    )(qt, k, vt, *small)


def _mlp_kernel(x_ref, ys_ref, ya_ref, sw_ref, wo_ref, nw_ref, wg_ref, wu_ref, wd_ref, fw_ref, o_ref, h_ref):
    blocks = [slice(r0, r0 + MLP_ROWS) for r0 in range(0, x_ref.shape[0], MLP_ROWS)]
    _round_robin([_mlp_rows(rs, x_ref, ys_ref, ya_ref, sw_ref, wo_ref, nw_ref, wg_ref, wu_ref, wd_ref, fw_ref,
                            o_ref, h_ref) for rs in blocks])


def _mlp_rows(rs, x_ref, ys_ref, ya_ref, sw_ref, wo_ref, nw_ref, wg_ref, wu_ref, wd_ref, fw_ref, o_ref, h_ref):
    gain = sw_ref[...] * (1.0 - LAMBDA_INIT)
    heads = []
    for g in range(DA_HEADS):
        o = ya_ref[rs, g * DA_V_DIM:(g + 1) * DA_V_DIM].astype(F32)
        heads.append((o * _rms_scale(o) * gain).astype(BF16))
    ya = jnp.concatenate(heads, axis=1)
    part = _dot(ys_ref[rs, :], wo_ref[0:SSD_INNER, :])
    yield
    part = part + _dot(ya, wo_ref[SSD_INNER:, :])
    yield
    h_ref[rs, :] = x_ref[rs, :] + part
    h = h_ref[rs, :]
    n2 = (h * _rms_scale(h) * nw_ref[...]).astype(BF16)
    ffn = None
    for f0, f1 in zip(FF_SPLITS[:-1], FF_SPLITS[1:]):
        gate = _dot(n2, wg_ref[:, f0:f1])
        yield
        up = _dot(n2, wu_ref[:, f0:f1])
        yield
        act = (_silu(gate) * up).astype(BF16)
        down = _dot(act, wd_ref[f0:f1, :])
        yield
        ffn = down if ffn is None else ffn + down
    out = h_ref[rs, :] + ffn
    o_ref[rs, :] = out * _rms_scale(out) * fw_ref[...]


def _mlp(x2, ys, ya, sw, wo, nw, wg, wu, wd, fw):
    t = x2.shape[0]
    tm = TM_PROJ
    row = pl.BlockSpec((tm, D_MODEL), lambda i: (i, 0))
    consts = (sw, wo, nw, wg, wu, wd, fw)
    return pl.pallas_call(
        _mlp_kernel,
        grid=(t // tm,),
        in_specs=[row, row, row] + [_const_spec(a.shape) for a in consts],
        out_specs=row,
        out_shape=jax.ShapeDtypeStruct((t, D_MODEL), F32),
        scratch_shapes=[pltpu.VMEM((tm, D_MODEL), F32)],
        compiler_params=pltpu.CompilerParams(
            dimension_semantics=("arbitrary",), vmem_limit_bytes=VMEM_LIMIT),
        name="mlp",
    )(x2, ys, ya, *consts)


def kernel(x, mix_norm_w, w_in, conv_w, conv_b, dt_bias, a_log, d_skip, ssd_norm_w, lam_q1, lam_k1, lam_q2,
           lam_k2, subln_w, w_out, ffn_norm_w, w_gate, w_up, w_down, final_norm_w):
    bsz, seqlen, _ = x.shape
    x2 = x.reshape(bsz * seqlen, D_MODEL)

    (z, xbc, dt_t, k, qt, vt), (wo16, wg16, wu16, wd16) = _in_proj(
        x2, mix_norm_w[0][None, :], w_in[0].T.astype(BF16), (w_out[0], w_gate[0], w_up[0], w_down[0]))

    idx = jnp.arange(SSD_CHUNK)
    tri = (idx[:, None] >= idx[None, :]).astype(BF16)
    tri3 = jnp.concatenate([tri.T, tri.T, tri.T], axis=0)
    sel = (jnp.arange(LANES)[:, None] == (jnp.arange(SSD_INNER)[None, :] // SSD_HEAD_DIM)).astype(BF16)
    exp2 = jnp.concatenate([sel, sel], axis=0)
    src = CONV_HALO + idx[None, :, None] - jnp.arange(1, SSD_CONV)[:, None, None]
    shift = (jnp.arange(CONV_HALO + SSD_CHUNK)[None, None, :] == src).astype(BF16)
    shift = shift.reshape((SSD_CONV - 1) * SSD_CHUNK, CONV_HALO + SSD_CHUNK)
    dskip_x = jnp.repeat(d_skip[0].astype(F32), SSD_HEAD_DIM)[None, :]

    per_head = lambda v: jnp.broadcast_to(v.astype(F32)[:, None], (SSD_HEADS, SSD_CHUNK))
    y_ssd = _ssd(z, xbc, dt_t, conv_w[0], conv_b[0][None, :], per_head(dt_bias[0]), per_head(a_log[0]),
                 dskip_x, ssd_norm_w[0][None, :], tri3, exp2, shift, bsz, seqlen)
    y_da = _attention(qt, k, vt, lam_q1[0][None, :], lam_k1[0][None, :], lam_q2[0][None, :],
                      lam_k2[0][None, :], bsz, seqlen)

    out = _mlp(x2, y_ssd, y_da, subln_w[0][None, :], wo16, ffn_norm_w[0][None, :], wg16, wu16, wd16,
               final_norm_w[None, :])
    return out.reshape(bsz, seqlen, D_MODEL)
```

```python
import math

import jax
import jax.numpy as jnp
from jax import lax
from jax.experimental import pallas as pl
from jax.experimental.pallas import tpu as pltpu

F32 = jnp.float32
BF16 = jnp.bfloat16

EPS = 1e-5
D_MODEL = 1024
SSD_HEADS = 16
SSD_HEAD_DIM = 64
SSD_INNER = SSD_HEADS * SSD_HEAD_DIM
SSD_GROUPS = 2
SSD_GROUP_WIDTH = SSD_INNER // SSD_GROUPS
SSD_STATE = 128
SSD_CONV = 4
SSD_CHUNK = 128
SSD_BC = SSD_GROUPS * SSD_STATE
SSD_CONV_DIM = SSD_INNER + 2 * SSD_BC
DA_HEADS = 8
DA_HEAD_DIM = 64
DA_V_DIM = 2 * DA_HEAD_DIM
DA_WIDTH = DA_HEADS * DA_V_DIM
D_FF = 2816
LAMBDA_INIT = 0.8 - 0.6 * math.exp(-0.3 * 0)

LANES = 128
CONV_HALO = 16
CONV_LANES = 512
VMEM_LIMIT = 56 * 1024 * 1024

TM_PROJ = 512
TM_IN_PROJ = 1024
TQ = 256
TK = 256
MXU_WIDTH = 256
FF_SPLITS = (0, 6 * MXU_WIDTH, D_FF)
HEADS_PER_STEP = 8
Q_TILES_PER_STEP = 2
KEY_TILES_PER_TRIP = 3
SSD_ROWS_PER_STEP = 4
SSD_CHUNKS_PER_STEP = 4
SCORE_LOOKAHEAD = 2
ONES_ROWS = 16
ACC_ROWS = DA_V_DIM + ONES_ROWS
LOG2E = math.log2(math.e)


def _const_spec(shape):
    nd = len(shape)
    return pl.BlockSpec(shape, lambda *_: (0,) * nd, pipeline_mode=pl.Buffered(1))


def _rms_scale(xf):
    return lax.rsqrt(jnp.mean(xf * xf, axis=-1, keepdims=True) + EPS)


def _dot(a, b):
    return jnp.dot(a, b, preferred_element_type=F32)


def _dot_nt(a, b):
    return lax.dot_general(a, b, (((1,), (1,)), ((), ())), preferred_element_type=F32)


def _split3(a):
    hi = a.astype(BF16)
    r1 = a - hi.astype(F32)
    mid = r1.astype(BF16)
    lo = (r1 - mid.astype(F32)).astype(BF16)
    return hi, mid, lo


def _silu_of_twice(h):
    return h + h * jnp.tanh(h)


def _silu(x):
    return _silu_of_twice(0.5 * x)


def _in_proj_kernel(x_ref, nw_ref, wt_ref, *refs):
    n_cast = (len(refs) - 6) // 2
    cast_in, (z_ref, xbc_ref, dt_ref, k_ref, qt_ref, vt_ref), cast_out = (
        refs[:n_cast], refs[n_cast:n_cast + 6], refs[n_cast + 6:])
    for src, dst in zip(cast_in, cast_out):
        dst[...] = src[...].astype(BF16)
    xf = x_ref[...]
    xn = (xf * _rms_scale(xf) * nw_ref[...]).astype(BF16)
    r0 = 0
    for ref, width, scale, transposed in (
            (z_ref, SSD_INNER, 0.5, False), (xbc_ref, SSD_CONV_DIM, None, False), (dt_ref, SSD_HEADS, None, True),
            (qt_ref, DA_WIDTH, DA_HEAD_DIM ** -0.5 * LOG2E, True), (k_ref, DA_WIDTH, None, False),
            (vt_ref, DA_WIDTH, None, True)):
        w = wt_ref[r0:r0 + width, :]
        r = _dot_nt(w, xn) if transposed else _dot_nt(xn, w)
        ref[...] = (r if scale is None else r * scale).astype(ref.dtype)
        r0 += width


def _in_proj(x2, nw, wt, to_cast):
    t = x2.shape[0]
    tm = TM_IN_PROJ
    steps = t // tm
    row = lambda w: pl.BlockSpec((tm, w), lambda i: (i, 0))
    col = pl.BlockSpec((DA_WIDTH, tm), lambda i: (0, i))
    cast_specs = [pl.BlockSpec((a.shape[0] // steps, a.shape[1]), lambda i: (i, 0)) for a in to_cast]
    outs = pl.pallas_call(
        _in_proj_kernel,
        grid=(steps,),
        in_specs=[row(D_MODEL), _const_spec(nw.shape), _const_spec(wt.shape)] + cast_specs,
        out_specs=[row(SSD_INNER), row(SSD_CONV_DIM), pl.BlockSpec((SSD_HEADS, tm), lambda i: (0, i)),
                   row(DA_WIDTH), col, col] + cast_specs,
        out_shape=[
            jax.ShapeDtypeStruct((t, SSD_INNER), BF16),
            jax.ShapeDtypeStruct((t, SSD_CONV_DIM), BF16),
            jax.ShapeDtypeStruct((SSD_HEADS, t), F32),
            jax.ShapeDtypeStruct((t, DA_WIDTH), BF16),
            jax.ShapeDtypeStruct((DA_WIDTH, t), BF16),
            jax.ShapeDtypeStruct((DA_WIDTH, t), BF16),
        ] + [jax.ShapeDtypeStruct(a.shape, BF16) for a in to_cast],
        compiler_params=pltpu.CompilerParams(
            dimension_semantics=("arbitrary",), vmem_limit_bytes=VMEM_LIMIT),
        name="in_proj",
    )(x2, nw, wt, *to_cast)
    return outs[:6], outs[6:]


def _ssd_kernel(z_ref, xbc_ref, dt_ref, cw_ref, cb_ref, dtb_ref, alog_ref, dskip_ref, nw_ref,
                tri3_ref, exp2_ref, shift_ref, y_ref, ext_ref, conv_ref, state_ref, yacc_ref):
    L = SSD_CHUNK
    c = pl.program_id(1)

    @pl.when(c == 0)
    def _():
        state_ref[...] = jnp.zeros_like(state_ref)
        ext_ref[:, 0:CONV_HALO, :] = jnp.zeros((ext_ref.shape[0], CONV_HALO, SSD_CONV_DIM), BF16)

    @pl.when(c > 0)
    def _():
        ext_ref[:, 0:CONV_HALO, :] = ext_ref[:, SSD_CHUNKS_PER_STEP * L:SSD_CHUNKS_PER_STEP * L + CONV_HALO, :]

    consts = (cw_ref, cb_ref, dtb_ref, alog_ref, dskip_ref, nw_ref, tri3_ref, exp2_ref, shift_ref)

    def row_chunks(b):
        for n in range(SSD_CHUNKS_PER_STEP):
            t = pl.ds(n * L, L)
            yield from _ssd_chunk(z_ref.at[b, t], xbc_ref.at[b, t], dt_ref.at[b, :, t], *consts, y_ref.at[b, t],
                                  ext_ref.at[b, pl.ds(n * L, CONV_HALO + L)], conv_ref.at[b], state_ref.at[b],
                                  yacc_ref.at[b])

    _round_robin([row_chunks(b) for b in range(z_ref.shape[0])])


def _round_robin(stages):
    stages = list(stages)
    while stages:
        for gen in list(stages):
            if next(gen, StopIteration) is StopIteration:
                stages.remove(gen)


def _ssd_chunk(z_ref, xbc_ref, dt_ref, cw_ref, cb_ref, dtb_ref, alog_ref, dskip_ref, nw_ref,
               tri3_ref, exp2_ref, shift_ref, y_ref, ext_ref, conv_ref, state_ref, yacc_ref):
    L = SSD_CHUNK
    ext_ref[CONV_HALO:CONV_HALO + L, :] = xbc_ref[...]

    for c0 in range(0, SSD_CONV_DIM, CONV_LANES):
        cols = slice(c0, c0 + CONV_LANES)
        e = ext_ref[:, cols]
        cwh = 0.5 * cw_ref[:, cols]
        acc = 0.5 * cb_ref[:, cols] + cwh[SSD_CONV - 1:SSD_CONV, :] * e[CONV_HALO:, :].astype(F32)
        shifted = _dot(shift_ref[...], e)
        for back in range(1, SSD_CONV):
            j = SSD_CONV - 1 - back
            acc = acc + cwh[j:j + 1, :] * shifted[(back - 1) * L:back * L, :]
        conv_ref[:, cols] = _silu_of_twice(acc)

    dtr = dt_ref[...] + dtb_ref[...]
    dtv = jnp.maximum(dtr, 0.0) + jnp.log(1.0 + jnp.exp(-jnp.abs(dtr)))
    adt = dtv * (-jnp.exp(alog_ref[...]))
    cs_t = _dot(jnp.concatenate(_split3(adt), axis=1), tri3_ref[...]) * LOG2E
    yield
    csd_t = cs_t - jnp.log2(dtv)
    w_state_t = dtv * jnp.exp2(cs_t[:, L - 1:L] - cs_t)

    def by_time(a_t):
        return jnp.concatenate([a_t, jnp.zeros((LANES - SSD_HEADS, L), F32)], axis=0).T

    cs = by_time(cs_t)
    ecs = by_time(jnp.exp2(cs_t))
    w_state = by_time(w_state_t)

    def expand(a):
        hi = a.astype(BF16)
        lo = (a - hi.astype(F32)).astype(BF16)
        return _dot(jnp.concatenate([hi, lo], axis=1), exp2_ref[...])

    ecs_x = expand(ecs)
    wst_x = expand(w_state)
    yield

    row = lax.broadcasted_iota(jnp.int32, (L, L), 0)
    colm = lax.broadcasted_iota(jnp.int32, (L, L), 1)
    tril = row >= colm
    lane = lax.broadcasted_iota(jnp.int32, (L, LANES), 1)
    lo_half = lane < SSD_HEAD_DIM

    for g in range(SSD_GROUPS):
        gx = g * SSD_GROUP_WIDTH
        bm = conv_ref[:, SSD_INNER + g * SSD_STATE:SSD_INNER + (g + 1) * SSD_STATE]
        cm = conv_ref[:, SSD_INNER + SSD_BC + g * SSD_STATE:SSD_INNER + SSD_BC + (g + 1) * SSD_STATE]
        cm16 = cm.astype(BF16)
        cb = _dot_nt(cm16, bm.astype(BF16))
        xs_g = conv_ref[:, gx:gx + SSD_GROUP_WIDTH]

        y_off = _dot(cm16, state_ref[:, gx:gx + SSD_GROUP_WIDTH].astype(BF16))
        yield
        yacc_ref[:, gx:gx + SSD_GROUP_WIDTH] = (
            y_off * ecs_x[:, gx:gx + SSD_GROUP_WIDTH] + xs_g * dskip_ref[:, gx:gx + SSD_GROUP_WIDTH])

        for pair in range(SSD_HEADS // SSD_GROUPS // 2):
            ms = []
            h0 = g * (SSD_HEADS // SSD_GROUPS) + 2 * pair
            for h in (h0, h0 + 1):
                seg = cs[:, h:h + 1] - csd_t[h:h + 1, :]
                ms.append((cb * jnp.exp2(jnp.where(tril, seg, -jnp.inf))).astype(BF16))
            x_pair = conv_ref[:, gx + pair * LANES:gx + (pair + 1) * LANES]
            x_blk = jnp.concatenate(
                [jnp.where(lo_half, x_pair, 0.0), jnp.where(lo_half, 0.0, x_pair)], axis=0).astype(BF16)
            sl = slice(gx + pair * LANES, gx + (pair + 1) * LANES)
            y_diag = _dot(jnp.concatenate(ms, axis=1), x_blk)
            yield
            yacc_ref[:, sl] = yacc_ref[:, sl] + y_diag

        xd = (xs_g * wst_x[:, gx:gx + SSD_GROUP_WIDTH]).astype(BF16)
        contrib = _dot(bm.T.astype(BF16), xd)
        yield
        state_ref[:, gx:gx + SSD_GROUP_WIDTH] = (
            state_ref[:, gx:gx + SSD_GROUP_WIDTH] * ecs_x[L - 1:L, gx:gx + SSD_GROUP_WIDTH] + contrib)

        z_half = z_ref[:, gx:gx + SSD_GROUP_WIDTH].astype(F32)
        gy = yacc_ref[:, gx:gx + SSD_GROUP_WIDTH] * _silu_of_twice(z_half)
        y_ref[:, gx:gx + SSD_GROUP_WIDTH] = (
            gy * _rms_scale(gy) * nw_ref[:, gx:gx + SSD_GROUP_WIDTH]).astype(y_ref.dtype)


def _ssd(z, xbc, dt_t, cw, cb, dtb, alog, dskip_x, nw, tri3, exp2, shift, bsz, seqlen):
    L = SSD_CHUNK * SSD_CHUNKS_PER_STEP
    nc = seqlen // L
    rb = SSD_ROWS_PER_STEP
    row = lambda w: pl.BlockSpec((rb, L, w), lambda r, c: (r, c, 0))
    dt_spec = pl.BlockSpec((rb, SSD_HEADS, L), lambda r, c: (r, 0, c))
    per_batch = lambda a: a.reshape(bsz, seqlen, a.shape[-1])
    consts = (cw, cb, dtb, alog, dskip_x, nw, tri3, exp2, shift)
    y = pl.pallas_call(
        _ssd_kernel,
        grid=(bsz // rb, nc),
        in_specs=[row(SSD_INNER), row(SSD_CONV_DIM), dt_spec] + [_const_spec(a.shape) for a in consts],
        out_specs=row(SSD_INNER),
        out_shape=jax.ShapeDtypeStruct((bsz, seqlen, SSD_INNER), BF16),
        scratch_shapes=[
            pltpu.VMEM((rb, CONV_HALO + L, SSD_CONV_DIM), BF16),
            pltpu.VMEM((rb, SSD_CHUNK, SSD_CONV_DIM), F32),
            pltpu.VMEM((rb, SSD_STATE, SSD_INNER), F32),
            pltpu.VMEM((rb, SSD_CHUNK, SSD_INNER), F32),
        ],
        compiler_params=pltpu.CompilerParams(
            dimension_semantics=("arbitrary", "arbitrary"), vmem_limit_bytes=VMEM_LIMIT),
        name="ssd",
    )(per_batch(z), per_batch(xbc), dt_t.reshape(SSD_HEADS, bsz, seqlen).transpose(1, 0, 2), *consts)
    return y.reshape(bsz * seqlen, SSD_INNER)


def _attn_kernel(qt_ref, k_ref, vt_ref, lq1_ref, lk1_ref, lq2_ref, lk2_ref, o_ref,
                 qs_ref, acc_ref, m_ref, mx_ref, s_ref):
    for t in range(Q_TILES_PER_STEP):
        _attn_query_tile(pl.program_id(2) * Q_TILES_PER_STEP + t, slice(t * TQ, (t + 1) * TQ),
                         qt_ref, k_ref, vt_ref, lq1_ref, lk1_ref, lq2_ref, lk2_ref, o_ref,
                         qs_ref, acc_ref, m_ref, mx_ref, s_ref)


def _attn_query_tile(i, cols, qt_ref, k_ref, vt_ref, lq1_ref, lk1_ref, lq2_ref, lk2_ref, o_ref,
                     qs_ref, acc_ref, m_ref, mx_ref, s_ref):
    d_idx = lax.broadcasted_iota(jnp.int32, (DA_V_DIM, TQ), 0)
    ones_rows = jnp.ones((ONES_ROWS, TK), BF16)

    for g in range(HEADS_PER_STEP):
        rows = slice(g * DA_V_DIM, (g + 1) * DA_V_DIM)
        q = qt_ref[rows, cols]
        zero = jnp.zeros_like(q)
        qs_ref[rows, 0:TQ] = jnp.where(d_idx < DA_HEAD_DIM, q, zero)
        qs_ref[rows, TQ:2 * TQ] = jnp.where(d_idx < DA_HEAD_DIM, zero, q)

    def scores(g, j, diagonal):
        rows = slice(g * DA_V_DIM, (g + 1) * DA_V_DIM)
        off = pl.multiple_of(j * TK, TK)
        s = _dot(k_ref[pl.ds(off, TK), rows], qs_ref[rows, :])
        if diagonal:
            key = lax.broadcasted_iota(jnp.int32, s.shape, 0)
            qry = lax.broadcasted_iota(jnp.int32, s.shape, 1) & (TQ - 1)
            s = jnp.where(key <= qry, s, -jnp.inf)
        s_ref[g] = s
        mx_ref[g:g + 1, :] = jnp.max(s, axis=0, keepdims=True)

    def softmax_pv(g, j, first):
        rows = slice(g * DA_V_DIM, (g + 1) * DA_V_DIM)
        arow = slice(g * ACC_ROWS, (g + 1) * ACC_ROWS)
        off = pl.multiple_of(j * TK, TK)
        if first:
            m_new = mx_ref[g:g + 1, :]
        else:
            m_prev = m_ref[g:g + 1, :]
            m_new = jnp.maximum(m_prev, mx_ref[g:g + 1, :])
        p = jnp.exp2(s_ref[g] - m_new).astype(BF16)
        v_ext = jnp.concatenate([vt_ref[rows, pl.ds(off, TK)], ones_rows], axis=0)
        pv = _dot(v_ext, p)
        if first:
            acc_ref[arow, :] = pv
        else:
            acc_ref[arow, :] = jnp.exp2(m_prev - m_new) * acc_ref[arow, :] + pv
        m_ref[g:g + 1, :] = m_new

    def run_tiles(tiles, diagonal, next_tile):
        chains = [(t, g) for t in tiles for g in range(HEADS_PER_STEP)]
        for n, (t, g) in enumerate(chains):
            ahead = n + SCORE_LOOKAHEAD
            if ahead < len(chains):
                scores(chains[ahead][1], chains[ahead][0], diagonal)
            else:
                scores(ahead - len(chains), next_tile, False)
            softmax_pv(g, t, first=diagonal)

    for g in range(SCORE_LOOKAHEAD):
        scores(g, i, True)
    run_tiles([i], diagonal=True, next_tile=0)

    def body(jj, carry):
        t0 = KEY_TILES_PER_TRIP * jj
        run_tiles([t0 + n for n in range(KEY_TILES_PER_TRIP)], diagonal=False, next_tile=t0 + KEY_TILES_PER_TRIP)
        return carry

    lax.fori_loop(0, i // KEY_TILES_PER_TRIP, body, 0)

    for rest in range(1, KEY_TILES_PER_TRIP):
        @pl.when(i % KEY_TILES_PER_TRIP == rest)
        def _(rest=rest):
            run_tiles([i - rest + n for n in range(rest)], diagonal=False, next_tile=i)

    lam = (jnp.exp(jnp.sum(lq1_ref[...] * lk1_ref[...], axis=1, keepdims=True))
           - jnp.exp(jnp.sum(lq2_ref[...] * lk2_ref[...], axis=1, keepdims=True)) + LAMBDA_INIT)
    for g in range(HEADS_PER_STEP):
        a0 = g * ACC_ROWS
        inv = 1.0 / acc_ref[a0 + DA_V_DIM:a0 + DA_V_DIM + 1, :]
        o = (acc_ref[a0:a0 + DA_V_DIM, 0:TQ] * inv[:, 0:TQ]
             - acc_ref[a0:a0 + DA_V_DIM, TQ:2 * TQ] * (lam * inv[:, TQ:2 * TQ]))
        o_ref[cols, g * DA_V_DIM:(g + 1) * DA_V_DIM] = o.T.astype(o_ref.dtype)


def _attention(qt, k, vt, lq1, lk1, lq2, lk2, bsz, seqlen):
    assert TQ == TK
    tq = Q_TILES_PER_STEP * TQ
    nq = seqlen // tq
    gw = HEADS_PER_STEP * DA_V_DIM
    small = (lq1, lk1, lq2, lk2)
    return pl.pallas_call(
        _attn_kernel,
        grid=(bsz, DA_HEADS // HEADS_PER_STEP, nq),
        in_specs=[
            pl.BlockSpec((gw, tq), lambda b, h, i: (h, b * nq + i)),
            pl.BlockSpec((seqlen, gw), lambda b, h, i: (b, h)),
            pl.BlockSpec((gw, seqlen), lambda b, h, i: (h, b)),
        ] + [_const_spec(a.shape) for a in small],
        out_specs=pl.BlockSpec((tq, gw), lambda b, h, i: (b * nq + i, h)),
        out_shape=jax.ShapeDtypeStruct((bsz * seqlen, DA_WIDTH), BF16),
        scratch_shapes=[
            pltpu.VMEM((gw, 2 * TQ), BF16),
            pltpu.VMEM((HEADS_PER_STEP * ACC_ROWS, 2 * TQ), F32),
            pltpu.VMEM((HEADS_PER_STEP, 2 * TQ), F32),
            pltpu.VMEM((HEADS_PER_STEP, 2 * TQ), F32),
            pltpu.VMEM((HEADS_PER_STEP, TK, 2 * TQ), F32),
        ],
        compiler_params=pltpu.CompilerParams(
            dimension_semantics=("arbitrary", "arbitrary", "arbitrary"), vmem_limit_bytes=VMEM_LIMIT),
        name="diff_attn",
    )(qt, k, vt, *small)


def _mlp_kernel(x_ref, ys_ref, ya_ref, sw_ref, wo_ref, nw_ref, wg_ref, wu_ref, wd_ref, fw_ref, o_ref, h_ref):
    gain = sw_ref[...] * (1.0 - LAMBDA_INIT)
    heads = []
    for g in range(DA_HEADS):
        o = ya_ref[:, g * DA_V_DIM:(g + 1) * DA_V_DIM].astype(F32)
        heads.append((o * _rms_scale(o) * gain).astype(BF16))
    ya = jnp.concatenate(heads, axis=1)
    h_ref[...] = x_ref[...] + _dot(ys_ref[...], wo_ref[0:SSD_INNER, :]) + _dot(ya, wo_ref[SSD_INNER:, :])
    h = h_ref[...]
    n2 = (h * _rms_scale(h) * nw_ref[...]).astype(BF16)
    ffn = None
    for f0, f1 in zip(FF_SPLITS[:-1], FF_SPLITS[1:]):
        gate = _dot(n2, wg_ref[:, f0:f1])
        up = _dot(n2, wu_ref[:, f0:f1])
        act = (_silu(gate) * up).astype(BF16)
        down = _dot(act, wd_ref[f0:f1, :])
        ffn = down if ffn is None else ffn + down
    out = h_ref[...] + ffn
    o_ref[...] = out * _rms_scale(out) * fw_ref[...]


def _mlp(x2, ys, ya, sw, wo, nw, wg, wu, wd, fw):
    t = x2.shape[0]
    tm = TM_PROJ
    row = pl.BlockSpec((tm, D_MODEL), lambda i: (i, 0))
    consts = (sw, wo, nw, wg, wu, wd, fw)
    return pl.pallas_call(
        _mlp_kernel,
        grid=(t // tm,),
        in_specs=[row, row, row] + [_const_spec(a.shape) for a in consts],
        out_specs=row,
        out_shape=jax.ShapeDtypeStruct((t, D_MODEL), F32),
        scratch_shapes=[pltpu.VMEM((tm, D_MODEL), F32)],
        compiler_params=pltpu.CompilerParams(
            dimension_semantics=("arbitrary",), vmem_limit_bytes=VMEM_LIMIT),
        name="mlp",
    )(x2, ys, ya, *consts)


def kernel(x, mix_norm_w, w_in, conv_w, conv_b, dt_bias, a_log, d_skip, ssd_norm_w, lam_q1, lam_k1, lam_q2,
           lam_k2, subln_w, w_out, ffn_norm_w, w_gate, w_up, w_down, final_norm_w):
    bsz, seqlen, _ = x.shape
    x2 = x.reshape(bsz * seqlen, D_MODEL)

    (z, xbc, dt_t, k, qt, vt), (wo16, wg16, wu16, wd16) = _in_proj(
        x2, mix_norm_w[0][None, :], w_in[0].T.astype(BF16), (w_out[0], w_gate[0], w_up[0], w_down[0]))

    idx = jnp.arange(SSD_CHUNK)
    tri = (idx[:, None] >= idx[None, :]).astype(BF16)
    tri3 = jnp.concatenate([tri.T, tri.T, tri.T], axis=0)
    sel = (jnp.arange(LANES)[:, None] == (jnp.arange(SSD_INNER)[None, :] // SSD_HEAD_DIM)).astype(BF16)
    exp2 = jnp.concatenate([sel, sel], axis=0)
    src = CONV_HALO + idx[None, :, None] - jnp.arange(1, SSD_CONV)[:, None, None]
    shift = (jnp.arange(CONV_HALO + SSD_CHUNK)[None, None, :] == src).astype(BF16)
    shift = shift.reshape((SSD_CONV - 1) * SSD_CHUNK, CONV_HALO + SSD_CHUNK)
    dskip_x = jnp.repeat(d_skip[0].astype(F32), SSD_HEAD_DIM)[None, :]

    per_head = lambda v: jnp.broadcast_to(v.astype(F32)[:, None], (SSD_HEADS, SSD_CHUNK))
    y_ssd = _ssd(z, xbc, dt_t, conv_w[0], conv_b[0][None, :], per_head(dt_bias[0]), per_head(a_log[0]),
                 dskip_x, ssd_norm_w[0][None, :], tri3, exp2, shift, bsz, seqlen)
    y_da = _attention(qt, k, vt, lam_q1[0][None, :], lam_k1[0][None, :], lam_q2[0][None, :],
                      lam_k2[0][None, :], bsz, seqlen)

    out = _mlp(x2, y_ssd, y_da, subln_w[0][None, :], wo16, ffn_norm_w[0][None, :], wg16, wu16, wd16,
               final_norm_w[None, :])
    return out.reshape(bsz, seqlen, D_MODEL)
```
